```python
import math
import jax, jax.numpy as jnp
from jax import lax
import numpy as np

D_MODEL = 2048
BATCH = 2
SEQ = 8192
DEPTH = 2

CHUNK = 64
N_META = 16

DN_ALPHA = (2 * DEPTH) ** 0.25
DN_BETA = (8 * DEPTH) ** -0.25
LN_EPS = 1e-5
RMS_EPS = 1e-6

S5_WIDTH = D_MODEL // 2
S5_GROUP = 16
S5_GROUPS = S5_WIDTH // S5_GROUP
S5_STATE = 64

MLA_HEADS = D_MODEL // 256
MLA_NOPE = 128
MLA_ROPE = 64
MLA_V = 128
MLA_Q_RANK = D_MODEL // 4
MLA_KV_RANK = D_MODEL // 8
ROPE_BASE = 10000.0
Q_BLOCK = 128

L0_IN = S5_WIDTH + MLA_Q_RANK + MLA_KV_RANK + MLA_ROPE
L0_MIX = S5_WIDTH + MLA_HEADS * MLA_V

SSD_INNER = 2 * D_MODEL
SSD_HEAD_DIM = 64
SSD_HEADS = SSD_INNER // SSD_HEAD_DIM
SSD_GROUPS = 8
SSD_HPG = SSD_HEADS // SSD_GROUPS
SSD_STATE = 128
SSD_CONV = 4
SSD_BLOCK = 128
SSD_CONV_DIM = SSD_INNER + 2 * SSD_GROUPS * SSD_STATE
L1_IN = SSD_INNER + SSD_CONV_DIM + SSD_HEADS

FFN_HIDDEN = -(-(8 * D_MODEL) // (3 * 256)) * 256

kernel_name = "hybrid_s5_mla_ssd_deepnorm_encoder"


def layer_norm(x, g, b):
    xf = x.astype(jnp.float32)
    mu = jnp.mean(xf, axis=-1, keepdims=True)
    var = jnp.mean(jnp.square(xf - mu), axis=-1, keepdims=True)
    y = (xf - mu) * lax.rsqrt(var + LN_EPS) * g.astype(jnp.float32) + b.astype(jnp.float32)
    return y.astype(x.dtype)


def rms_norm(x, g):
    xf = x.astype(jnp.float32)
    y = xf * lax.rsqrt(jnp.mean(jnp.square(xf), axis=-1, keepdims=True) + RMS_EPS)
    return (y * g.astype(jnp.float32)).astype(x.dtype)


def chunk_ids(n):
    p = jnp.arange(n)
    return jnp.where(p < N_META, 0, 1 + (p - N_META) // CHUNK)


def rope_tables(n):
    pos = jnp.arange(n, dtype=jnp.float32)
    inv = ROPE_BASE ** (-jnp.arange(0, MLA_ROPE, 2, dtype=jnp.float32) / MLA_ROPE)
    ang = pos[:, None] * inv[None, :]
    return jnp.cos(ang), jnp.sin(ang)


def apply_rope(x, cos, sin):
    half = x.shape[-1] // 2
    x1, x2 = x[..., :half], x[..., half:]
    cos = cos.astype(x.dtype)
    sin = sin.astype(x.dtype)
    return jnp.concatenate([x1 * cos - x2 * sin, x2 * cos + x1 * sin], axis=-1)


def _complex_affine_combine(e1, e2):
    a1r, a1i, b1r, b1i = e1
    a2r, a2i, b2r, b2i = e2
    ar = a2r * a1r - a2i * a1i
    ai = a2r * a1i + a2i * a1r
    br = a2r * b1r - a2i * b1i + b2r
    bi = a2r * b1i + a2i * b1r + b2i
    return ar, ai, br, bi


def s5_mixer(u, log_dt, a_re, a_im, b_re, b_im, c_re, c_im, d, w_glu):
    bsz, n, _ = u.shape
    ug = u.reshape(bsz, n, S5_GROUPS, S5_GROUP)
    dt = jnp.exp(log_dt)[:, None]
    mag = jnp.exp(dt * a_re)
    ab_re = mag * jnp.cos(dt * a_im)
    ab_im = mag * jnp.sin(dt * a_im)
    den = a_re * a_re + a_im * a_im
    nr = ab_re - 1.0
    f_re = (nr * a_re + ab_im * a_im) / den
    f_im = (ab_im * a_re - nr * a_im) / den
    bb_re = f_re[..., None] * b_re - f_im[..., None] * b_im
    bb_im = f_re[..., None] * b_im + f_im[..., None] * b_re
    bu_re = jnp.einsum('blgj,gpj->lbgp', ug, bb_re)
    bu_im = jnp.einsum('blgj,gpj->lbgp', ug, bb_im)
    a_seq_re = jnp.broadcast_to(ab_re, (n, 1) + ab_re.shape)
    a_seq_im = jnp.broadcast_to(ab_im, (n, 1) + ab_im.shape)
    _, _, s_re, s_im = lax.associative_scan(
        _complex_affine_combine, (a_seq_re, a_seq_im, bu_re, bu_im), axis=0)
    y = (jnp.einsum('lbgp,gjp->blgj', s_re, c_re)
         - jnp.einsum('lbgp,gjp->blgj', s_im, c_im))
    y = y.reshape(bsz, n, S5_WIDTH) + d * u
    g = jax.nn.gelu(y)
    return g * jax.nn.sigmoid(g @ w_glu)


def mla_mixer(q_lat, kv_lat, k_rope_raw, q_norm, w_uq, kv_norm, w_ukv, cos, sin, cid):
    bsz, n, _ = q_lat.shape
    q = (rms_norm(q_lat, q_norm) @ w_uq).reshape(bsz, n, MLA_HEADS, MLA_NOPE + MLA_ROPE)
    q_nope = q[..., :MLA_NOPE]
    q_rope = apply_rope(q[..., MLA_NOPE:], cos[:, None, :], sin[:, None, :])
    kv = (rms_norm(kv_lat, kv_norm) @ w_ukv).reshape(bsz, n, MLA_HEADS, MLA_NOPE + MLA_V)
    k_nope = kv[..., :MLA_NOPE]
    v = kv[..., MLA_NOPE:]
    k_rope = apply_rope(k_rope_raw, cos, sin)
    n_pad = -(-n // Q_BLOCK) * Q_BLOCK
    nb = n_pad // Q_BLOCK
    pad = ((0, 0), (0, n_pad - n), (0, 0), (0, 0))
    qn_b = jnp.pad(q_nope, pad).reshape(bsz, nb, Q_BLOCK, MLA_HEADS, MLA_NOPE).transpose(1, 0, 2, 3, 4)
    qr_b = jnp.pad(q_rope, pad).reshape(bsz, nb, Q_BLOCK, MLA_HEADS, MLA_ROPE).transpose(1, 0, 2, 3, 4)
    cid_q = chunk_ids(n_pad).reshape(nb, Q_BLOCK)
    scale = (MLA_NOPE + MLA_ROPE) ** -0.5

    def attend(blk):
        qn, qr, cq = blk
        s = (jnp.einsum('bqhd,bkhd->bhqk', qn, k_nope)
             + jnp.einsum('bqhd,bkd->bhqk', qr, k_rope))
        s = s.astype(jnp.float32) * scale
        visible = cid[None, :] <= cq[:, None]
        s = jnp.where(visible, s, -jnp.inf)
        p = jax.nn.softmax(s, axis=-1).astype(v.dtype)
        return jnp.einsum('bhqk,bkhd->bqhd', p, v)

    o = lax.map(attend, (qn_b, qr_b, cid_q))
    o = o.transpose(1, 0, 2, 3, 4).reshape(bsz, n_pad, MLA_HEADS * MLA_V)
    return o[:, :n]


def s5_mla_mixer(h, cos, sin, cid, w_in, s5_log_dt, s5_a_re, s5_a_im, s5_b_re, s5_b_im,
                 s5_c_re, s5_c_im, s5_d, s5_w_glu, mla_q_norm, mla_w_uq, mla_kv_norm,
                 mla_w_ukv, w_out):
    proj = h @ w_in
    o1 = S5_WIDTH
    o2 = o1 + MLA_Q_RANK
    o3 = o2 + MLA_KV_RANK
    a_out = s5_mixer(proj[..., :o1], s5_log_dt, s5_a_re, s5_a_im, s5_b_re, s5_b_im,
                     s5_c_re, s5_c_im, s5_d, s5_w_glu)
    b_out = mla_mixer(proj[..., o1:o2], proj[..., o2:o3], proj[..., o3:], mla_q_norm,
                      mla_w_uq, mla_kv_norm, mla_w_ukv, cos, sin, cid)
    return jnp.concatenate([a_out, b_out], axis=-1) @ w_out


def causal_depthwise_conv(x, w, b):
    k = w.shape[0]
    y = lax.conv_general_dilated(x, w[:, None, :], window_strides=(1,), padding=((k - 1, 0),),
                                 dimension_numbers=('NWC', 'WIO', 'NWC'),
                                 feature_group_count=x.shape[-1])
    return y + b


def ssd_scan(x, dt, a, bm, cm):
    bsz, n, g, r, p = x.shape
    q = SSD_BLOCK
    nc = n // q
    x = x.reshape(bsz, nc, q, g, r, p)
    dt = dt.reshape(bsz, nc, q, g, r)
    bm = bm.reshape(bsz, nc, q, g, -1)
    cm = cm.reshape(bsz, nc, q, g, -1)
    xdt = x * dt[..., None]
    da = jnp.moveaxis(dt * a, 2, -1)
    cs = jnp.cumsum(da, axis=-1)
    tri = jnp.tril(jnp.ones((q, q), dtype=bool))
    seg = cs[..., :, None] - cs[..., None, :]
    decay = jnp.exp(jnp.where(tri, seg, -jnp.inf))
    cb = jnp.einsum('bclgn,bcsgn->bcgls', cm, bm)
    y_diag = jnp.einsum('bcgls,bcgrls,bcsgrp->bclgrp', cb, decay, xdt)
    decay_states = jnp.exp(cs[..., -1:] - cs)
    states = jnp.einsum('bclgn,bcgrl,bclgrp->bcgrpn', bm, decay_states, xdt)
    chunk_decay = jnp.exp(cs[..., -1])

    def carry_step(hs, inp):
        s_c, a_c = inp
        return hs * a_c[..., None, None] + s_c, hs

    h0 = jnp.zeros_like(states[:, 0])
    _, prev = lax.scan(carry_step, h0, (jnp.moveaxis(states, 1, 0), jnp.moveaxis(chunk_decay, 1, 0)))
    y_off = jnp.einsum('bclgn,cbgrpn,bcgrl->bclgrp', cm, prev, jnp.exp(cs))
    return (y_diag + y_off).reshape(bsz, n, g, r, p)


def mamba2_mixer(h, w_in, conv_w, conv_b, dt_bias, a_log, d, norm_g, w_out):
    bsz, n, _ = h.shape
    zxbcdt = h @ w_in
    z = zxbcdt[..., :SSD_INNER]
    xbc = zxbcdt[..., SSD_INNER:SSD_INNER + SSD_CONV_DIM]
    dt_raw = zxbcdt[..., SSD_INNER + SSD_CONV_DIM:]
    xbc = jax.nn.silu(causal_depthwise_conv(xbc, conv_w, conv_b))
    gn = SSD_GROUPS * SSD_STATE
    xs = xbc[..., :SSD_INNER].reshape(bsz, n, SSD_GROUPS, SSD_HPG, SSD_HEAD_DIM)
    bm = xbc[..., SSD_INNER:SSD_INNER + gn].reshape(bsz, n, SSD_GROUPS, SSD_STATE)
    cm = xbc[..., SSD_INNER + gn:].reshape(bsz, n, SSD_GROUPS, SSD_STATE)
    dt = jax.nn.softplus(dt_raw + dt_bias).reshape(bsz, n, SSD_GROUPS, SSD_HPG)
    a = -jnp.exp(a_log).reshape(SSD_GROUPS, SSD_HPG)
    n_pad = -(-n // SSD_BLOCK) * SSD_BLOCK
    tp = n_pad - n
    y = ssd_scan(jnp.pad(xs, ((0, 0), (0, tp), (0, 0), (0, 0), (0, 0))),
                 jnp.pad(dt, ((0, 0), (0, tp), (0, 0), (0, 0))), a,
                 jnp.pad(bm, ((0, 0), (0, tp), (0, 0), (0, 0))),
                 jnp.pad(cm, ((0, 0), (0, tp), (0, 0), (0, 0))))[:, :n]
    y = y + d.reshape(SSD_GROUPS, SSD_HPG)[..., None] * xs
    y = y.reshape(bsz, n, SSD_INNER) * jax.nn.silu(z)
    y = rms_norm(y.reshape(bsz, n, SSD_GROUPS, SSD_INNER // SSD_GROUPS),
                 norm_g.reshape(SSD_GROUPS, SSD_INNER // SSD_GROUPS)).reshape(bsz, n, SSD_INNER)
    return y @ w_out


def swiglu_ffn(h, w_gate, w_up, w_down):
    return (jax.nn.silu(h @ w_gate) * (h @ w_up)) @ w_down


def setup_inputs(seed: int = 0) -> dict:
    key = jax.random.key(seed)
    ks = iter(jax.random.split(key, 64))

    def nrm(shape, scale):
        return jax.random.normal(next(ks), shape, jnp.float32) * scale

    def uni(shape, lo, hi):
        return jax.random.uniform(next(ks), shape, jnp.float32, lo, hi)

    d = D_MODEL
    f = FFN_HIDDEN
    G, P, J = S5_GROUPS, S5_STATE, S5_GROUP
    inp = {}
    inp['x'] = nrm((BATCH, SEQ, d), 1.0)
    inp['meta_tokens'] = nrm((N_META, d), 1.0)
    inp['l0_w_in'] = nrm((d, L0_IN), d ** -0.5)
    inp['l0_s5_log_dt'] = uni((G,), math.log(0.001), math.log(0.1))
    inp['l0_s5_a_re'] = -0.5 + nrm((G, P), 0.01)
    inp['l0_s5_a_im'] = jnp.pi * jnp.arange(P, dtype=jnp.float32)[None, :] + nrm((G, P), 0.01)
    inp['l0_s5_b_re'] = nrm((G, P, J), (2 * J) ** -0.5)
    inp['l0_s5_b_im'] = nrm((G, P, J), (2 * J) ** -0.5)
    inp['l0_s5_c_re'] = nrm((G, J, P), 0.5 ** 0.5)
    inp['l0_s5_c_im'] = nrm((G, J, P), 0.5 ** 0.5)
    inp['l0_s5_d'] = nrm((S5_WIDTH,), 1.0)
    inp['l0_s5_w_glu'] = nrm((S5_WIDTH, S5_WIDTH), S5_WIDTH ** -0.5)
    inp['l0_mla_q_norm'] = 1.0 + nrm((MLA_Q_RANK,), 0.02)
    inp['l0_mla_w_uq'] = nrm((MLA_Q_RANK, MLA_HEADS * (MLA_NOPE + MLA_ROPE)), MLA_Q_RANK ** -0.5)
    inp['l0_mla_kv_norm'] = 1.0 + nrm((MLA_KV_RANK,), 0.02)
    inp['l0_mla_w_ukv'] = nrm((MLA_KV_RANK, MLA_HEADS * (MLA_NOPE + MLA_V)), MLA_KV_RANK ** -0.5)
    inp['l0_w_out'] = nrm((L0_MIX, d), L0_MIX ** -0.5 * DN_BETA)
    inp['l0_ln1_g'] = 1.0 + nrm((d,), 0.02)
    inp['l0_ln1_b'] = nrm((d,), 0.02)
    inp['l0_ffn_w_gate'] = nrm((d, f), d ** -0.5)
    inp['l0_ffn_w_up'] = nrm((d, f), d ** -0.5 * DN_BETA)
    inp['l0_ffn_w_down'] = nrm((f, d), f ** -0.5 * DN_BETA)
    inp['l0_ln2_g'] = 1.0 + nrm((d,), 0.02)
    inp['l0_ln2_b'] = nrm((d,), 0.02)
    inp['l1_w_in'] = nrm((d, L1_IN), d ** -0.5)
    inp['l1_conv_w'] = nrm((SSD_CONV, SSD_CONV_DIM), SSD_CONV ** -0.5)
    inp['l1_conv_b'] = nrm((SSD_CONV_DIM,), 0.02)
    dt0 = jnp.exp(uni((SSD_HEADS,), math.log(0.001), math.log(0.1)))
    inp['l1_dt_bias'] = dt0 + jnp.log(-jnp.expm1(-dt0))
    inp['l1_a_log'] = jnp.log(uni((SSD_HEADS,), 1.0, 16.0))
    inp['l1_d'] = 1.0 + nrm((SSD_HEADS,), 0.02)
    inp['l1_norm_g'] = 1.0 + nrm((SSD_INNER,), 0.02)
    inp['l1_w_out'] = nrm((SSD_INNER, d), SSD_INNER ** -0.5 * DN_BETA)
    inp['l1_ln1_g'] = 1.0 + nrm((d,), 0.02)
    inp['l1_ln1_b'] = nrm((d,), 0.02)
    inp['l1_ffn_w_gate'] = nrm((d, f), d ** -0.5)
    inp['l1_ffn_w_up'] = nrm((d, f), d ** -0.5 * DN_BETA)
    inp['l1_ffn_w_down'] = nrm((f, d), f ** -0.5 * DN_BETA)
    inp['l1_ln2_g'] = 1.0 + nrm((d,), 0.02)
    inp['l1_ln2_b'] = nrm((d,), 0.02)
    return inp


def reference(x, meta_tokens,
              l0_w_in, l0_s5_log_dt, l0_s5_a_re, l0_s5_a_im, l0_s5_b_re, l0_s5_b_im,
              l0_s5_c_re, l0_s5_c_im, l0_s5_d, l0_s5_w_glu, l0_mla_q_norm, l0_mla_w_uq,
              l0_mla_kv_norm, l0_mla_w_ukv, l0_w_out, l0_ln1_g, l0_ln1_b,
              l0_ffn_w_gate, l0_ffn_w_up, l0_ffn_w_down, l0_ln2_g, l0_ln2_b,
              l1_w_in, l1_conv_w, l1_conv_b, l1_dt_bias, l1_a_log, l1_d, l1_norm_g,
              l1_w_out, l1_ln1_g, l1_ln1_b, l1_ffn_w_gate, l1_ffn_w_up, l1_ffn_w_down,
              l1_ln2_g, l1_ln2_b):
    bsz = x.shape[0]
    meta = jnp.broadcast_to(meta_tokens[None].astype(x.dtype), (bsz, N_META, D_MODEL))
    h = jnp.concatenate([meta, x], axis=1)
    n = h.shape[1]
    cos, sin = rope_tables(n)
    cid = chunk_ids(n)

    mixers = [
        lambda t: s5_mla_mixer(t, cos, sin, cid, l0_w_in, l0_s5_log_dt, l0_s5_a_re, l0_s5_a_im,
                               l0_s5_b_re, l0_s5_b_im, l0_s5_c_re, l0_s5_c_im, l0_s5_d,
                               l0_s5_w_glu, l0_mla_q_norm, l0_mla_w_uq, l0_mla_kv_norm,
                               l0_mla_w_ukv, l0_w_out),
        lambda t: mamba2_mixer(t, l1_w_in, l1_conv_w, l1_conv_b, l1_dt_bias, l1_a_log, l1_d,
                               l1_norm_g, l1_w_out),
    ]
    post = [
        (l0_ln1_g, l0_ln1_b, l0_ffn_w_gate, l0_ffn_w_up, l0_ffn_w_down, l0_ln2_g, l0_ln2_b),
        (l1_ln1_g, l1_ln1_b, l1_ffn_w_gate, l1_ffn_w_up, l1_ffn_w_down, l1_ln2_g, l1_ln2_b),
    ]
    for i in range(DEPTH):
        ln1_g, ln1_b, w_gate, w_up, w_down, ln2_g, ln2_b = post[i]
        h = layer_norm(DN_ALPHA * h + mixers[i](h), ln1_g, ln1_b)
        h = layer_norm(DN_ALPHA * h + swiglu_ffn(h, w_gate, w_up, w_down), ln2_g, ln2_b)
    return h[:, N_META:]
```

```python
import functools
import math

import jax
import jax.numpy as jnp
from jax import lax
from jax.experimental import pallas as pl
from jax.experimental.pallas import tpu as pltpu

F32 = jnp.float32
BF16 = jnp.bfloat16

DEPTH = 2
CHUNK = 64
N_META = 16
DN_ALPHA = (2 * DEPTH) ** 0.25
LN_EPS = 1e-5
RMS_EPS = 1e-6
S5_GROUP = 16
S5_STATE = 64
S5_Q = 16
MLA_HEADS = 8
MLA_NOPE = 128
MLA_ROPE = 64
MLA_V = 128
MLA_Q_RANK = 512
MLA_KV_RANK = 256
ROPE_BASE = 10000.0
SSD_HEAD_DIM = 64
SSD_GROUPS = 8
SSD_HPG = 8
SSD_STATE = 128
SSD_CONV = 4
SSD_BLOCK = 128

LANES = 128
SUBLANES = 8
VMEM_LIMIT = 56 * 1024 * 1024
MASK_VALUE = -1e30
HI = lax.Precision.HIGHEST


def _pick(n, candidates):
    for c in candidates:
        if n % c == 0:
            return c
    raise ValueError(f"no tile for {n} in {candidates}")


def _params(*sem):
    return pltpu.CompilerParams(dimension_semantics=sem, vmem_limit_bytes=VMEM_LIMIT)


def _mm_kernel(x_ref, w_ref, o_ref):
    o_ref[...] = jnp.dot(x_ref[...], w_ref[...], preferred_element_type=F32).astype(o_ref.dtype)


def _matmul(x, w, out_dtype, *, tm, tn, name):
    m, k = x.shape
    n = w.shape[1]
    return pl.pallas_call(
        _mm_kernel,
        grid=(m // tm, n // tn),
        in_specs=[pl.BlockSpec((tm, k), lambda i, j: (i, 0)),
                  pl.BlockSpec((k, tn), lambda i, j: (0, j))],
        out_specs=pl.BlockSpec((tm, tn), lambda i, j: (i, j)),
        out_shape=jax.ShapeDtypeStruct((m, n), out_dtype),
        compiler_params=_params("parallel", "arbitrary"),
        name=name,
    )(x, w)


def _mm_res_ln_kernel(x_ref, w_ref, res_ref, g_ref, b_ref, o32_ref, o16_ref, acc_ref, *, nk):
    k = pl.program_id(1)
    part = jnp.dot(x_ref[...], w_ref[...], preferred_element_type=F32)

    @pl.when(k == 0)
    def _():
        acc_ref[...] = part

    @pl.when(k > 0)
    def _():
        acc_ref[...] += part

    @pl.when(k == nk - 1)
    def _():
        y = DN_ALPHA * res_ref[...] + acc_ref[...]
        mu = jnp.mean(y, axis=-1, keepdims=True)
        yc = y - mu
        var = jnp.mean(yc * yc, axis=-1, keepdims=True)
        out = yc * lax.rsqrt(var + LN_EPS) * g_ref[...] + b_ref[...]
        o32_ref[...] = out
        o16_ref[...] = out.astype(BF16)


def _mm_res_ln(x, w, res, g, b, *, tm, tk, name):
    m, kk = x.shape
    d = w.shape[1]
    nk = kk // tk
    return pl.pallas_call(
        functools.partial(_mm_res_ln_kernel, nk=nk),
        grid=(m // tm, nk),
        in_specs=[pl.BlockSpec((tm, tk), lambda i, k: (i, k)),
                  pl.BlockSpec((tk, d), lambda i, k: (k, 0)),
                  pl.BlockSpec((tm, d), lambda i, k: (i, 0)),
                  pl.BlockSpec((1, d), lambda i, k: (0, 0)),
                  pl.BlockSpec((1, d), lambda i, k: (0, 0))],
        out_specs=[pl.BlockSpec((tm, d), lambda i, k: (i, 0)),
                   pl.BlockSpec((tm, d), lambda i, k: (i, 0))],
        out_shape=[jax.ShapeDtypeStruct((m, d), F32), jax.ShapeDtypeStruct((m, d), BF16)],
        scratch_shapes=[pltpu.VMEM((tm, d), F32)],
        compiler_params=_params("parallel", "arbitrary"),
        name=name,
    )(x, w, res, g.reshape(1, d), b.reshape(1, d))


def _ffn_up_kernel(x_ref, wg_ref, wu_ref, o_ref):
    x = x_ref[...]
    g = jnp.dot(x, wg_ref[...], preferred_element_type=F32)
    u = jnp.dot(x, wu_ref[...], preferred_element_type=F32)
    o_ref[...] = (g * jax.nn.sigmoid(g) * u).astype(o_ref.dtype)


def _ffn_up(x, wg, wu, *, tm, tf, name):
    m, d = x.shape
    f = wg.shape[1]
    return pl.pallas_call(
        _ffn_up_kernel,
        grid=(m // tm, f // tf),
        in_specs=[pl.BlockSpec((tm, d), lambda i, j: (i, 0)),
                  pl.BlockSpec((d, tf), lambda i, j: (0, j)),
                  pl.BlockSpec((d, tf), lambda i, j: (0, j))],
        out_specs=pl.BlockSpec((tm, tf), lambda i, j: (i, j)),
        out_shape=jax.ShapeDtypeStruct((m, f), BF16),
        compiler_params=_params("parallel", "arbitrary"),
        name=name,
    )(x, wg, wu)


def _ffn_block(h32, h16, w_gate, w_up, w_down, g, b, tag):
    m = h16.shape[0]
    f = w_gate.shape[1]
    hid = _ffn_up(h16, w_gate.astype(BF16), w_up.astype(BF16),
                  tm=_pick(m, (1664, 1280, 640, 128)), tf=_pick(f, (512, 256, 128)),
                  name=f"{tag}_ffn_up")
    return _mm_res_ln(hid, w_down.astype(BF16), h32, g, b,
                      tm=_pick(m, (640, 128)), tk=_pick(f, (512, 256, 128)), name=f"{tag}_ffn_down_ln")


def _s5_tables(log_dt, a_re, a_im, b_re, b_im, c_re, c_im):
    g_, p_ = a_re.shape
    j_ = b_re.shape[-1]
    q = S5_Q
    dt = jnp.exp(log_dt)[:, None]
    lr = dt * a_re
    li = dt * a_im
    ks = jnp.arange(q + 1, dtype=F32)[:, None, None]
    mag = jnp.exp(ks * lr)
    pw_re = mag * jnp.cos(ks * li)
    pw_im = mag * jnp.sin(ks * li)
    ab_re, ab_im = pw_re[1], pw_im[1]
    den = a_re * a_re + a_im * a_im
    nr = ab_re - 1.0
    f_re = (nr * a_re + ab_im * a_im) / den
    f_im = (ab_im * a_re - nr * a_im) / den
    bb_re = f_re[..., None] * b_re - f_im[..., None] * b_im
    bb_im = f_re[..., None] * b_im + f_im[..., None] * b_re
    cp_re = c_re[None] * pw_re[:, :, None, :] - c_im[None] * pw_im[:, :, None, :]
    cp_im = c_re[None] * pw_im[:, :, None, :] + c_im[None] * pw_re[:, :, None, :]
    kern = (jnp.einsum('lgjp,gpi->lgji', cp_re[:q], bb_re, precision=HI)
            - jnp.einsum('lgjp,gpi->lgji', cp_im[:q], bb_im, precision=HI))
    ii = jnp.arange(q)
    lag = ii[None, :] - ii[:, None]
    toe = jnp.where((lag >= 0)[:, :, None, None, None], kern[jnp.clip(lag, 0, q - 1)], 0.0)
    toe = toe.transpose(2, 0, 4, 1, 3).reshape(g_, q * j_, q * j_)
    rev_re = pw_re[:q][::-1][:, :, :, None]
    rev_im = pw_im[:q][::-1][:, :, :, None]
    st_re = (rev_re * bb_re[None] - rev_im * bb_im[None]).transpose(1, 0, 3, 2).reshape(g_, q * j_, p_)
    st_im = (rev_re * bb_im[None] + rev_im * bb_re[None]).transpose(1, 0, 3, 2).reshape(g_, q * j_, p_)
    wo_re = cp_re[1:].transpose(1, 3, 0, 2).reshape(g_, p_, q * j_)
    wo_im = (-cp_im[1:]).transpose(1, 3, 0, 2).reshape(g_, p_, q * j_)

    def pair_diag(w):
        gg, r, c = w.shape
        w = w.reshape(gg // 2, 2, r, c)
        z = jnp.zeros_like(w[:, 0])
        return jnp.concatenate([jnp.concatenate([w[:, 0], z], axis=2),
                                jnp.concatenate([z, w[:, 1]], axis=2)], axis=1)

    toe_p = pair_diag(toe).astype(BF16)
    st_p = jnp.concatenate([pair_diag(st_re), pair_diag(st_im)], axis=2).astype(BF16)
    wo_re_p = pair_diag(wo_re).astype(BF16)
    wo_im_p = pair_diag(wo_im).astype(BF16)
    aq_re = pw_re[q].reshape(1, g_ * p_)
    aq_im = pw_im[q].reshape(1, g_ * p_)
    return toe_p, st_p, wo_re_p, wo_im_p, aq_re, aq_im


def _s5_local_kernel(u_ref, w_ref, zr_ref, zi_ref):
    z = jnp.dot(u_ref[0], w_ref[0], preferred_element_type=F32)
    half = z.shape[1] // 2
    zr_ref[...] = z[:, :half]
    zi_ref[...] = z[:, half:]


def _s5_scan_kernel(zr_ref, zi_ref, ar_ref, ai_ref, sr_ref, si_ref, *, nb, nc):
    ar = ar_ref[...]
    ai = ai_ref[...]

    def body(c, carry):
        new = []
        for b in range(nb):
            s_re, s_im = carry[2 * b], carry[2 * b + 1]
            row = b * nc + c
            sr_ref[pl.ds(row, 1), :] = s_re
            si_ref[pl.ds(row, 1), :] = s_im
            z_re = zr_ref[pl.ds(row, 1), :]
            z_im = zi_ref[pl.ds(row, 1), :]
            new.append(ar * s_re - ai * s_im + z_re)
            new.append(ar * s_im + ai * s_re + z_im)
        return tuple(new)

    zero = jnp.zeros(ar.shape, F32)
    lax.fori_loop(0, nc, body, (zero,) * (2 * nb))


def _s5_out_kernel(u_ref, t_ref, sr_ref, si_ref, wr_ref, wi_ref, y_ref):
    y = jnp.dot(u_ref[0], t_ref[0], preferred_element_type=F32)
    y += jnp.dot(sr_ref[...].astype(BF16), wr_ref[0], preferred_element_type=F32)
    y += jnp.dot(si_ref[...].astype(BF16), wi_ref[0], preferred_element_type=F32)
    y_ref[0] = y


def _glu_kernel(y_ref, u_ref, d_ref, w_ref, o_ref):
    yy = y_ref[...] + d_ref[...] * u_ref[...]
    g = jax.nn.gelu(yy)
    gate = jax.nn.sigmoid(jnp.dot(g.astype(BF16), w_ref[...], preferred_element_type=F32))
    o_ref[...] = (g * gate).astype(o_ref.dtype)


def _s5_mixer(proj, nb, lb, front, log_dt, a_re, a_im, b_re, b_im, c_re, c_im, d, w_glu):
    r = proj.shape[0]
    g_, p_ = a_re.shape
    j_ = S5_GROUP
    w_ = g_ * j_
    q = S5_Q
    m = r // q
    nc = lb // q
    np_ = g_ // 2
    toe_p, st_p, wo_re_p, wo_im_p, aq_re, aq_im = _s5_tables(log_dt, a_re, a_im, b_re, b_im, c_re, c_im)

    u = proj[:, :w_].reshape(nb, lb, w_)
    u = jnp.where((jnp.arange(lb) >= front)[None, :, None], u, 0.0)
    uc = (u.reshape(m, q, np_, 2, j_).transpose(2, 0, 3, 1, 4)
          .reshape(np_, m, 2 * q * j_).astype(BF16))
    kw = 2 * q * j_
    sw = 2 * p_

    z_re, z_im = pl.pallas_call(
        _s5_local_kernel,
        grid=(np_,),
        in_specs=[pl.BlockSpec((1, m, kw), lambda i: (i, 0, 0)),
                  pl.BlockSpec((1, kw, 2 * sw), lambda i: (i, 0, 0))],
        out_specs=[pl.BlockSpec((m, sw), lambda i: (0, i)),
                   pl.BlockSpec((m, sw), lambda i: (0, i))],
        out_shape=[jax.ShapeDtypeStruct((m, g_ * p_), F32)] * 2,
        compiler_params=_params("parallel"),
        name="s5_local",
    )(uc, st_p)

    cw = _pick(g_ * p_, (512, 256, 128))
    s_re, s_im = pl.pallas_call(
        functools.partial(_s5_scan_kernel, nb=nb, nc=nc),
        grid=(g_ * p_ // cw,),
        in_specs=[pl.BlockSpec((m, cw), lambda i: (0, i)),
                  pl.BlockSpec((m, cw), lambda i: (0, i)),
                  pl.BlockSpec((1, cw), lambda i: (0, i)),
                  pl.BlockSpec((1, cw), lambda i: (0, i))],
        out_specs=[pl.BlockSpec((m, cw), lambda i: (0, i)),
                   pl.BlockSpec((m, cw), lambda i: (0, i))],
        out_shape=[jax.ShapeDtypeStruct((m, g_ * p_), F32)] * 2,
        compiler_params=_params("parallel"),
        name="s5_scan",
    )(z_re, z_im, aq_re, aq_im)

    yc = pl.pallas_call(
        _s5_out_kernel,
        grid=(np_,),
        in_specs=[pl.BlockSpec((1, m, kw), lambda i: (i, 0, 0)),
                  pl.BlockSpec((1, kw, kw), lambda i: (i, 0, 0)),
                  pl.BlockSpec((m, sw), lambda i: (0, i)),
                  pl.BlockSpec((m, sw), lambda i: (0, i)),
                  pl.BlockSpec((1, sw, kw), lambda i: (i, 0, 0)),
                  pl.BlockSpec((1, sw, kw), lambda i: (i, 0, 0))],
        out_specs=pl.BlockSpec((1, m, kw), lambda i: (i, 0, 0)),
        out_shape=jax.ShapeDtypeStruct((np_, m, kw), F32),
        compiler_params=_params("parallel"),
        name="s5_out",
    )(uc, toe_p, s_re, s_im, wo_re_p, wo_im_p)

    y = yc.reshape(np_, m, 2, q, j_).transpose(1, 3, 0, 2, 4).reshape(r, w_)
    tm = _pick(r, (832, 640, 128))
    return pl.pallas_call(
        _glu_kernel,
        grid=(r // tm,),
        in_specs=[pl.BlockSpec((tm, w_), lambda i: (i, 0)),
                  pl.BlockSpec((tm, w_), lambda i: (i, 0)),
                  pl.BlockSpec((1, w_), lambda i: (0, 0)),
                  pl.BlockSpec((w_, w_), lambda i: (0, 0))],
        out_specs=pl.BlockSpec((tm, w_), lambda i: (i, 0)),
        out_shape=jax.ShapeDtypeStruct((r, w_), BF16),
        compiler_params=_params("parallel"),
        name="s5_glu",
    )(y, proj, d.reshape(1, w_), w_glu.astype(BF16))


def _rms(x, g):
    return x * lax.rsqrt(jnp.mean(x * x, axis=-1, keepdims=True) + RMS_EPS) * g


def _mla_prep_kernel(lat_ref, cc_ref, ss_ref, qg_ref, kg_ref, wqn_ref, wqr_ref, wqs_ref, wkn_ref, wv_ref,
                     q_ref, k_ref, v_ref, *, scale):
    lat = lat_ref[...]
    o1 = MLA_Q_RANK
    o2 = o1 + MLA_KV_RANK
    o3 = o2 + LANES
    qn = _rms(lat[:, :o1], qg_ref[...]).astype(BF16)
    kvn = _rms(lat[:, o1:o2], kg_ref[...]).astype(BF16)
    cc = cc_ref[...]
    ss = ss_ref[...]
    k_rope = (lat[:, o2:o3] * cc + lat[:, o3:] * ss).astype(BF16)
    q_nope = jnp.dot(qn, wqn_ref[...], preferred_element_type=F32) * scale
    q_r = jnp.dot(qn, wqr_ref[...], preferred_element_type=F32)
    q_s = jnp.dot(qn, wqs_ref[...], preferred_element_type=F32)
    k_nope = jnp.dot(kvn, wkn_ref[...], preferred_element_type=F32)
    v = jnp.dot(kvn, wv_ref[...], preferred_element_type=F32)
    for h in range(MLA_HEADS):
        sl = slice(h * LANES, (h + 1) * LANES)
        q_ref[h, :, 0:LANES] = q_nope[:, sl].astype(BF16)
        q_ref[h, :, LANES:2 * LANES] = ((q_r[:, sl] * cc + q_s[:, sl] * ss) * scale).astype(BF16)
        k_ref[h, :, 0:LANES] = k_nope[:, sl].astype(BF16)
        k_ref[h, :, LANES:2 * LANES] = k_rope
        v_ref[h] = v[:, sl].astype(BF16)


def _attn_kernel(q_ref, k_ref, v_ref, o_ref, m_ref, l_ref, acc_ref, *, tq, front):
    i = pl.program_id(2)
    q = q_ref[0]
    m_ref[...] = jnp.full(m_ref.shape, MASK_VALUE, F32)
    l_ref[...] = jnp.zeros(l_ref.shape, F32)
    acc_ref[...] = jnp.zeros(acc_ref.shape, F32)
    shift = CHUNK.bit_length() - 1
    q_chunk = (i * tq + lax.broadcasted_iota(jnp.int32, (tq, tq), 0)) >> shift
    k_iota = lax.broadcasted_iota(jnp.int32, (tq, tq), 1)

    def body(j, carry):
        start = pl.multiple_of(j * tq, tq)
        k = k_ref[0, pl.ds(start, tq), :]
        v = v_ref[0, pl.ds(start, tq), :]
        s = lax.dot_general(q, k, (((1,), (1,)), ((), ())), preferred_element_type=F32)
        k_pos = j * tq + k_iota
        visible = (k_pos >= front) & ((k_pos >> shift) <= q_chunk)
        s = jnp.where(visible, s, MASK_VALUE)
        m_prev = m_ref[...]
        m_new = jnp.maximum(m_prev, jnp.max(s, axis=-1, keepdims=True))
        alpha = jnp.exp(m_prev - m_new)
        p = jnp.exp(s - m_new)
        l_ref[...] = alpha * l_ref[...] + jnp.sum(p, axis=-1, keepdims=True)
        acc_ref[...] = alpha * acc_ref[...] + jnp.dot(p.astype(BF16), v, preferred_element_type=F32)
        m_ref[...] = m_new
        return carry

    lax.fori_loop(0, i + 1, body, 0)
    o_ref[...] = (acc_ref[...] / l_ref[...]).astype(o_ref.dtype)


def _rope_tables(lb, front):
    pos = (jnp.arange(lb) - front).astype(F32)
    inv = ROPE_BASE ** (-jnp.arange(0, MLA_ROPE, 2, dtype=F32) / MLA_ROPE)
    ang = pos[:, None] * inv[None, :]
    cos, sin = jnp.cos(ang), jnp.sin(ang)
    zeros = jnp.zeros((lb, LANES - MLA_ROPE), F32)
    cc = jnp.concatenate([cos, cos, zeros], axis=1)
    ss = jnp.concatenate([-sin, sin, zeros], axis=1)
    return cc, ss


def _pad_heads(w, width, take):
    k = w.shape[0]
    w = w.reshape(k, MLA_HEADS, width)[:, :, take]
    pad = LANES - w.shape[-1]
    if pad:
        w = jnp.concatenate([w, jnp.zeros((k, MLA_HEADS, pad), w.dtype)], axis=-1)
    return w.reshape(k, MLA_HEADS * LANES).astype(BF16)


def _mla_mixer(proj, nb, lb, front, q_norm, w_uq, kv_norm, w_ukv):
    r = proj.shape[0]
    half = MLA_ROPE // 2
    qw = MLA_NOPE + MLA_ROPE
    rope_cols = jnp.arange(MLA_NOPE, qw)
    swap_cols = jnp.concatenate([rope_cols[half:], rope_cols[:half]])
    wqn = _pad_heads(w_uq, qw, jnp.arange(MLA_NOPE))
    wqr = _pad_heads(w_uq, qw, rope_cols)
    wqs = _pad_heads(w_uq, qw, swap_cols)
    wkn = _pad_heads(w_ukv, MLA_NOPE + MLA_V, jnp.arange(MLA_NOPE))
    wv = _pad_heads(w_ukv, MLA_NOPE + MLA_V, jnp.arange(MLA_NOPE, MLA_NOPE + MLA_V))
    cc, ss = _rope_tables(lb, front)
    scale = qw ** -0.5

    tm = _pick(lb, (640, 128))
    nt = lb // tm
    hl = MLA_HEADS * LANES
    const = lambda i: (0, 0)
    q, k, v = pl.pallas_call(
        functools.partial(_mla_prep_kernel, scale=scale),
        grid=(r // tm,),
        in_specs=[pl.BlockSpec((tm, 2 * MLA_Q_RANK), lambda i: (i, 1)),
                  pl.BlockSpec((tm, LANES), lambda i: (i % nt, 0)),
                  pl.BlockSpec((tm, LANES), lambda i: (i % nt, 0)),
                  pl.BlockSpec((1, MLA_Q_RANK), const),
                  pl.BlockSpec((1, MLA_KV_RANK), const),
                  pl.BlockSpec((MLA_Q_RANK, hl), const),
                  pl.BlockSpec((MLA_Q_RANK, hl), const),
                  pl.BlockSpec((MLA_Q_RANK, hl), const),
                  pl.BlockSpec((MLA_KV_RANK, hl), const),
                  pl.BlockSpec((MLA_KV_RANK, hl), const)],
        out_specs=[pl.BlockSpec((MLA_HEADS, tm, 2 * LANES), lambda i: (0, i, 0)),
                   pl.BlockSpec((MLA_HEADS, tm, 2 * LANES), lambda i: (0, i, 0)),
                   pl.BlockSpec((MLA_HEADS, tm, LANES), lambda i: (0, i, 0))],
        out_shape=[jax.ShapeDtypeStruct((MLA_HEADS, r, 2 * LANES), BF16),
                   jax.ShapeDtypeStruct((MLA_HEADS, r, 2 * LANES), BF16),
                   jax.ShapeDtypeStruct((MLA_HEADS, r, LANES), BF16)],
        compiler_params=_params("parallel"),
        name="mla_prep",
    )(proj, cc, ss, q_norm.reshape(1, -1), kv_norm.reshape(1, -1), wqn, wqr, wqs, wkn, wv)

    tq = _pick(lb, (640, 128))
    nq = lb // tq
    return pl.pallas_call(
        functools.partial(_attn_kernel, tq=tq, front=front),
        grid=(nb, MLA_HEADS, nq),
        in_specs=[pl.BlockSpec((1, tq, 2 * LANES), lambda b, h, i: (h, b * nq + i, 0)),
                  pl.BlockSpec((1, lb, 2 * LANES), lambda b, h, i: (h, b, 0)),
                  pl.BlockSpec((1, lb, LANES), lambda b, h, i: (h, b, 0))],
        out_specs=pl.BlockSpec((tq, LANES), lambda b, h, i: (b * nq + i, h)),
        out_shape=jax.ShapeDtypeStruct((r, hl), BF16),
        scratch_shapes=[pltpu.VMEM((tq, 1), F32), pltpu.VMEM((tq, 1), F32), pltpu.VMEM((tq, LANES), F32)],
        compiler_params=_params("parallel", "parallel", "arbitrary"),
        name="mla_attention",
    )(q, k, v)


def _dummy_rows(rows, nb, lb, front):
    mask = rows < 0
    for b in range(nb):
        mask = mask | ((rows >= b * lb) & (rows < b * lb + front))
    return mask


def _conv_kernel(x_ref, halo_ref, w_ref, b_ref, o_ref, *, tm, nb, lb, front):
    i = pl.program_id(0)
    tc = x_ref.shape[1]
    rows = i * tm + lax.broadcasted_iota(jnp.int32, (tm, tc), 0)
    x = jnp.where(_dummy_rows(rows, nb, lb, front), 0.0, x_ref[...].astype(F32))
    hrows = i * tm - SUBLANES + lax.broadcasted_iota(jnp.int32, (SUBLANES, tc), 0)
    halo = jnp.where(_dummy_rows(hrows, nb, lb, front), 0.0, halo_ref[...].astype(F32))
    w = w_ref[...]
    k = w.shape[0]
    acc = b_ref[...] + w[k - 1:k, :] * x
    head = b_ref[...] + w[k - 1:k, :] * x[:SUBLANES]
    sub = lax.broadcasted_iota(jnp.int32, (SUBLANES, tc), 0)
    for d in range(1, k):
        wd = w[k - 1 - d:k - d, :]
        acc += wd * pltpu.roll(x, d, 0)
        shifted = jnp.where(sub < d, pltpu.roll(halo, d, 0), pltpu.roll(x[:SUBLANES], d, 0))
        head += wd * shifted
    o_ref[...] = (acc * jax.nn.sigmoid(acc)).astype(o_ref.dtype)
    o_ref[0:SUBLANES, :] = (head * jax.nn.sigmoid(head)).astype(o_ref.dtype)


def _ssd_kernel(xs_ref, bm_ref, cm_ref, bmt_ref, dtc_ref, dtr_ref, bc_ref, br_ref, af_ref, ar_ref,
                d_ref, z_ref, ng_ref, o_ref, state_ref, *, front):
    c = pl.program_id(2)
    q = xs_ref.shape[0]
    hp = xs_ref.shape[1]
    nh = dtr_ref.shape[1]
    hd = hp // nh
    wide = dtc_ref.shape[1]

    @pl.when(c == 0)
    def _():
        state_ref[...] = jnp.zeros(state_ref.shape, F32)

    first_live = jnp.where(c == 0, front, 0)
    live_c = lax.broadcasted_iota(jnp.int32, (q, wide), 0) >= first_live
    live_r = lax.broadcasted_iota(jnp.int32, (nh, q), 1) >= first_live
    dt_c = jnp.where(live_c, jax.nn.softplus(dtc_ref[...] + bc_ref[0]), 0.0)
    dt_r = jnp.where(live_r, jax.nn.softplus(dtr_ref[0] + br_ref[0]), 0.0)
    li = lax.broadcasted_iota(jnp.int32, (q, q), 0)
    si = lax.broadcasted_iota(jnp.int32, (q, q), 1)
    lower = li >= si
    e_row = lax.broadcasted_iota(jnp.int32, (wide, hp), 0)
    e_col = lax.broadcasted_iota(jnp.int32, (wide, hp), 1)
    expand = ((e_col >= e_row * hd) & (e_col < (e_row + 1) * hd)).astype(F32)
    dt_full = jnp.dot(dt_c, expand, precision=HI, preferred_element_type=F32)
    cs_full = jnp.dot(lower.astype(F32), dt_full * af_ref[0], precision=HI, preferred_element_type=F32)
    cs_r = jnp.dot(dt_r * ar_ref[0], (li <= si).astype(F32), precision=HI, preferred_element_type=F32)
    last = cs_full[q - 1:q, :]

    xs = xs_ref[...].astype(F32)
    xdt = xs * dt_full
    bm = bm_ref[...]
    cm = cm_ref[...]
    cb = lax.dot_general(cm, bm, (((1,), (1,)), ((), ())), preferred_element_type=F32)

    prev = state_ref[...]
    y = jnp.dot(cm, prev.astype(BF16), preferred_element_type=F32) * jnp.exp(cs_full)
    local = jnp.dot(bmt_ref[...], (xdt * jnp.exp(last - cs_full)).astype(BF16), preferred_element_type=F32)
    state_ref[...] = prev * jnp.exp(last) + local

    lane = lax.broadcasted_iota(jnp.int32, (q, 2 * hd), 1)
    diag = []
    for pr in range(nh // 2):
        ms = []
        for h in (2 * pr, 2 * pr + 1):
            seg = cs_full[:, h * hd:h * hd + 1] - cs_r[h:h + 1, :]
            ms.append((cb * jnp.exp(jnp.where(lower, seg, -jnp.inf))).astype(BF16))
        xp = xdt[:, pr * 2 * hd:(pr + 1) * 2 * hd]
        rhs = jnp.concatenate([jnp.where(lane < hd, xp, 0.0), jnp.where(lane >= hd, xp, 0.0)], axis=0)
        diag.append(jnp.dot(jnp.concatenate(ms, axis=1), rhs.astype(BF16), preferred_element_type=F32))
    y = y + jnp.concatenate(diag, axis=1) + d_ref[0] * xs

    z = z_ref[...].astype(F32)
    y = y * (z * jax.nn.sigmoid(z))
    y = y * lax.rsqrt(jnp.mean(y * y, axis=-1, keepdims=True) + RMS_EPS) * ng_ref[...]
    o_ref[...] = y.astype(o_ref.dtype)


def _mamba2_mixer(h16, nb, lb, front, w_in, conv_w, conv_b, dt_bias, a_log, d, norm_g):
    r = h16.shape[0]
    inner = norm_g.shape[0]
    gn = SSD_GROUPS * SSD_STATE
    cdim = inner + 2 * gn
    nheads = SSD_GROUPS * SSD_HPG
    tm = _pick(r, (1664, 1280, 640, 128))
    w16 = w_in.astype(BF16)
    z = _matmul(h16, w16[:, :inner], BF16, tm=tm, tn=_pick(inner, (1024, 512, 128)), name="l1_in_z")
    xbc = _matmul(h16, w16[:, inner:inner + cdim], BF16, tm=tm, tn=_pick(cdim, (1024, 512, 128)),
                  name="l1_in_xbc")
    w_dt = w16[:, inner + cdim:].reshape(-1, SSD_GROUPS, SSD_HPG)
    w_dt = jnp.concatenate([w_dt, jnp.zeros(w_dt.shape[:2] + (LANES - SSD_HPG,), BF16)], axis=2)
    dt_raw = _matmul(h16, w_dt.reshape(-1, SSD_GROUPS * LANES), F32, tm=tm, tn=SSD_GROUPS * LANES,
                     name="l1_in_dt")

    tcv = _pick(cdim, (512, 256, 128))
    blocks = tm // SUBLANES
    act = pl.pallas_call(
        functools.partial(_conv_kernel, tm=tm, nb=nb, lb=lb, front=front),
        grid=(r // tm, cdim // tcv),
        in_specs=[pl.BlockSpec((tm, tcv), lambda i, j: (i, j)),
                  pl.BlockSpec((SUBLANES, tcv), lambda i, j: (jnp.maximum(i * blocks - 1, 0), j)),
                  pl.BlockSpec((SSD_CONV, tcv), lambda i, j: (0, j)),
                  pl.BlockSpec((1, tcv), lambda i, j: (0, j))],
        out_specs=pl.BlockSpec((tm, tcv), lambda i, j: (i, j)),
        out_shape=jax.ShapeDtypeStruct((r, cdim), BF16),
        compiler_params=_params("parallel", "parallel"),
        name="ssd_conv",
    )(xbc, xbc, conv_w, conv_b.reshape(1, cdim))

    gsz = inner // SSD_GROUPS
    nc = lb // SSD_BLOCK
    bmt = act[:, inner:inner + gn].T
    dt_r = dt_raw.reshape(r, SSD_GROUPS, LANES)[:, :, :SSD_HPG].transpose(1, 2, 0)
    bias = dt_bias.reshape(SSD_GROUPS, SSD_HPG)
    bias_c = jnp.concatenate([bias, jnp.zeros((SSD_GROUPS, LANES - SSD_HPG), F32)], axis=1)
    a = -jnp.exp(a_log).reshape(SSD_GROUPS, SSD_HPG)
    a_full = jnp.repeat(a, SSD_HEAD_DIM, axis=1).reshape(SSD_GROUPS, 1, gsz)
    d_full = jnp.repeat(d, SSD_HEAD_DIM).reshape(SSD_GROUPS, 1, gsz)
    xg = inner // gsz
    row = lambda b, g, c: b * nc + c
    return pl.pallas_call(
        functools.partial(_ssd_kernel, front=front),
        grid=(nb, SSD_GROUPS, nc),
        in_specs=[pl.BlockSpec((SSD_BLOCK, gsz), lambda b, g, c: (row(b, g, c), g)),
                  pl.BlockSpec((SSD_BLOCK, SSD_STATE), lambda b, g, c: (row(b, g, c), xg * gsz // SSD_STATE + g)),
                  pl.BlockSpec((SSD_BLOCK, SSD_STATE),
                               lambda b, g, c: (row(b, g, c), (inner + gn) // SSD_STATE + g)),
                  pl.BlockSpec((SSD_STATE, SSD_BLOCK), lambda b, g, c: (g, row(b, g, c))),
                  pl.BlockSpec((SSD_BLOCK, LANES), lambda b, g, c: (row(b, g, c), g)),
                  pl.BlockSpec((1, SSD_HPG, SSD_BLOCK), lambda b, g, c: (g, 0, row(b, g, c))),
                  pl.BlockSpec((1, 1, LANES), lambda b, g, c: (g, 0, 0)),
                  pl.BlockSpec((1, SSD_HPG, 1), lambda b, g, c: (g, 0, 0)),
                  pl.BlockSpec((1, 1, gsz), lambda b, g, c: (g, 0, 0)),
                  pl.BlockSpec((1, SSD_HPG, 1), lambda b, g, c: (g, 0, 0)),
                  pl.BlockSpec((1, 1, gsz), lambda b, g, c: (g, 0, 0)),
                  pl.BlockSpec((SSD_BLOCK, gsz), lambda b, g, c: (row(b, g, c), g)),
                  pl.BlockSpec((1, gsz), lambda b, g, c: (0, g))],
        out_specs=pl.BlockSpec((SSD_BLOCK, gsz), lambda b, g, c: (row(b, g, c), g)),
        out_shape=jax.ShapeDtypeStruct((r, inner), BF16),
        scratch_shapes=[pltpu.VMEM((SSD_STATE, gsz), F32)],
        compiler_params=_params("parallel", "parallel", "arbitrary"),
        name="ssd_scan",
    )(act, act, act, bmt, dt_raw, dt_r,
      bias_c[:, None, :], bias[:, :, None], a_full, a[:, :, None],
      d_full, z, norm_g.reshape(1, inner))


def kernel(x, meta_tokens, l0_w_in, l0_s5_log_dt, l0_s5_a_re, l0_s5_a_im, l0_s5_b_re, l0_s5_b_im, l0_s5_c_re, l0_s5_c_im, l0_s5_d, l0_s5_w_glu, l0_mla_q_norm, l0_mla_w_uq, l0_mla_kv_norm, l0_mla_w_ukv, l0_w_out, l0_ln1_g, l0_ln1_b, l0_ffn_w_gate, l0_ffn_w_up, l0_ffn_w_down, l0_ln2_g, l0_ln2_b, l1_w_in, l1_conv_w, l1_conv_b, l1_dt_bias, l1_a_log, l1_d, l1_norm_g, l1_w_out, l1_ln1_g, l1_ln1_b, l1_ffn_w_gate, l1_ffn_w_up, l1_ffn_w_down, l1_ln2_g, l1_ln2_b):
    nb, seq, dm = x.shape
    n_meta = meta_tokens.shape[0]
    front = (-n_meta) % SSD_BLOCK
    lb = front + n_meta + seq
    r = nb * lb
    meta = jnp.broadcast_to(meta_tokens[None].astype(x.dtype), (nb, n_meta, dm))
    h32 = jnp.concatenate([jnp.zeros((nb, front, dm), x.dtype), meta, x], axis=1).reshape(r, dm)
    h16 = h32.astype(BF16)

    s5w = l0_s5_d.shape[0]
    o3 = s5w + MLA_Q_RANK + MLA_KV_RANK
    kr = l0_w_in[:, o3:]
    half = MLA_ROPE // 2
    zpad = jnp.zeros((dm, LANES - MLA_ROPE), l0_w_in.dtype)
    w0 = jnp.concatenate([l0_w_in[:, :o3], kr, zpad, kr[:, half:], kr[:, :half], zpad], axis=1).astype(BF16)
    tm = _pick(r, (1664, 1280, 640, 128))
    proj = _matmul(h16, w0, F32, tm=tm, tn=_pick(w0.shape[1], (1024, 512, 128)), name="l0_in")
    a_out = _s5_mixer(proj, nb, lb, front, l0_s5_log_dt, l0_s5_a_re, l0_s5_a_im, l0_s5_b_re, l0_s5_b_im,
                      l0_s5_c_re, l0_s5_c_im, l0_s5_d, l0_s5_w_glu)
    b_out = _mla_mixer(proj, nb, lb, front, l0_mla_q_norm, l0_mla_w_uq, l0_mla_kv_norm, l0_mla_w_ukv)
    mix = jnp.concatenate([a_out, b_out], axis=1)
    tml = _pick(r, (640, 128))
    h32, h16 = _mm_res_ln(mix, l0_w_out.astype(BF16), h32, l0_ln1_g, l0_ln1_b, tm=tml, tk=512, name="l0_out_ln")
    h32, h16 = _ffn_block(h32, h16, l0_ffn_w_gate, l0_ffn_w_up, l0_ffn_w_down, l0_ln2_g, l0_ln2_b, "l0")

    y = _mamba2_mixer(h16, nb, lb, front, l1_w_in, l1_conv_w, l1_conv_b, l1_dt_bias, l1_a_log, l1_d, l1_norm_g)
    h32, h16 = _mm_res_ln(y, l1_w_out.astype(BF16), h32, l1_ln1_g, l1_ln1_b, tm=tml, tk=512, name="l1_out_ln")
    h32, h16 = _ffn_block(h32, h16, l1_ffn_w_gate, l1_ffn_w_up, l1_ffn_w_down, l1_ln2_g, l1_ln2_b, "l1")
    return h32.reshape(nb, lb, dm)[:, front + n_meta:, :]
```

```python
import functools
import math

import jax
import jax.numpy as jnp
from jax import lax
from jax.experimental import pallas as pl
from jax.experimental.pallas import tpu as pltpu

F32 = jnp.float32
BF16 = jnp.bfloat16

DEPTH = 2
CHUNK = 64
N_META = 16
DN_ALPHA = (2 * DEPTH) ** 0.25
LN_EPS = 1e-5
RMS_EPS = 1e-6
S5_GROUP = 16
S5_STATE = 64
S5_Q = 16
S5_LANE_GROUPS = 128 // S5_GROUP
MLA_HEADS = 8
MLA_NOPE = 128
MLA_ROPE = 64
MLA_V = 128
MLA_Q_RANK = 512
MLA_KV_RANK = 256
ROPE_BASE = 10000.0
SSD_HEAD_DIM = 64
SSD_GROUPS = 8
SSD_HPG = 8
SSD_STATE = 128
SSD_CONV = 4
SSD_BLOCK = 128

LANES = 128
SUBLANES = 8
VMEM_LIMIT = 56 * 1024 * 1024
MASK_VALUE = -1e30
HI = lax.Precision.HIGHEST


def _pick(n, candidates):
    for c in candidates:
        if n % c == 0:
            return c
    raise ValueError(f"no tile for {n} in {candidates}")


def _params(*sem):
    return pltpu.CompilerParams(dimension_semantics=sem, vmem_limit_bytes=VMEM_LIMIT)


def _mm_kernel(x_ref, w_ref, o_ref):
    o_ref[...] = jnp.dot(x_ref[...], w_ref[...], preferred_element_type=F32).astype(o_ref.dtype)


def _matmul(x, w, out_dtype, *, tm, tn, name):
    m, k = x.shape
    n = w.shape[1]
    return pl.pallas_call(
        _mm_kernel,
        grid=(m // tm, n // tn),
        in_specs=[pl.BlockSpec((tm, k), lambda i, j: (i, 0)),
                  pl.BlockSpec((k, tn), lambda i, j: (0, j))],
        out_specs=pl.BlockSpec((tm, tn), lambda i, j: (i, j)),
        out_shape=jax.ShapeDtypeStruct((m, n), out_dtype),
        compiler_params=_params("parallel", "arbitrary"),
        name=name,
    )(x, w)


def _mm_res_ln_kernel(x_ref, w_ref, res_ref, g_ref, b_ref, o32_ref, o16_ref, acc_ref, *, nk):
    k = pl.program_id(1)

    @pl.when(k == 0)
    def _():
        acc_ref[...] = jnp.zeros(acc_ref.shape, F32)

    acc_ref[...] += jnp.dot(x_ref[...], w_ref[...], preferred_element_type=F32)

    @pl.when(k == nk - 1)
    def _():
        y = DN_ALPHA * res_ref[...] + acc_ref[...]
        mu = jnp.mean(y, axis=-1, keepdims=True)
        yc = y - mu
        var = jnp.mean(yc * yc, axis=-1, keepdims=True)
        out = yc * lax.rsqrt(var + LN_EPS) * g_ref[...] + b_ref[...]
        o32_ref[...] = out
        o16_ref[...] = out.astype(BF16)


def _mm_res_ln(x, w, res, g, b, *, tm, tk, name):
    m, kk = x.shape
    d = w.shape[1]
    nk = kk // tk
    return pl.pallas_call(
        functools.partial(_mm_res_ln_kernel, nk=nk),
        grid=(m // tm, nk),
        in_specs=[pl.BlockSpec((tm, tk), lambda i, k: (i, k)),
                  pl.BlockSpec((tk, d), lambda i, k: (k, 0)),
                  pl.BlockSpec((tm, d), lambda i, k: (i, 0)),
                  pl.BlockSpec((1, d), lambda i, k: (0, 0)),
                  pl.BlockSpec((1, d), lambda i, k: (0, 0))],
        out_specs=[pl.BlockSpec((tm, d), lambda i, k: (i, 0)),
                   pl.BlockSpec((tm, d), lambda i, k: (i, 0))],
        out_shape=[jax.ShapeDtypeStruct((m, d), F32), jax.ShapeDtypeStruct((m, d), BF16)],
        scratch_shapes=[pltpu.VMEM((tm, d), F32)],
        compiler_params=_params("parallel", "arbitrary"),
        name=name,
    )(x, w, res, g.reshape(1, d), b.reshape(1, d))


def _ffn_up_kernel(x_ref, wg_ref, wu_ref, o_ref):
    x = x_ref[...]
    g = jnp.dot(x, wg_ref[...], preferred_element_type=F32)
    u = jnp.dot(x, wu_ref[...], preferred_element_type=F32)
    o_ref[...] = (g * jax.nn.sigmoid(g) * u).astype(o_ref.dtype)


def _ffn_up(x, wg, wu, *, tm, tf, name):
    m, d = x.shape
    f = wg.shape[1]
    return pl.pallas_call(
        _ffn_up_kernel,
        grid=(m // tm, f // tf),
        in_specs=[pl.BlockSpec((tm, d), lambda i, j: (i, 0)),
                  pl.BlockSpec((d, tf), lambda i, j: (0, j)),
                  pl.BlockSpec((d, tf), lambda i, j: (0, j))],
        out_specs=pl.BlockSpec((tm, tf), lambda i, j: (i, j)),
        out_shape=jax.ShapeDtypeStruct((m, f), BF16),
        compiler_params=_params("parallel", "arbitrary"),
        name=name,
    )(x, wg, wu)


def _ffn_block(h32, h16, w_gate, w_up, w_down, g, b, tag):
    m = h16.shape[0]
    f = w_gate.shape[1]
    hid = _ffn_up(h16, w_gate.astype(BF16), w_up.astype(BF16),
                  tm=_pick(m, (1664, 1280, 640, 128)), tf=_pick(f, (512, 256, 128)),
                  name=f"{tag}_ffn_up")
    return _mm_res_ln(hid, w_down.astype(BF16), h32, g, b,
                      tm=_pick(m, (640, 128)), tk=_pick(f, (1408, 512, 256, 128)), name=f"{tag}_ffn_down_ln")


def _s5_tables(log_dt, a_re, a_im, b_re, b_im, c_re, c_im):
    g_, p_ = a_re.shape
    j_ = b_re.shape[-1]
    q = S5_Q
    dt = jnp.exp(log_dt)[:, None]
    lr = dt * a_re
    li = dt * a_im
    ks = jnp.arange(q + 1, dtype=F32)[:, None, None]
    mag = jnp.exp(ks * lr)
    pw_re = mag * jnp.cos(ks * li)
    pw_im = mag * jnp.sin(ks * li)
    ab_re, ab_im = pw_re[1], pw_im[1]
    den = a_re * a_re + a_im * a_im
    nr = ab_re - 1.0
    f_re = (nr * a_re + ab_im * a_im) / den
    f_im = (ab_im * a_re - nr * a_im) / den
    bb_re = f_re[..., None] * b_re - f_im[..., None] * b_im
    bb_im = f_re[..., None] * b_im + f_im[..., None] * b_re
    cp_re = c_re[None] * pw_re[:, :, None, :] - c_im[None] * pw_im[:, :, None, :]
    cp_im = c_re[None] * pw_im[:, :, None, :] + c_im[None] * pw_re[:, :, None, :]
    kern = (jnp.einsum('lgjp,gpi->lgji', cp_re[:q], bb_re, precision=HI)
            - jnp.einsum('lgjp,gpi->lgji', cp_im[:q], bb_im, precision=HI))
    ii = jnp.arange(q)
    lag = ii[None, :] - ii[:, None]
    toe = jnp.where((lag >= 0)[:, :, None, None, None], kern[jnp.clip(lag, 0, q - 1)], 0.0)
    toe = toe.transpose(2, 0, 4, 1, 3)
    kr = (q - 1 - jnp.arange(q)).astype(F32)[:, None, None]
    rev_mag = jnp.exp(kr * lr)
    rev_re = (rev_mag * jnp.cos(kr * li))[:, :, :, None]
    rev_im = (rev_mag * jnp.sin(kr * li))[:, :, :, None]
    st_re = (rev_re * bb_re[None] - rev_im * bb_im[None]).transpose(1, 0, 3, 2)
    st_im = (rev_re * bb_im[None] + rev_im * bb_re[None]).transpose(1, 0, 3, 2)
    wo_re = cp_re[1:].transpose(1, 3, 0, 2)
    wo_im = (-cp_im[1:]).transpose(1, 3, 0, 2)

    gl = S5_LANE_GROUPS
    nblk = g_ // gl
    eye = jnp.eye(gl, dtype=F32)
    toe_b = (toe.reshape(nblk, gl, q, j_, q, j_).transpose(0, 2, 1, 3, 4, 5)[:, :, :, :, :, None, :]
             * eye[None, None, :, None, None, :, None])
    toe_b = toe_b.reshape(nblk, q * gl * j_, q * gl * j_).astype(BF16)

    def st_blocks(st):
        w = (st.reshape(nblk, gl, q, j_, p_).transpose(0, 2, 1, 3, 4)[:, :, :, :, None, :]
             * eye[None, None, :, None, :, None])
        return w.reshape(nblk, q * gl * j_, gl * p_)

    st_b = jnp.concatenate([st_blocks(st_re), st_blocks(st_im)], axis=2).astype(BF16)

    def wo_blocks(wo):
        w = wo.reshape(nblk, gl, p_, q, j_)[:, :, :, :, None, :] * eye[None, :, None, None, :, None]
        return w.reshape(nblk, gl * p_, q * gl * j_).astype(BF16)

    aq_re = pw_re[q].reshape(1, g_ * p_)
    aq_im = pw_im[q].reshape(1, g_ * p_)
    return toe_b, st_b, wo_blocks(wo_re), wo_blocks(wo_im), aq_re, aq_im


def _s5_rows(u_ref, front_rows):
    rows = u_ref.shape[0] // S5_Q
    live = lax.broadcasted_iota(jnp.int32, (rows, u_ref.shape[1]), 0) >= front_rows
    pieces = [jnp.where(live, u_ref[pl.ds(i, rows, stride=S5_Q), :], 0.0) for i in range(S5_Q)]
    return pieces, jnp.concatenate([p.astype(BF16) for p in pieces], axis=1)


def _s5_local_kernel(u_ref, w_ref, zr_ref, zi_ref, *, front_rows):
    _, x = _s5_rows(u_ref, front_rows)
    z = jnp.dot(x, w_ref[0], preferred_element_type=F32)
    half = z.shape[1] // 2
    zr_ref[...] = z[:, :half]
    zi_ref[...] = z[:, half:]


def _s5_scan_kernel(zr_ref, zi_ref, ar_ref, ai_ref, sr_ref, si_ref, *, nb, nc):
    ar = ar_ref[...]
    ai = ai_ref[...]

    def body(c, carry):
        new = []
        for b in range(nb):
            s_re, s_im = carry[2 * b], carry[2 * b + 1]
            row = b * nc + c
            sr_ref[pl.ds(row, 1), :] = s_re
            si_ref[pl.ds(row, 1), :] = s_im
            z_re = zr_ref[pl.ds(row, 1), :]
            z_im = zi_ref[pl.ds(row, 1), :]
            new.append(ar * s_re - ai * s_im + z_re)
            new.append(ar * s_im + ai * s_re + z_im)
        return tuple(new)

    zero = jnp.zeros(ar.shape, F32)
    lax.fori_loop(0, nc, body, (zero,) * (2 * nb))


def _s5_out_kernel(u_ref, t_ref, sr_ref, si_ref, wr_ref, wi_ref, d_ref, y_ref, *, front_rows):
    pieces, x = _s5_rows(u_ref, front_rows)
    rows = x.shape[0]
    y = jnp.dot(x, t_ref[0], preferred_element_type=F32)
    y += jnp.dot(sr_ref[...].astype(BF16), wr_ref[0], preferred_element_type=F32)
    y += jnp.dot(si_ref[...].astype(BF16), wi_ref[0], preferred_element_type=F32)
    d = d_ref[...]
    for j in range(S5_Q):
        y_ref[pl.ds(j, rows, stride=S5_Q), :] = y[:, j * LANES:(j + 1) * LANES] + d * pieces[j]


def _glu_kernel(y_ref, w_ref, o_ref):
    g = jax.nn.gelu(y_ref[...])
    gate = jax.nn.sigmoid(jnp.dot(g.astype(BF16), w_ref[...], preferred_element_type=F32))
    o_ref[...] = (g * gate).astype(o_ref.dtype)


def _s5_mixer(proj, nb, lb, front, log_dt, a_re, a_im, b_re, b_im, c_re, c_im, d, w_glu):
    r = proj.shape[0]
    g_, p_ = a_re.shape
    w_ = g_ * S5_GROUP
    q = S5_Q
    m = r // q
    nc = lb // q
    nblk = w_ // LANES
    sw = S5_LANE_GROUPS * p_
    kw = q * LANES
    front_rows = front // q
    toe_b, st_b, wo_re_b, wo_im_b, aq_re, aq_im = _s5_tables(log_dt, a_re, a_im, b_re, b_im, c_re, c_im)

    z_re, z_im = pl.pallas_call(
        functools.partial(_s5_local_kernel, front_rows=front_rows),
        grid=(nblk, nb),
        in_specs=[pl.BlockSpec((lb, LANES), lambda c, b: (b, c)),
                  pl.BlockSpec((1, kw, 2 * sw), lambda c, b: (c, 0, 0))],
        out_specs=[pl.BlockSpec((nc, sw), lambda c, b: (b, c)),
                   pl.BlockSpec((nc, sw), lambda c, b: (b, c))],
        out_shape=[jax.ShapeDtypeStruct((m, g_ * p_), F32)] * 2,
        compiler_params=_params("parallel", "parallel"),
        name="s5_local",
    )(proj, st_b)

    cw = _pick(g_ * p_, (512, 256, 128))
    s_re, s_im = pl.pallas_call(
        functools.partial(_s5_scan_kernel, nb=nb, nc=nc),
        grid=(g_ * p_ // cw,),
        in_specs=[pl.BlockSpec((m, cw), lambda i: (0, i)),
                  pl.BlockSpec((m, cw), lambda i: (0, i)),
                  pl.BlockSpec((1, cw), lambda i: (0, i)),
                  pl.BlockSpec((1, cw), lambda i: (0, i))],
        out_specs=[pl.BlockSpec((m, cw), lambda i: (0, i)),
                   pl.BlockSpec((m, cw), lambda i: (0, i))],
        out_shape=[jax.ShapeDtypeStruct((m, g_ * p_), F32)] * 2,
        compiler_params=_params("parallel"),
        name="s5_scan",
    )(z_re, z_im, aq_re, aq_im)

    y = pl.pallas_call(
        functools.partial(_s5_out_kernel, front_rows=front_rows),
        grid=(nblk, nb),
        in_specs=[pl.BlockSpec((lb, LANES), lambda c, b: (b, c)),
                  pl.BlockSpec((1, kw, kw), lambda c, b: (c, 0, 0)),
                  pl.BlockSpec((nc, sw), lambda c, b: (b, c)),
                  pl.BlockSpec((nc, sw), lambda c, b: (b, c)),
                  pl.BlockSpec((1, sw, kw), lambda c, b: (c, 0, 0)),
                  pl.BlockSpec((1, sw, kw), lambda c, b: (c, 0, 0)),
                  pl.BlockSpec((1, LANES), lambda c, b: (0, c))],
        out_specs=pl.BlockSpec((lb, LANES), lambda c, b: (b, c)),
        out_shape=jax.ShapeDtypeStruct((r, w_), F32),
        compiler_params=_params("parallel", "parallel"),
        name="s5_out",
    )(proj, toe_b, s_re, s_im, wo_re_b, wo_im_b, d.reshape(1, w_))

    tm = _pick(r, (832, 640, 128))
    return pl.pallas_call(
        _glu_kernel,
        grid=(r // tm,),
        in_specs=[pl.BlockSpec((tm, w_), lambda i: (i, 0)),
                  pl.BlockSpec((w_, w_), lambda i: (0, 0))],
        out_specs=pl.BlockSpec((tm, w_), lambda i: (i, 0)),
        out_shape=jax.ShapeDtypeStruct((r, w_), BF16),
        compiler_params=_params("parallel"),
        name="s5_glu",
    )(y, w_glu.astype(BF16))


def _rms(x, g):
    return x * lax.rsqrt(jnp.mean(x * x, axis=-1, keepdims=True) + RMS_EPS) * g


def _mla_prep_kernel(lat_ref, cc_ref, ss_ref, qg_ref, kg_ref, wqn_ref, wqr_ref, wqs_ref, wkn_ref, wvt_ref,
                     q_ref, k_ref, vt_ref, *, scale, tiles_per_batch, front):
    tm = lat_ref.shape[0]
    lat = lat_ref[...]
    o1 = MLA_Q_RANK
    o2 = o1 + MLA_KV_RANK
    o3 = o2 + LANES
    qn = _rms(lat[:, :o1], qg_ref[...]).astype(BF16)
    kvn = _rms(lat[:, o1:o2], kg_ref[...]).astype(BF16)
    cc = cc_ref[...]
    ss = ss_ref[...]
    lane = lax.broadcasted_iota(jnp.int32, (tm, LANES), 1)
    row = (pl.program_id(0) % tiles_per_batch) * tm + lax.broadcasted_iota(jnp.int32, (tm, LANES), 0)
    bias_lane = lane == MLA_ROPE
    k_rope = jnp.where(bias_lane & (row < front), MASK_VALUE, lat[:, o2:o3] * cc + lat[:, o3:] * ss).astype(BF16)
    q_nope = jnp.dot(qn, wqn_ref[...], preferred_element_type=F32) * scale
    q_r = jnp.dot(qn, wqr_ref[...], preferred_element_type=F32)
    q_s = jnp.dot(qn, wqs_ref[...], preferred_element_type=F32)
    k_nope = jnp.dot(kvn, wkn_ref[...], preferred_element_type=F32)
    v_t = lax.dot_general(wvt_ref[...], kvn, (((1,), (1,)), ((), ())), preferred_element_type=F32)
    for h in range(MLA_HEADS):
        sl = slice(h * LANES, (h + 1) * LANES)
        q_ref[h, :, 0:LANES] = q_nope[:, sl].astype(BF16)
        q_rope = jnp.where(bias_lane, 1.0, (q_r[:, sl] * cc + q_s[:, sl] * ss) * scale)
        q_ref[h, :, LANES:2 * LANES] = q_rope.astype(BF16)
        k_ref[h, :, 0:LANES] = k_nope[:, sl].astype(BF16)
        k_ref[h, :, LANES:2 * LANES] = k_rope
        vt_ref[h, 0] = v_t[h * MLA_V:(h + 1) * MLA_V, :].astype(BF16)


def _attn_kernel(q_ref, k_ref, vt_ref, o_ref, m_ref, l_ref, acc_ref, *, tq):
    i = pl.program_id(2)
    q = q_ref[0]
    m_ref[...] = jnp.full(m_ref.shape, MASK_VALUE, F32)
    l_ref[...] = jnp.zeros(l_ref.shape, F32)
    acc_ref[...] = jnp.zeros(acc_ref.shape, F32)
    shift = CHUNK.bit_length() - 1

    def tile(j, diagonal):
        k = k_ref[0, pl.ds(pl.multiple_of(j * tq, tq), tq), :]
        s = lax.dot_general(k, q, (((1,), (1,)), ((), ())), preferred_element_type=F32)
        if diagonal:
            k_chunk = lax.broadcasted_iota(jnp.int32, (tq, tq), 0) >> shift
            q_chunk = lax.broadcasted_iota(jnp.int32, (tq, tq), 1) >> shift
            s = jnp.where(k_chunk <= q_chunk, s, MASK_VALUE)
        m_prev = m_ref[...]
        m_new = jnp.maximum(m_prev, jnp.max(s, axis=0, keepdims=True))
        alpha = jnp.exp2(m_prev - m_new)
        p = jnp.exp2(s - m_new)
        l_ref[...] = alpha * l_ref[...] + jnp.sum(p, axis=0, keepdims=True)
        acc_ref[...] = alpha * acc_ref[...] + jnp.dot(vt_ref[0, j], p.astype(BF16), preferred_element_type=F32)
        m_ref[...] = m_new

    def body(j, carry):
        tile(j, False)
        return carry

    lax.fori_loop(0, i, body, 0)
    tile(i, True)
    o_ref[...] = (acc_ref[...] / l_ref[...]).T.astype(o_ref.dtype)


def _rope_tables(lb, front):
    pos = (jnp.arange(lb) - front).astype(F32)
    inv = ROPE_BASE ** (-jnp.arange(0, MLA_ROPE, 2, dtype=F32) / MLA_ROPE)
    ang = pos[:, None] * inv[None, :]
    cos, sin = jnp.cos(ang), jnp.sin(ang)
    zeros = jnp.zeros((lb, LANES - MLA_ROPE), F32)
    cc = jnp.concatenate([cos, cos, zeros], axis=1)
    ss = jnp.concatenate([-sin, sin, zeros], axis=1)
    return cc, ss


def _pad_heads(w, width, take):
    k = w.shape[0]
    w = w.reshape(k, MLA_HEADS, width)[:, :, take]
    pad = LANES - w.shape[-1]
    if pad:
        w = jnp.concatenate([w, jnp.zeros((k, MLA_HEADS, pad), w.dtype)], axis=-1)
    return w.reshape(k, MLA_HEADS * LANES).astype(BF16)


def _mla_mixer(proj, nb, lb, front, q_norm, w_uq, kv_norm, w_ukv):
    r = proj.shape[0]
    half = MLA_ROPE // 2
    qw = MLA_NOPE + MLA_ROPE
    rope_cols = jnp.arange(MLA_NOPE, qw)
    swap_cols = jnp.concatenate([rope_cols[half:], rope_cols[:half]])
    wqn = _pad_heads(w_uq, qw, jnp.arange(MLA_NOPE))
    wqr = _pad_heads(w_uq, qw, rope_cols)
    wqs = _pad_heads(w_uq, qw, swap_cols)
    wkn = _pad_heads(w_ukv, MLA_NOPE + MLA_V, jnp.arange(MLA_NOPE))
    wvt = _pad_heads(w_ukv, MLA_NOPE + MLA_V, jnp.arange(MLA_NOPE, MLA_NOPE + MLA_V)).T
    cc, ss = _rope_tables(lb, front)
    scale = qw ** -0.5 * math.log2(math.e)

    tq = _pick(lb, (640, 128))
    nq = lb // tq
    hl = MLA_HEADS * LANES
    lat_w = MLA_Q_RANK + MLA_KV_RANK + 2 * LANES
    const = lambda i: (0, 0)
    q, k, vt = pl.pallas_call(
        functools.partial(_mla_prep_kernel, scale=scale, tiles_per_batch=nq, front=front),
        grid=(r // tq,),
        in_specs=[pl.BlockSpec((tq, lat_w), lambda i: (i, 1)),
                  pl.BlockSpec((tq, LANES), lambda i: (i % nq, 0)),
                  pl.BlockSpec((tq, LANES), lambda i: (i % nq, 0)),
                  pl.BlockSpec((1, MLA_Q_RANK), const),
                  pl.BlockSpec((1, MLA_KV_RANK), const),
                  pl.BlockSpec((MLA_Q_RANK, hl), const),
                  pl.BlockSpec((MLA_Q_RANK, hl), const),
                  pl.BlockSpec((MLA_Q_RANK, hl), const),
                  pl.BlockSpec((MLA_KV_RANK, hl), const),
                  pl.BlockSpec((hl, MLA_KV_RANK), const)],
        out_specs=[pl.BlockSpec((MLA_HEADS, tq, 2 * LANES), lambda i: (0, i, 0)),
                   pl.BlockSpec((MLA_HEADS, tq, 2 * LANES), lambda i: (0, i, 0)),
                   pl.BlockSpec((MLA_HEADS, 1, MLA_V, tq), lambda i: (0, i, 0, 0))],
        out_shape=[jax.ShapeDtypeStruct((MLA_HEADS, r, 2 * LANES), BF16),
                   jax.ShapeDtypeStruct((MLA_HEADS, r, 2 * LANES), BF16),
                   jax.ShapeDtypeStruct((MLA_HEADS, r // tq, MLA_V, tq), BF16)],
        compiler_params=_params("parallel"),
        name="mla_prep",
    )(proj, cc, ss, q_norm.reshape(1, -1), kv_norm.reshape(1, -1), wqn, wqr, wqs, wkn, wvt)

    return pl.pallas_call(
        functools.partial(_attn_kernel, tq=tq),
        grid=(nb, MLA_HEADS, nq),
        in_specs=[pl.BlockSpec((1, tq, 2 * LANES), lambda b, h, i: (h, b * nq + i, 0)),
                  pl.BlockSpec((1, lb, 2 * LANES), lambda b, h, i: (h, b, 0)),
                  pl.BlockSpec((1, nq, MLA_V, tq), lambda b, h, i: (h, b, 0, 0))],
        out_specs=pl.BlockSpec((tq, LANES), lambda b, h, i: (b * nq + i, h)),
        out_shape=jax.ShapeDtypeStruct((r, hl), BF16),
        scratch_shapes=[pltpu.VMEM((1, tq), F32), pltpu.VMEM((1, tq), F32), pltpu.VMEM((MLA_V, tq), F32)],
        compiler_params=_params("parallel", "parallel", "arbitrary"),
        name="mla_attention",
    )(q, k, vt)


def _dummy_rows(rows, nb, lb, front):
    mask = rows < 0
    for b in range(nb):
        mask = mask | ((rows >= b * lb) & (rows < b * lb + front))
    return mask


def _conv_kernel(x_ref, halo_ref, w_ref, b_ref, o_ref, *, tm, nb, lb, front):
    i = pl.program_id(0)
    tc = x_ref.shape[1]
    rows = i * tm + lax.broadcasted_iota(jnp.int32, (tm, tc), 0)
    x = jnp.where(_dummy_rows(rows, nb, lb, front), 0.0, x_ref[...].astype(F32))
    hrows = i * tm - SUBLANES + lax.broadcasted_iota(jnp.int32, (SUBLANES, tc), 0)
    halo = jnp.where(_dummy_rows(hrows, nb, lb, front), 0.0, halo_ref[...].astype(F32))
    w = w_ref[...]
    k = w.shape[0]
    acc = b_ref[...] + w[k - 1:k, :] * x
    head = b_ref[...] + w[k - 1:k, :] * x[:SUBLANES]
    sub = lax.broadcasted_iota(jnp.int32, (SUBLANES, tc), 0)
    for d in range(1, k):
        wd = w[k - 1 - d:k - d, :]
        acc += wd * pltpu.roll(x, d, 0)
        shifted = jnp.where(sub < d, pltpu.roll(halo, d, 0), pltpu.roll(x[:SUBLANES], d, 0))
        head += wd * shifted
    o_ref[...] = (acc * jax.nn.sigmoid(acc)).astype(o_ref.dtype)
    o_ref[0:SUBLANES, :] = (head * jax.nn.sigmoid(head)).astype(o_ref.dtype)


def _ssd_kernel(xs_ref, bm_ref, cm_ref, bmt_ref, dtc_ref, dtr_ref, bc_ref, br_ref, af_ref, ar_ref,
                d_ref, z_ref, ng_ref, o_ref, state_ref, *, front):
    c = pl.program_id(2)
    q = xs_ref.shape[0]
    hp = xs_ref.shape[1]
    nh = dtr_ref.shape[1]
    hd = hp // nh
    wide = dtc_ref.shape[1]

    @pl.when(c == 0)
    def _():
        state_ref[...] = jnp.zeros(state_ref.shape, F32)

    first_live = jnp.where(c == 0, front, 0)
    live_c = lax.broadcasted_iota(jnp.int32, (q, wide), 0) >= first_live
    live_r = lax.broadcasted_iota(jnp.int32, (nh, q), 1) >= first_live
    dt_c = jnp.where(live_c, jax.nn.softplus(dtc_ref[...] + bc_ref[0]), 0.0)
    dt_r = jnp.where(live_r, jax.nn.softplus(dtr_ref[0] + br_ref[0]), 0.0)
    li = lax.broadcasted_iota(jnp.int32, (q, q), 0)
    si = lax.broadcasted_iota(jnp.int32, (q, q), 1)
    lower = li >= si
    e_row = lax.broadcasted_iota(jnp.int32, (wide, hp), 0)
    e_col = lax.broadcasted_iota(jnp.int32, (wide, hp), 1)
    expand = ((e_col >= e_row * hd) & (e_col < (e_row + 1) * hd)).astype(F32)
    dt_full = jnp.dot(dt_c, expand, precision=HI, preferred_element_type=F32)
    cs_full = jnp.dot(lower.astype(F32), dt_full * af_ref[0], precision=HI, preferred_element_type=F32)
    cs_r = jnp.dot(dt_r * ar_ref[0], (li <= si).astype(F32), precision=HI, preferred_element_type=F32)
    last = cs_full[q - 1:q, :]

    xs = xs_ref[...].astype(F32)
    xdt = xs * dt_full
    bm = bm_ref[...]
    cm = cm_ref[...]
    cb = lax.dot_general(cm, bm, (((1,), (1,)), ((), ())), preferred_element_type=F32)

    prev = state_ref[...]
    y = jnp.dot(cm, prev.astype(BF16), preferred_element_type=F32) * jnp.exp(cs_full)
    local = jnp.dot(bmt_ref[...], (xdt * jnp.exp(last - cs_full)).astype(BF16), preferred_element_type=F32)
    state_ref[...] = prev * jnp.exp(last) + local

    lane = lax.broadcasted_iota(jnp.int32, (q, 2 * hd), 1)
    diag = []
    for pr in range(nh // 2):
        ms = []
        for h in (2 * pr, 2 * pr + 1):
            seg = cs_full[:, h * hd:h * hd + 1] - cs_r[h:h + 1, :]
            ms.append((cb * jnp.exp(jnp.where(lower, seg, -jnp.inf))).astype(BF16))
        xp = xdt[:, pr * 2 * hd:(pr + 1) * 2 * hd]
        rhs = jnp.concatenate([jnp.where(lane < hd, xp, 0.0), jnp.where(lane >= hd, xp, 0.0)], axis=0)
        diag.append(jnp.dot(jnp.concatenate(ms, axis=1), rhs.astype(BF16), preferred_element_type=F32))
    y = y + jnp.concatenate(diag, axis=1) + d_ref[0] * xs

    z = z_ref[...].astype(F32)
    y = y * (z * jax.nn.sigmoid(z))
    y = y * lax.rsqrt(jnp.mean(y * y, axis=-1, keepdims=True) + RMS_EPS) * ng_ref[...]
    o_ref[...] = y.astype(o_ref.dtype)


def _mamba2_mixer(h16, nb, lb, front, w_in, conv_w, conv_b, dt_bias, a_log, d, norm_g):
    r = h16.shape[0]
    inner = norm_g.shape[0]
    gn = SSD_GROUPS * SSD_STATE
    cdim = inner + 2 * gn
    tm = _pick(r, (1664, 1280, 640, 128))
    w16 = w_in.astype(BF16)
    z = _matmul(h16, w16[:, :inner], BF16, tm=tm, tn=_pick(inner, (1024, 512, 128)), name="l1_in_z")
    xbc = _matmul(h16, w16[:, inner:inner + cdim], BF16, tm=tm, tn=_pick(cdim, (1024, 512, 128)),
                  name="l1_in_xbc")
    w_dt = w16[:, inner + cdim:].reshape(-1, SSD_GROUPS, SSD_HPG)
    w_dt = jnp.concatenate([w_dt, jnp.zeros(w_dt.shape[:2] + (LANES - SSD_HPG,), BF16)], axis=2)
    dt_raw = _matmul(h16, w_dt.reshape(-1, SSD_GROUPS * LANES), F32, tm=tm, tn=SSD_GROUPS * LANES,
                     name="l1_in_dt")

    tcv = _pick(cdim, (512, 256, 128))
    blocks = tm // SUBLANES
    act = pl.pallas_call(
        functools.partial(_conv_kernel, tm=tm, nb=nb, lb=lb, front=front),
        grid=(r // tm, cdim // tcv),
        in_specs=[pl.BlockSpec((tm, tcv), lambda i, j: (i, j)),
                  pl.BlockSpec((SUBLANES, tcv), lambda i, j: (jnp.maximum(i * blocks - 1, 0), j)),
                  pl.BlockSpec((SSD_CONV, tcv), lambda i, j: (0, j)),
                  pl.BlockSpec((1, tcv), lambda i, j: (0, j))],
        out_specs=pl.BlockSpec((tm, tcv), lambda i, j: (i, j)),
        out_shape=jax.ShapeDtypeStruct((r, cdim), BF16),
        compiler_params=_params("parallel", "parallel"),
        name="ssd_conv",
    )(xbc, xbc, conv_w, conv_b.reshape(1, cdim))

    gsz = inner // SSD_GROUPS
    nc = lb // SSD_BLOCK
    bmt = act[:, inner:inner + gn].T
    dt_r = dt_raw.reshape(r, SSD_GROUPS, LANES)[:, :, :SSD_HPG].transpose(1, 2, 0)
    bias = dt_bias.reshape(SSD_GROUPS, SSD_HPG)
    bias_c = jnp.concatenate([bias, jnp.zeros((SSD_GROUPS, LANES - SSD_HPG), F32)], axis=1)
    a = -jnp.exp(a_log).reshape(SSD_GROUPS, SSD_HPG)
    a_full = jnp.repeat(a, SSD_HEAD_DIM, axis=1).reshape(SSD_GROUPS, 1, gsz)
    d_full = jnp.repeat(d, SSD_HEAD_DIM).reshape(SSD_GROUPS, 1, gsz)
    xg = inner // gsz
    row = lambda b, g, c: b * nc + c
    return pl.pallas_call(
        functools.partial(_ssd_kernel, front=front),
        grid=(nb, SSD_GROUPS, nc),
        in_specs=[pl.BlockSpec((SSD_BLOCK, gsz), lambda b, g, c: (row(b, g, c), g)),
                  pl.BlockSpec((SSD_BLOCK, SSD_STATE), lambda b, g, c: (row(b, g, c), xg * gsz // SSD_STATE + g)),
                  pl.BlockSpec((SSD_BLOCK, SSD_STATE),
                               lambda b, g, c: (row(b, g, c), (inner + gn) // SSD_STATE + g)),
                  pl.BlockSpec((SSD_STATE, SSD_BLOCK), lambda b, g, c: (g, row(b, g, c))),
                  pl.BlockSpec((SSD_BLOCK, LANES), lambda b, g, c: (row(b, g, c), g)),
                  pl.BlockSpec((1, SSD_HPG, SSD_BLOCK), lambda b, g, c: (g, 0, row(b, g, c))),
                  pl.BlockSpec((1, 1, LANES), lambda b, g, c: (g, 0, 0)),
                  pl.BlockSpec((1, SSD_HPG, 1), lambda b, g, c: (g, 0, 0)),
                  pl.BlockSpec((1, 1, gsz), lambda b, g, c: (g, 0, 0)),
                  pl.BlockSpec((1, SSD_HPG, 1), lambda b, g, c: (g, 0, 0)),
                  pl.BlockSpec((1, 1, gsz), lambda b, g, c: (g, 0, 0)),
                  pl.BlockSpec((SSD_BLOCK, gsz), lambda b, g, c: (row(b, g, c), g)),
                  pl.BlockSpec((1, gsz), lambda b, g, c: (0, g))],
        out_specs=pl.BlockSpec((SSD_BLOCK, gsz), lambda b, g, c: (row(b, g, c), g)),
        out_shape=jax.ShapeDtypeStruct((r, inner), BF16),
        scratch_shapes=[pltpu.VMEM((SSD_STATE, gsz), F32)],
        compiler_params=_params("parallel", "parallel", "arbitrary"),
        name="ssd_scan",
    )(act, act, act, bmt, dt_raw, dt_r,
      bias_c[:, None, :], bias[:, :, None], a_full, a[:, :, None],
      d_full, z, norm_g.reshape(1, inner))


def kernel(x, meta_tokens, l0_w_in, l0_s5_log_dt, l0_s5_a_re, l0_s5_a_im, l0_s5_b_re, l0_s5_b_im, l0_s5_c_re, l0_s5_c_im, l0_s5_d, l0_s5_w_glu, l0_mla_q_norm, l0_mla_w_uq, l0_mla_kv_norm, l0_mla_w_ukv, l0_w_out, l0_ln1_g, l0_ln1_b, l0_ffn_w_gate, l0_ffn_w_up, l0_ffn_w_down, l0_ln2_g, l0_ln2_b, l1_w_in, l1_conv_w, l1_conv_b, l1_dt_bias, l1_a_log, l1_d, l1_norm_g, l1_w_out, l1_ln1_g, l1_ln1_b, l1_ffn_w_gate, l1_ffn_w_up, l1_ffn_w_down, l1_ln2_g, l1_ln2_b):
    nb, seq, dm = x.shape
    n_meta = meta_tokens.shape[0]
    front = (-n_meta) % SSD_BLOCK
    lb = front + n_meta + seq
    r = nb * lb
    meta = jnp.broadcast_to(meta_tokens[None].astype(x.dtype), (nb, n_meta, dm))
    h32 = jnp.concatenate([jnp.zeros((nb, front, dm), x.dtype), meta, x], axis=1).reshape(r, dm)
    h16 = h32.astype(BF16)

    s5w = l0_s5_d.shape[0]
    o3 = s5w + MLA_Q_RANK + MLA_KV_RANK
    kr = l0_w_in[:, o3:]
    half = MLA_ROPE // 2
    zpad = jnp.zeros((dm, LANES - MLA_ROPE), l0_w_in.dtype)
    w0 = jnp.concatenate([l0_w_in[:, :o3], kr, zpad, kr[:, half:], kr[:, :half], zpad], axis=1).astype(BF16)
    tm = _pick(r, (1664, 1280, 640, 128))
    proj = _matmul(h16, w0, F32, tm=tm, tn=_pick(w0.shape[1], (1024, 512, 128)), name="l0_in")
    a_out = _s5_mixer(proj, nb, lb, front, l0_s5_log_dt, l0_s5_a_re, l0_s5_a_im, l0_s5_b_re, l0_s5_b_im,
                      l0_s5_c_re, l0_s5_c_im, l0_s5_d, l0_s5_w_glu)
    b_out = _mla_mixer(proj, nb, lb, front, l0_mla_q_norm, l0_mla_w_uq, l0_mla_kv_norm, l0_mla_w_ukv)
    mix = jnp.concatenate([a_out, b_out], axis=1)
    tml = _pick(r, (640, 128))
    h32, h16 = _mm_res_ln(mix, l0_w_out.astype(BF16), h32, l0_ln1_g, l0_ln1_b, tm=tml, tk=2048, name="l0_out_ln")
    h32, h16 = _ffn_block(h32, h16, l0_ffn_w_gate, l0_ffn_w_up, l0_ffn_w_down, l0_ln2_g, l0_ln2_b, "l0")

    y = _mamba2_mixer(h16, nb, lb, front, l1_w_in, l1_conv_w, l1_conv_b, l1_dt_bias, l1_a_log, l1_d, l1_norm_g)
    h32, h16 = _mm_res_ln(y, l1_w_out.astype(BF16), h32, l1_ln1_g, l1_ln1_b, tm=tml, tk=1024, name="l1_out_ln")
    h32, h16 = _ffn_block(h32, h16, l1_ffn_w_gate, l1_ffn_w_up, l1_ffn_w_down, l1_ln2_g, l1_ln2_b, "l1")
    return h32.reshape(nb, lb, dm)[:, front + n_meta:, :]
```

```python
import functools
import math

import jax
import jax.numpy as jnp
from jax import lax
from jax.experimental import pallas as pl
from jax.experimental.pallas import tpu as pltpu

F32 = jnp.float32
BF16 = jnp.bfloat16

DEPTH = 2
CHUNK = 64
N_META = 16
DN_ALPHA = (2 * DEPTH) ** 0.25
LN_EPS = 1e-5
RMS_EPS = 1e-6
S5_GROUP = 16
S5_STATE = 64
S5_Q = 16
S5_LANE_GROUPS = 128 // S5_GROUP
MLA_HEADS = 8
MLA_NOPE = 128
MLA_ROPE = 64
MLA_V = 128
MLA_Q_RANK = 512
MLA_KV_RANK = 256
ROPE_BASE = 10000.0
SSD_HEAD_DIM = 64
SSD_GROUPS = 8
SSD_HPG = 8
SSD_STATE = 128
SSD_CONV = 4
SSD_BLOCK = 128

LANES = 128
SUBLANES = 8
VMEM_LIMIT = 56 * 1024 * 1024
MASK_VALUE = -1e30
HI = lax.Precision.HIGHEST


def _pick(n, candidates):
    for c in candidates:
        if n % c == 0:
            return c
    raise ValueError(f"no tile for {n} in {candidates}")


def _params(*sem):
    return pltpu.CompilerParams(dimension_semantics=sem, vmem_limit_bytes=VMEM_LIMIT)


def _mm_kernel(x_ref, w_ref, o_ref):
    o_ref[...] = jnp.dot(x_ref[...], w_ref[...], preferred_element_type=F32).astype(o_ref.dtype)


def _matmul(x, w, out_dtype, *, tm, tn, name):
    m, k = x.shape
    n = w.shape[1]
    return pl.pallas_call(
        _mm_kernel,
        grid=(m // tm, n // tn),
        in_specs=[pl.BlockSpec((tm, k), lambda i, j: (i, 0)),
                  pl.BlockSpec((k, tn), lambda i, j: (0, j))],
        out_specs=pl.BlockSpec((tm, tn), lambda i, j: (i, j)),
        out_shape=jax.ShapeDtypeStruct((m, n), out_dtype),
        compiler_params=_params("parallel", "arbitrary"),
        name=name,
    )(x, w)


def _mm_res_ln_kernel(*refs, nk, parts):
    x_refs = refs[:parts]
    w_ref, res_ref, g_ref, b_ref, o32_ref, o16_ref, acc_ref = refs[parts:]
    k = pl.program_id(1)

    @pl.when(k == 0)
    def _():
        acc_ref[...] = jnp.zeros(acc_ref.shape, F32)

    if parts == 1:
        acc_ref[...] += jnp.dot(x_refs[0][...], w_ref[...], preferred_element_type=F32)
    else:
        for p in range(parts):
            @pl.when(k == p)
            def _(p=p):
                acc_ref[...] += jnp.dot(x_refs[p][...], w_ref[...], preferred_element_type=F32)

    @pl.when(k == nk - 1)
    def _():
        y = DN_ALPHA * res_ref[...] + acc_ref[...]
        mu = jnp.mean(y, axis=-1, keepdims=True)
        yc = y - mu
        var = jnp.mean(yc * yc, axis=-1, keepdims=True)
        out = yc * lax.rsqrt(var + LN_EPS) * g_ref[...] + b_ref[...]
        o32_ref[...] = out
        o16_ref[...] = out.astype(BF16)


def _mm_res_ln(x, w, res, g, b, *, tm, tk, name):
    xs = x if isinstance(x, tuple) else (x,)
    m = xs[0].shape[0]
    kk, d = w.shape
    nk = kk // tk
    if len(xs) > 1:
        assert len(xs) == nk and all(p.shape == (m, tk) for p in xs)
        x_specs = [pl.BlockSpec((tm, tk), lambda i, k: (i, 0)) for _ in xs]
    else:
        x_specs = [pl.BlockSpec((tm, tk), lambda i, k: (i, k))]
    return pl.pallas_call(
        functools.partial(_mm_res_ln_kernel, nk=nk, parts=len(xs)),
        grid=(m // tm, nk),
        in_specs=x_specs + [
                  pl.BlockSpec((tk, d), lambda i, k: (k, 0)),
                  pl.BlockSpec((tm, d), lambda i, k: (i, 0)),
                  pl.BlockSpec((1, d), lambda i, k: (0, 0)),
                  pl.BlockSpec((1, d), lambda i, k: (0, 0))],
        out_specs=[pl.BlockSpec((tm, d), lambda i, k: (i, 0)),
                   pl.BlockSpec((tm, d), lambda i, k: (i, 0))],
        out_shape=[jax.ShapeDtypeStruct((m, d), F32), jax.ShapeDtypeStruct((m, d), BF16)],
        scratch_shapes=[pltpu.VMEM((tm, d), F32)],
        compiler_params=_params("parallel", "arbitrary"),
        name=name,
    )(*xs, w, res, g.reshape(1, d), b.reshape(1, d))


def _ffn_up_kernel(x_ref, wg_ref, wu_ref, o_ref):
    x = x_ref[...]
    g = jnp.dot(x, wg_ref[...], preferred_element_type=F32)
    u = jnp.dot(x, wu_ref[...], preferred_element_type=F32)
    o_ref[...] = (g * jax.nn.sigmoid(g) * u).astype(o_ref.dtype)


def _ffn_up(x, wg, wu, *, tm, tf, name):
    m, d = x.shape
    f = wg.shape[1]
    return pl.pallas_call(
        _ffn_up_kernel,
        grid=(m // tm, f // tf),
        in_specs=[pl.BlockSpec((tm, d), lambda i, j: (i, 0)),
                  pl.BlockSpec((d, tf), lambda i, j: (0, j)),
                  pl.BlockSpec((d, tf), lambda i, j: (0, j))],
        out_specs=pl.BlockSpec((tm, tf), lambda i, j: (i, j)),
        out_shape=jax.ShapeDtypeStruct((m, f), BF16),
        compiler_params=_params("parallel", "arbitrary"),
        name=name,
    )(x, wg, wu)


def _ffn_block(h32, h16, w_gate, w_up, w_down, g, b, tag):
    m = h16.shape[0]
    f = w_gate.shape[1]
    hid = _ffn_up(h16, w_gate.astype(BF16), w_up.astype(BF16),
                  tm=_pick(m, (1664, 1280, 640, 128)), tf=_pick(f, (512, 256, 128)),
                  name=f"{tag}_ffn_up")
    return _mm_res_ln(hid, w_down.astype(BF16), h32, g, b,
                      tm=_pick(m, (640, 128)), tk=_pick(f, (1408, 512, 256, 128)), name=f"{tag}_ffn_down_ln")


def _s5_tables(log_dt, a_re, a_im, b_re, b_im, c_re, c_im):
    g_, p_ = a_re.shape
    j_ = b_re.shape[-1]
    q = S5_Q
    dt = jnp.exp(log_dt)[:, None]
    lr = dt * a_re
    li = dt * a_im
    ks = jnp.arange(q + 1, dtype=F32)[:, None, None]
    mag = jnp.exp(ks * lr)
    pw_re = mag * jnp.cos(ks * li)
    pw_im = mag * jnp.sin(ks * li)
    ab_re, ab_im = pw_re[1], pw_im[1]
    den = a_re * a_re + a_im * a_im
    nr = ab_re - 1.0
    f_re = (nr * a_re + ab_im * a_im) / den
    f_im = (ab_im * a_re - nr * a_im) / den
    bb_re = f_re[..., None] * b_re - f_im[..., None] * b_im
    bb_im = f_re[..., None] * b_im + f_im[..., None] * b_re
    cp_re = c_re[None] * pw_re[:, :, None, :] - c_im[None] * pw_im[:, :, None, :]
    cp_im = c_re[None] * pw_im[:, :, None, :] + c_im[None] * pw_re[:, :, None, :]
    kern = (jnp.einsum('lgjp,gpi->lgji', cp_re[:q], bb_re, precision=HI)
            - jnp.einsum('lgjp,gpi->lgji', cp_im[:q], bb_im, precision=HI))
    kr = (q - 1 - jnp.arange(q)).astype(F32)[:, None, None]
    rev_mag = jnp.exp(kr * lr)
    rev_re = (rev_mag * jnp.cos(kr * li))[:, :, :, None]
    rev_im = (rev_mag * jnp.sin(kr * li))[:, :, :, None]
    st_re = (rev_re * bb_re[None] - rev_im * bb_im[None]).transpose(1, 0, 3, 2)
    st_im = (rev_re * bb_im[None] + rev_im * bb_re[None]).transpose(1, 0, 3, 2)
    wo_re = cp_re[1:].transpose(1, 3, 0, 2)
    wo_im = (-cp_im[1:]).transpose(1, 3, 0, 2)

    gl = S5_LANE_GROUPS
    nblk = g_ // gl

    def block_diag(w, rows_per_group, cols_per_group):
        w = jnp.tile(w, (1,) * (w.ndim - 1) + (gl,))
        row_g = jnp.arange(gl * rows_per_group)[:, None] // rows_per_group
        col_g = jnp.arange(gl * cols_per_group)[None, :] // cols_per_group
        return jnp.where(row_g == col_g, w, 0.0)

    kc = kern.reshape(q, nblk, gl, j_, j_).transpose(1, 0, 2, 4, 3).reshape(nblk, q, gl * j_, j_)
    lag_b = block_diag(kc, j_, j_).astype(BF16)

    def st_blocks(st):
        w = st.reshape(nblk, gl, q, j_, p_).transpose(0, 2, 1, 3, 4).reshape(nblk, q, gl * j_, p_)
        return block_diag(w, j_, p_)

    st_b = jnp.concatenate([st_blocks(st_re), st_blocks(st_im)], axis=3)
    st_b = st_b.reshape(nblk, q * gl * j_, 2 * gl * p_).astype(BF16)

    def wo_blocks(wo):
        w = block_diag(wo.reshape(nblk, gl * p_, q, j_).transpose(0, 2, 1, 3), p_, j_)
        return w.transpose(0, 2, 1, 3).reshape(nblk, gl * p_, q * gl * j_).astype(BF16)

    aq_re = pw_re[q].reshape(1, g_ * p_)
    aq_im = pw_im[q].reshape(1, g_ * p_)
    return lag_b, st_b, wo_blocks(wo_re), wo_blocks(wo_im), aq_re, aq_im


def _s5_rows(u_ref, front_rows):
    rows = u_ref.shape[0] // S5_Q
    live = lax.broadcasted_iota(jnp.int32, (rows, u_ref.shape[1]), 0) >= front_rows
    pieces = [jnp.where(live, u_ref[pl.ds(i, rows, stride=S5_Q), :], 0.0) for i in range(S5_Q)]
    return pieces, jnp.concatenate([p.astype(BF16) for p in pieces], axis=1)


def _s5_local_kernel(u_ref, w_ref, zr_ref, zi_ref, *, front_rows):
    _, x = _s5_rows(u_ref, front_rows)
    z = jnp.dot(x, w_ref[0], preferred_element_type=F32)
    half = z.shape[1] // 2
    zr_ref[...] = z[:, :half]
    zi_ref[...] = z[:, half:]


def _s5_scan_kernel(zr_ref, zi_ref, ar_ref, ai_ref, sr_ref, si_ref, *, nb, nc):
    ar = ar_ref[...]
    ai = ai_ref[...]

    def body(c, carry):
        new = []
        for b in range(nb):
            s_re, s_im = carry[2 * b], carry[2 * b + 1]
            row = b * nc + c
            sr_ref[pl.ds(row, 1), :] = s_re
            si_ref[pl.ds(row, 1), :] = s_im
            z_re = zr_ref[pl.ds(row, 1), :]
            z_im = zi_ref[pl.ds(row, 1), :]
            new.append(ar * s_re - ai * s_im + z_re)
            new.append(ar * s_im + ai * s_re + z_im)
        return tuple(new)

    zero = jnp.zeros(ar.shape, F32)
    lax.fori_loop(0, nc, body, (zero,) * (2 * nb))


def _s5_out_kernel(u_ref, lag_ref, sr_ref, si_ref, wr_ref, wi_ref, d_ref, y_ref, toe_ref, *, front_rows):
    @pl.when(pl.program_id(1) == 0)
    def _():
        zero = jnp.zeros((LANES, LANES), toe_ref.dtype)
        for i in range(S5_Q):
            for j in range(S5_Q):
                toe_ref[i * LANES:(i + 1) * LANES, j * LANES:(j + 1) * LANES] = lag_ref[0, j - i] if j >= i else zero

    pieces, x = _s5_rows(u_ref, front_rows)
    rows = x.shape[0]
    y = jnp.dot(x, toe_ref[...], preferred_element_type=F32)
    y += jnp.dot(sr_ref[...].astype(BF16), wr_ref[0], preferred_element_type=F32)
    y += jnp.dot(si_ref[...].astype(BF16), wi_ref[0], preferred_element_type=F32)
    d = d_ref[...]
    for j in range(S5_Q):
        y_ref[pl.ds(j, rows, stride=S5_Q), :] = y[:, j * LANES:(j + 1) * LANES] + d * pieces[j]


def _glu_kernel(y_ref, w_ref, o_ref):
    g = jax.nn.gelu(y_ref[...])
    gate = jax.nn.sigmoid(jnp.dot(g.astype(BF16), w_ref[...], preferred_element_type=F32))
    o_ref[...] = (g * gate).astype(o_ref.dtype)


def _s5_mixer(proj, nb, lb, front, log_dt, a_re, a_im, b_re, b_im, c_re, c_im, d, w_glu):
    r = proj.shape[0]
    g_, p_ = a_re.shape
    w_ = g_ * S5_GROUP
    q = S5_Q
    m = r // q
    nc = lb // q
    nblk = w_ // LANES
    sw = S5_LANE_GROUPS * p_
    kw = q * LANES
    front_rows = front // q
    lag_b, st_b, wo_re_b, wo_im_b, aq_re, aq_im = _s5_tables(log_dt, a_re, a_im, b_re, b_im, c_re, c_im)

    z_re, z_im = pl.pallas_call(
        functools.partial(_s5_local_kernel, front_rows=front_rows),
        grid=(nblk, nb),
        in_specs=[pl.BlockSpec((lb, LANES), lambda c, b: (b, c)),
                  pl.BlockSpec((1, kw, 2 * sw), lambda c, b: (c, 0, 0))],
        out_specs=[pl.BlockSpec((nc, sw), lambda c, b: (b, c)),
                   pl.BlockSpec((nc, sw), lambda c, b: (b, c))],
        out_shape=[jax.ShapeDtypeStruct((m, g_ * p_), F32)] * 2,
        compiler_params=_params("parallel", "parallel"),
        name="s5_local",
    )(proj, st_b)

    cw = _pick(g_ * p_, (512, 256, 128))
    s_re, s_im = pl.pallas_call(
        functools.partial(_s5_scan_kernel, nb=nb, nc=nc),
        grid=(g_ * p_ // cw,),
        in_specs=[pl.BlockSpec((m, cw), lambda i: (0, i)),
                  pl.BlockSpec((m, cw), lambda i: (0, i)),
                  pl.BlockSpec((1, cw), lambda i: (0, i)),
                  pl.BlockSpec((1, cw), lambda i: (0, i))],
        out_specs=[pl.BlockSpec((m, cw), lambda i: (0, i)),
                   pl.BlockSpec((m, cw), lambda i: (0, i))],
        out_shape=[jax.ShapeDtypeStruct((m, g_ * p_), F32)] * 2,
        compiler_params=_params("parallel"),
        name="s5_scan",
    )(z_re, z_im, aq_re, aq_im)

    y = pl.pallas_call(
        functools.partial(_s5_out_kernel, front_rows=front_rows),
        grid=(nblk, nb),
        in_specs=[pl.BlockSpec((lb, LANES), lambda c, b: (b, c)),
                  pl.BlockSpec((1, q, LANES, LANES), lambda c, b: (c, 0, 0, 0)),
                  pl.BlockSpec((nc, sw), lambda c, b: (b, c)),
                  pl.BlockSpec((nc, sw), lambda c, b: (b, c)),
                  pl.BlockSpec((1, sw, kw), lambda c, b: (c, 0, 0)),
                  pl.BlockSpec((1, sw, kw), lambda c, b: (c, 0, 0)),
                  pl.BlockSpec((1, LANES), lambda c, b: (0, c))],
        out_specs=pl.BlockSpec((lb, LANES), lambda c, b: (b, c)),
        out_shape=jax.ShapeDtypeStruct((r, w_), F32),
        scratch_shapes=[pltpu.VMEM((kw, kw), BF16)],
        compiler_params=_params("parallel", "arbitrary"),
        name="s5_out",
    )(proj, lag_b, s_re, s_im, wo_re_b, wo_im_b, d.reshape(1, w_))

    tm = _pick(r, (832, 640, 128))
    return pl.pallas_call(
        _glu_kernel,
        grid=(r // tm,),
        in_specs=[pl.BlockSpec((tm, w_), lambda i: (i, 0)),
                  pl.BlockSpec((w_, w_), lambda i: (0, 0))],
        out_specs=pl.BlockSpec((tm, w_), lambda i: (i, 0)),
        out_shape=jax.ShapeDtypeStruct((r, w_), BF16),
        compiler_params=_params("parallel"),
        name="s5_glu",
    )(y, w_glu.astype(BF16))


def _rms(x, g):
    return x * lax.rsqrt(jnp.mean(x * x, axis=-1, keepdims=True) + RMS_EPS) * g


def _mla_prep_kernel(lat_ref, cc_ref, ss_ref, qg_ref, kg_ref, wqn_ref, wqr_ref, wqs_ref, wkn_ref, wvt_ref,
                     q_ref, k_ref, vt_ref, *, scale, tiles_per_batch, front):
    tm = lat_ref.shape[0]
    lat = lat_ref[...]
    o1 = MLA_Q_RANK
    o2 = o1 + MLA_KV_RANK
    o3 = o2 + LANES
    qn = _rms(lat[:, :o1], qg_ref[...]).astype(BF16)
    kvn = _rms(lat[:, o1:o2], kg_ref[...]).astype(BF16)
    cc = cc_ref[...]
    ss = ss_ref[...]
    lane = lax.broadcasted_iota(jnp.int32, (tm, LANES), 1)
    row = (pl.program_id(0) % tiles_per_batch) * tm + lax.broadcasted_iota(jnp.int32, (tm, LANES), 0)
    bias_lane = lane == MLA_ROPE
    k_rope = jnp.where(bias_lane & (row < front), MASK_VALUE, lat[:, o2:o3] * cc + lat[:, o3:] * ss).astype(BF16)
    q_nope = jnp.dot(qn, wqn_ref[...], preferred_element_type=F32) * scale
    q_r = jnp.dot(qn, wqr_ref[...], preferred_element_type=F32)
    q_s = jnp.dot(qn, wqs_ref[...], preferred_element_type=F32)
    k_nope = jnp.dot(kvn, wkn_ref[...], preferred_element_type=F32)
    v_t = lax.dot_general(wvt_ref[...], kvn, (((1,), (1,)), ((), ())), preferred_element_type=F32)
    for h in range(MLA_HEADS):
        sl = slice(h * LANES, (h + 1) * LANES)
        q_ref[h, :, 0:LANES] = q_nope[:, sl].astype(BF16)
        q_rope = jnp.where(bias_lane, 1.0, (q_r[:, sl] * cc + q_s[:, sl] * ss) * scale)
        q_ref[h, :, LANES:2 * LANES] = q_rope.astype(BF16)
        k_ref[h, :, 0:LANES] = k_nope[:, sl].astype(BF16)
        k_ref[h, :, LANES:2 * LANES] = k_rope
        vt_ref[h, 0] = v_t[h * MLA_V:(h + 1) * MLA_V, :].astype(BF16)


def _attn_kernel(q_ref, k_ref, vt_ref, o_ref, sa_ref, sb_ref, m_ref, l_ref, acc_ref, *, tq):
    i = pl.program_id(2)
    q = q_ref[0]
    m_ref[...] = jnp.full(m_ref.shape, MASK_VALUE, F32)
    l_ref[...] = jnp.zeros(l_ref.shape, F32)
    acc_ref[...] = jnp.zeros(acc_ref.shape, F32)
    shift = CHUNK.bit_length() - 1

    def scores(j):
        k = k_ref[0, pl.ds(pl.multiple_of(j * tq, tq), tq), :]
        return lax.dot_general(k, q, (((1,), (1,)), ((), ())), preferred_element_type=F32)

    def absorb(s_ref, j, diagonal):
        s = s_ref[...]
        if diagonal:
            k_chunk = lax.broadcasted_iota(jnp.int32, (tq, tq), 0) >> shift
            q_chunk = lax.broadcasted_iota(jnp.int32, (tq, tq), 1) >> shift
            s = jnp.where(k_chunk <= q_chunk, s, MASK_VALUE)
        m_prev = m_ref[...]
        m_new = jnp.maximum(m_prev, jnp.max(s, axis=0, keepdims=True))
        alpha = jnp.exp2(m_prev - m_new)
        p = jnp.exp2(s - m_new)
        l_ref[...] = alpha * l_ref[...] + jnp.sum(p, axis=0, keepdims=True)
        acc_ref[...] = alpha * acc_ref[...] + jnp.dot(vt_ref[0, j], p.astype(BF16), preferred_element_type=F32)
        m_ref[...] = m_new

    sa_ref[...] = scores(0)

    def body(t, carry):
        j = 2 * t
        sb_ref[...] = scores(j + 1)
        absorb(sa_ref, j, False)
        sa_ref[...] = scores(j + 2)
        absorb(sb_ref, j + 1, False)
        return carry

    lax.fori_loop(0, i // 2, body, 0)

    @pl.when(i % 2 == 0)
    def _():
        absorb(sa_ref, i, True)

    @pl.when(i % 2 == 1)
    def _():
        sb_ref[...] = scores(i)
        absorb(sa_ref, i - 1, False)
        absorb(sb_ref, i, True)

    o_ref[...] = (acc_ref[...] / l_ref[...]).T.astype(o_ref.dtype)


def _rope_tables(lb, front):
    pos = (jnp.arange(lb) - front).astype(F32)
    inv = ROPE_BASE ** (-jnp.arange(0, MLA_ROPE, 2, dtype=F32) / MLA_ROPE)
    ang = pos[:, None] * inv[None, :]
    cos, sin = jnp.cos(ang), jnp.sin(ang)
    zeros = jnp.zeros((lb, LANES - MLA_ROPE), F32)
    cc = jnp.concatenate([cos, cos, zeros], axis=1)
    ss = jnp.concatenate([-sin, sin, zeros], axis=1)
    return cc, ss


def _pad_heads(w, width, take):
    k = w.shape[0]
    w = w.reshape(k, MLA_HEADS, width)[:, :, take]
    pad = LANES - w.shape[-1]
    if pad:
        w = jnp.concatenate([w, jnp.zeros((k, MLA_HEADS, pad), w.dtype)], axis=-1)
    return w.reshape(k, MLA_HEADS * LANES).astype(BF16)


def _mla_mixer(proj, nb, lb, front, q_norm, w_uq, kv_norm, w_ukv):
    r = proj.shape[0]
    half = MLA_ROPE // 2
    qw = MLA_NOPE + MLA_ROPE
    rope_cols = jnp.arange(MLA_NOPE, qw)
    swap_cols = jnp.concatenate([rope_cols[half:], rope_cols[:half]])
    wqn = _pad_heads(w_uq, qw, jnp.arange(MLA_NOPE))
    wqr = _pad_heads(w_uq, qw, rope_cols)
    wqs = _pad_heads(w_uq, qw, swap_cols)
    wkn = _pad_heads(w_ukv, MLA_NOPE + MLA_V, jnp.arange(MLA_NOPE))
    wvt = _pad_heads(w_ukv, MLA_NOPE + MLA_V, jnp.arange(MLA_NOPE, MLA_NOPE + MLA_V)).T
    cc, ss = _rope_tables(lb, front)
    scale = qw ** -0.5 * math.log2(math.e)

    tq = _pick(lb, (640, 128))
    nq = lb // tq
    hl = MLA_HEADS * LANES
    lat_w = MLA_Q_RANK + MLA_KV_RANK + 2 * LANES
    const = lambda i: (0, 0)
    q, k, vt = pl.pallas_call(
        functools.partial(_mla_prep_kernel, scale=scale, tiles_per_batch=nq, front=front),
        grid=(r // tq,),
        in_specs=[pl.BlockSpec((tq, lat_w), lambda i: (i, 1)),
                  pl.BlockSpec((tq, LANES), lambda i: (i % nq, 0)),
                  pl.BlockSpec((tq, LANES), lambda i: (i % nq, 0)),
                  pl.BlockSpec((1, MLA_Q_RANK), const),
                  pl.BlockSpec((1, MLA_KV_RANK), const),
                  pl.BlockSpec((MLA_Q_RANK, hl), const),
                  pl.BlockSpec((MLA_Q_RANK, hl), const),
                  pl.BlockSpec((MLA_Q_RANK, hl), const),
                  pl.BlockSpec((MLA_KV_RANK, hl), const),
                  pl.BlockSpec((hl, MLA_KV_RANK), const)],
        out_specs=[pl.BlockSpec((MLA_HEADS, tq, 2 * LANES), lambda i: (0, i, 0)),
                   pl.BlockSpec((MLA_HEADS, tq, 2 * LANES), lambda i: (0, i, 0)),
                   pl.BlockSpec((MLA_HEADS, 1, MLA_V, tq), lambda i: (0, i, 0, 0))],
        out_shape=[jax.ShapeDtypeStruct((MLA_HEADS, r, 2 * LANES), BF16),
                   jax.ShapeDtypeStruct((MLA_HEADS, r, 2 * LANES), BF16),
                   jax.ShapeDtypeStruct((MLA_HEADS, r // tq, MLA_V, tq), BF16)],
        compiler_params=_params("parallel"),
        name="mla_prep",
    )(proj, cc, ss, q_norm.reshape(1, -1), kv_norm.reshape(1, -1), wqn, wqr, wqs, wkn, wvt)

    return pl.pallas_call(
        functools.partial(_attn_kernel, tq=tq),
        grid=(nb, MLA_HEADS, nq),
        in_specs=[pl.BlockSpec((1, tq, 2 * LANES), lambda b, h, i: (h, b * nq + i, 0)),
                  pl.BlockSpec((1, lb, 2 * LANES), lambda b, h, i: (h, b, 0)),
                  pl.BlockSpec((1, nq, MLA_V, tq), lambda b, h, i: (h, b, 0, 0))],
        out_specs=pl.BlockSpec((tq, LANES), lambda b, h, i: (b * nq + i, h)),
        out_shape=jax.ShapeDtypeStruct((r, hl), BF16),
        scratch_shapes=[pltpu.VMEM((tq, tq), F32), pltpu.VMEM((tq, tq), F32),
                        pltpu.VMEM((1, tq), F32), pltpu.VMEM((1, tq), F32), pltpu.VMEM((MLA_V, tq), F32)],
        compiler_params=_params("parallel", "parallel", "arbitrary"),
        name="mla_attention",
    )(q, k, vt)


def _dummy_rows(rows, nb, lb, front):
    mask = rows < 0
    for b in range(nb):
        mask = mask | ((rows >= b * lb) & (rows < b * lb + front))
    return mask


def _conv_kernel(x_ref, halo_ref, w_ref, b_ref, o_ref, *, tm, nb, lb, front):
    i = pl.program_id(0)
    tc = x_ref.shape[1]
    rows = i * tm + lax.broadcasted_iota(jnp.int32, (tm, tc), 0)
    x = jnp.where(_dummy_rows(rows, nb, lb, front), 0.0, x_ref[...].astype(F32))
    hrows = i * tm - SUBLANES + lax.broadcasted_iota(jnp.int32, (SUBLANES, tc), 0)
    halo = jnp.where(_dummy_rows(hrows, nb, lb, front), 0.0, halo_ref[...].astype(F32))
    w = w_ref[...]
    k = w.shape[0]
    acc = b_ref[...] + w[k - 1:k, :] * x
    head = b_ref[...] + w[k - 1:k, :] * x[:SUBLANES]
    sub = lax.broadcasted_iota(jnp.int32, (SUBLANES, tc), 0)
    for d in range(1, k):
        wd = w[k - 1 - d:k - d, :]
        acc += wd * pltpu.roll(x, d, 0)
        shifted = jnp.where(sub < d, pltpu.roll(halo, d, 0), pltpu.roll(x[:SUBLANES], d, 0))
        head += wd * shifted
    o_ref[...] = (acc * jax.nn.sigmoid(acc)).astype(o_ref.dtype)
    o_ref[0:SUBLANES, :] = (head * jax.nn.sigmoid(head)).astype(o_ref.dtype)


def _split3(x):
    hi = x.astype(BF16).astype(F32)
    mid = (x - hi).astype(BF16).astype(F32)
    lo = (x - hi - mid).astype(BF16).astype(F32)
    return hi, mid, lo


def _ssd_kernel(xs_ref, bm_ref, cm_ref, bmt_ref, dtc_ref, dtr_ref, bc_ref, br_ref, ac_ref, ar_ref,
                d_ref, z_ref, ng_ref, ex_ref, o_ref, state_ref, *, front):
    c = pl.program_id(2)
    q = xs_ref.shape[0]
    hp = xs_ref.shape[1]
    nh = dtr_ref.shape[1]
    hd = hp // nh
    wide = dtc_ref.shape[1]

    @pl.when(c == 0)
    def _():
        state_ref[...] = jnp.zeros(state_ref.shape, F32)

    first_live = jnp.where(c == 0, front, 0)
    live_c = lax.broadcasted_iota(jnp.int32, (q, wide), 0) >= first_live
    live_r = lax.broadcasted_iota(jnp.int32, (nh, q), 1) >= first_live
    dt_c = jnp.where(live_c, jax.nn.softplus(dtc_ref[...] + bc_ref[0]), 0.0)
    dt_r = jnp.where(live_r, jax.nn.softplus(dtr_ref[0] + br_ref[0]), 0.0)
    li = lax.broadcasted_iota(jnp.int32, (q, q), 0)
    si = lax.broadcasted_iota(jnp.int32, (q, q), 1)
    lower = li >= si
    expand3 = ex_ref[...]

    def terms(x, axis):
        return jnp.concatenate(_split3(x), axis=axis).astype(BF16)

    dt_full = jnp.dot(terms(dt_c, 1), expand3, preferred_element_type=F32)
    cs3 = jnp.dot(lower.astype(BF16), terms(dt_c * ac_ref[0], 1), preferred_element_type=F32)
    cs_c = cs3[:, :wide] + cs3[:, wide:2 * wide] + cs3[:, 2 * wide:]
    cs_full = jnp.dot(terms(cs_c, 1), expand3, preferred_element_type=F32)
    csr3 = jnp.dot(terms(dt_r * ar_ref[0], 0), (li <= si).astype(BF16), preferred_element_type=F32)
    cs_r = csr3[:nh] + csr3[nh:2 * nh] + csr3[2 * nh:]
    last = cs_full[q - 1:q, :]

    xs = xs_ref[...].astype(F32)
    xdt = xs * dt_full
    bm = bm_ref[...]
    cm = cm_ref[...]
    cb = lax.dot_general(cm, bm, (((1,), (1,)), ((), ())), preferred_element_type=F32)

    prev = state_ref[...]
    y = jnp.dot(cm, prev.astype(BF16), preferred_element_type=F32) * jnp.exp(cs_full)
    local = jnp.dot(bmt_ref[...], (xdt * jnp.exp(last - cs_full)).astype(BF16), preferred_element_type=F32)
    state_ref[...] = prev * jnp.exp(last) + local

    lane = lax.broadcasted_iota(jnp.int32, (q, 2 * hd), 1)
    diag = []
    for pr in range(nh // 2):
        ms = []
        for h in (2 * pr, 2 * pr + 1):
            seg = cs_full[:, h * hd:h * hd + 1] - cs_r[h:h + 1, :]
            ms.append((cb * jnp.exp(jnp.where(lower, seg, -jnp.inf))).astype(BF16))
        xp = xdt[:, pr * 2 * hd:(pr + 1) * 2 * hd]
        rhs = jnp.concatenate([jnp.where(lane < hd, xp, 0.0), jnp.where(lane >= hd, xp, 0.0)], axis=0)
        diag.append(jnp.dot(jnp.concatenate(ms, axis=1), rhs.astype(BF16), preferred_element_type=F32))
    y = y + jnp.concatenate(diag, axis=1) + d_ref[0] * xs

    z = z_ref[...].astype(F32)
    y = y * (z * jax.nn.sigmoid(z))
    y = y * lax.rsqrt(jnp.mean(y * y, axis=-1, keepdims=True) + RMS_EPS) * ng_ref[...]
    o_ref[...] = y.astype(o_ref.dtype)


def _mamba2_mixer(h16, nb, lb, front, w_in, conv_w, conv_b, dt_bias, a_log, d, norm_g):
    r = h16.shape[0]
    inner = norm_g.shape[0]
    gn = SSD_GROUPS * SSD_STATE
    cdim = inner + 2 * gn
    tm = _pick(r, (1664, 1280, 640, 128))
    w16 = w_in.astype(BF16)
    z = _matmul(h16, w16[:, :inner], BF16, tm=tm, tn=_pick(inner, (1024, 512, 128)), name="l1_in_z")
    xbc = _matmul(h16, w16[:, inner:inner + cdim], BF16, tm=tm, tn=_pick(cdim, (1024, 512, 128)),
                  name="l1_in_xbc")
    w_dt = w16[:, inner + cdim:].reshape(-1, SSD_GROUPS, SSD_HPG)
    w_dt = jnp.concatenate([w_dt, jnp.zeros(w_dt.shape[:2] + (LANES - SSD_HPG,), BF16)], axis=2)
    dt_raw = _matmul(h16, w_dt.reshape(-1, SSD_GROUPS * LANES), F32, tm=tm, tn=SSD_GROUPS * LANES,
                     name="l1_in_dt")

    tcv = _pick(cdim, (512, 256, 128))
    blocks = tm // SUBLANES
    act = pl.pallas_call(
        functools.partial(_conv_kernel, tm=tm, nb=nb, lb=lb, front=front),
        grid=(r // tm, cdim // tcv),
        in_specs=[pl.BlockSpec((tm, tcv), lambda i, j: (i, j)),
                  pl.BlockSpec((SUBLANES, tcv), lambda i, j: (jnp.maximum(i * blocks - 1, 0), j)),
                  pl.BlockSpec((SSD_CONV, tcv), lambda i, j: (0, j)),
                  pl.BlockSpec((1, tcv), lambda i, j: (0, j))],
        out_specs=pl.BlockSpec((tm, tcv), lambda i, j: (i, j)),
        out_shape=jax.ShapeDtypeStruct((r, cdim), BF16),
        compiler_params=_params("parallel", "parallel"),
        name="ssd_conv",
    )(xbc, xbc, conv_w, conv_b.reshape(1, cdim))

    gsz = inner // SSD_GROUPS
    nc = lb // SSD_BLOCK
    bmt = act[:, inner:inner + gn].T
    dt_r = dt_raw.reshape(r, SSD_GROUPS, LANES)[:, :, :SSD_HPG].transpose(1, 2, 0)
    bias = dt_bias.reshape(SSD_GROUPS, SSD_HPG)
    bias_c = jnp.concatenate([bias, jnp.zeros((SSD_GROUPS, LANES - SSD_HPG), F32)], axis=1)
    a = -jnp.exp(a_log).reshape(SSD_GROUPS, SSD_HPG)
    a_c = jnp.concatenate([a, jnp.zeros((SSD_GROUPS, LANES - SSD_HPG), F32)], axis=1)
    d_full = jnp.repeat(d, SSD_HEAD_DIM).reshape(SSD_GROUPS, 1, gsz)
    expand = (jnp.arange(LANES)[:, None] == jnp.arange(gsz)[None, :] // SSD_HEAD_DIM).astype(BF16)
    expand3 = jnp.concatenate([expand] * 3, axis=0)
    xg = inner // gsz
    row = lambda b, g, c: b * nc + c
    return pl.pallas_call(
        functools.partial(_ssd_kernel, front=front),
        grid=(nb, SSD_GROUPS, nc),
        in_specs=[pl.BlockSpec((SSD_BLOCK, gsz), lambda b, g, c: (row(b, g, c), g)),
                  pl.BlockSpec((SSD_BLOCK, SSD_STATE), lambda b, g, c: (row(b, g, c), xg * gsz // SSD_STATE + g)),
                  pl.BlockSpec((SSD_BLOCK, SSD_STATE),
                               lambda b, g, c: (row(b, g, c), (inner + gn) // SSD_STATE + g)),
                  pl.BlockSpec((SSD_STATE, SSD_BLOCK), lambda b, g, c: (g, row(b, g, c))),
                  pl.BlockSpec((SSD_BLOCK, LANES), lambda b, g, c: (row(b, g, c), g)),
                  pl.BlockSpec((1, SSD_HPG, SSD_BLOCK), lambda b, g, c: (g, 0, row(b, g, c))),
                  pl.BlockSpec((1, 1, LANES), lambda b, g, c: (g, 0, 0)),
                  pl.BlockSpec((1, SSD_HPG, 1), lambda b, g, c: (g, 0, 0)),
                  pl.BlockSpec((1, 1, LANES), lambda b, g, c: (g, 0, 0)),
                  pl.BlockSpec((1, SSD_HPG, 1), lambda b, g, c: (g, 0, 0)),
                  pl.BlockSpec((1, 1, gsz), lambda b, g, c: (g, 0, 0)),
                  pl.BlockSpec((SSD_BLOCK, gsz), lambda b, g, c: (row(b, g, c), g)),
                  pl.BlockSpec((1, gsz), lambda b, g, c: (0, g)),
                  pl.BlockSpec((3 * LANES, gsz), lambda b, g, c: (0, 0))],
        out_specs=pl.BlockSpec((SSD_BLOCK, gsz), lambda b, g, c: (row(b, g, c), g)),
        out_shape=jax.ShapeDtypeStruct((r, inner), BF16),
        scratch_shapes=[pltpu.VMEM((SSD_STATE, gsz), F32)],
        compiler_params=_params("parallel", "parallel", "arbitrary"),
        name="ssd_scan",
    )(act, act, act, bmt, dt_raw, dt_r,
      bias_c[:, None, :], bias[:, :, None], a_c[:, None, :], a[:, :, None],
      d_full, z, norm_g.reshape(1, inner), expand3)


def kernel(x, meta_tokens, l0_w_in, l0_s5_log_dt, l0_s5_a_re, l0_s5_a_im, l0_s5_b_re, l0_s5_b_im, l0_s5_c_re, l0_s5_c_im, l0_s5_d, l0_s5_w_glu, l0_mla_q_norm, l0_mla_w_uq, l0_mla_kv_norm, l0_mla_w_ukv, l0_w_out, l0_ln1_g, l0_ln1_b, l0_ffn_w_gate, l0_ffn_w_up, l0_ffn_w_down, l0_ln2_g, l0_ln2_b, l1_w_in, l1_conv_w, l1_conv_b, l1_dt_bias, l1_a_log, l1_d, l1_norm_g, l1_w_out, l1_ln1_g, l1_ln1_b, l1_ffn_w_gate, l1_ffn_w_up, l1_ffn_w_down, l1_ln2_g, l1_ln2_b):
    nb, seq, dm = x.shape
    n_meta = meta_tokens.shape[0]
    front = (-n_meta) % SSD_BLOCK
    lb = front + n_meta + seq
    r = nb * lb
    meta = jnp.broadcast_to(meta_tokens[None].astype(x.dtype), (nb, n_meta, dm))
    h32 = jnp.concatenate([jnp.zeros((nb, front, dm), x.dtype), meta, x], axis=1).reshape(r, dm)
    h16 = h32.astype(BF16)

    s5w = l0_s5_d.shape[0]
    o3 = s5w + MLA_Q_RANK + MLA_KV_RANK
    kr = l0_w_in[:, o3:]
    half = MLA_ROPE // 2
    zpad = jnp.zeros((dm, LANES - MLA_ROPE), l0_w_in.dtype)
    w0 = jnp.concatenate([l0_w_in[:, :o3], kr, zpad, kr[:, half:], kr[:, :half], zpad], axis=1).astype(BF16)
    tm = _pick(r, (1664, 1280, 640, 128))
    proj = _matmul(h16, w0, F32, tm=tm, tn=_pick(w0.shape[1], (1024, 512, 128)), name="l0_in")
    a_out = _s5_mixer(proj, nb, lb, front, l0_s5_log_dt, l0_s5_a_re, l0_s5_a_im, l0_s5_b_re, l0_s5_b_im,
                      l0_s5_c_re, l0_s5_c_im, l0_s5_d, l0_s5_w_glu)
    b_out = _mla_mixer(proj, nb, lb, front, l0_mla_q_norm, l0_mla_w_uq, l0_mla_kv_norm, l0_mla_w_ukv)
    tml = _pick(r, (640, 128))
    h32, h16 = _mm_res_ln((a_out, b_out), l0_w_out.astype(BF16), h32, l0_ln1_g, l0_ln1_b, tm=tml,
                          tk=a_out.shape[1], name="l0_out_ln")
    h32, h16 = _ffn_block(h32, h16, l0_ffn_w_gate, l0_ffn_w_up, l0_ffn_w_down, l0_ln2_g, l0_ln2_b, "l0")

    y = _mamba2_mixer(h16, nb, lb, front, l1_w_in, l1_conv_w, l1_conv_b, l1_dt_bias, l1_a_log, l1_d, l1_norm_g)
    h32, h16 = _mm_res_ln(y, l1_w_out.astype(BF16), h32, l1_ln1_g, l1_ln1_b, tm=tml, tk=1024, name="l1_out_ln")
    h32, h16 = _ffn_block(h32, h16, l1_ffn_w_gate, l1_ffn_w_up, l1_ffn_w_down, l1_ln2_g, l1_ln2_b, "l1")
    return h32.reshape(nb, lb, dm)[:, front + n_meta:, :]
```

```python
import functools
import math

import jax
import jax.numpy as jnp
from jax import lax
from jax.experimental import pallas as pl
from jax.experimental.pallas import tpu as pltpu

F32 = jnp.float32
BF16 = jnp.bfloat16

DEPTH = 2
CHUNK = 64
N_META = 16
DN_ALPHA = (2 * DEPTH) ** 0.25
LN_EPS = 1e-5
RMS_EPS = 1e-6
S5_GROUP = 16
S5_STATE = 64
S5_Q = 16
S5_LANE_GROUPS = 128 // S5_GROUP
MLA_HEADS = 8
MLA_NOPE = 128
MLA_ROPE = 64
MLA_V = 128
MLA_Q_RANK = 512
MLA_KV_RANK = 256
ROPE_BASE = 10000.0
SSD_HEAD_DIM = 64
SSD_GROUPS = 8
SSD_HPG = 8
SSD_STATE = 128
SSD_CONV = 4
SSD_BLOCK = 128

LANES = 128
SUBLANES = 8
VMEM_LIMIT = 56 * 1024 * 1024
MASK_VALUE = -1e30
HI = lax.Precision.HIGHEST


def _pick(n, candidates):
    for c in candidates:
        if n % c == 0:
            return c
    raise ValueError(f"no tile for {n} in {candidates}")


def _params(*sem):
    return pltpu.CompilerParams(dimension_semantics=sem, vmem_limit_bytes=VMEM_LIMIT)


def _mm_kernel(x_ref, w_ref, o_ref):
    o_ref[...] = jnp.dot(x_ref[...], w_ref[...], preferred_element_type=F32).astype(o_ref.dtype)


def _matmul(x, w, out_dtype, *, tm, tn, name):
    m, k = x.shape
    n = w.shape[1]
    return pl.pallas_call(
        _mm_kernel,
        grid=(m // tm, n // tn),
        in_specs=[pl.BlockSpec((tm, k), lambda i, j: (i, 0)),
                  pl.BlockSpec((k, tn), lambda i, j: (0, j))],
        out_specs=pl.BlockSpec((tm, tn), lambda i, j: (i, j)),
        out_shape=jax.ShapeDtypeStruct((m, n), out_dtype),
        compiler_params=_params("parallel", "arbitrary"),
        name=name,
    )(x, w)


def _mm_ws_kernel(x_ref, w_ref, o_ref, wb_ref):
    @pl.when(pl.program_id(1) == 0)
    def _():
        wb_ref[...] = w_ref[...].astype(BF16)

    o_ref[...] = jnp.dot(x_ref[...], wb_ref[...], preferred_element_type=F32).astype(o_ref.dtype)


def _matmul_ws(x, w, out_dtype, *, col0, ncols, tm, tn, name):
    m, k = x.shape
    assert col0 % tn == 0 and ncols % tn == 0
    c0 = col0 // tn
    return pl.pallas_call(
        _mm_ws_kernel,
        grid=(ncols // tn, m // tm),
        in_specs=[pl.BlockSpec((tm, k), lambda j, i: (i, 0)),
                  pl.BlockSpec((k, tn), lambda j, i: (0, c0 + j))],
        out_specs=pl.BlockSpec((tm, tn), lambda j, i: (i, j)),
        out_shape=jax.ShapeDtypeStruct((m, ncols), out_dtype),
        scratch_shapes=[pltpu.VMEM((k, tn), BF16)],
        compiler_params=_params("parallel", "arbitrary"),
        name=name,
    )(x, w)


def _mm_res_ln_kernel(*refs, nk, parts):
    x_refs = refs[:parts]
    w_ref, res_ref, g_ref, b_ref, o32_ref, o16_ref, acc_ref = refs[parts:]
    k = pl.program_id(1)

    @pl.when(k == 0)
    def _():
        acc_ref[...] = jnp.zeros(acc_ref.shape, F32)

    if parts == 1:
        acc_ref[...] += jnp.dot(x_refs[0][...], w_ref[...], preferred_element_type=F32)
    else:
        for p in range(parts):
            @pl.when(k == p)
            def _(p=p):
                acc_ref[...] += jnp.dot(x_refs[p][...], w_ref[...], preferred_element_type=F32)

    @pl.when(k == nk - 1)
    def _():
        y = DN_ALPHA * res_ref[...] + acc_ref[...]
        mu = jnp.mean(y, axis=-1, keepdims=True)
        yc = y - mu
        var = jnp.mean(yc * yc, axis=-1, keepdims=True)
        out = yc * lax.rsqrt(var + LN_EPS) * g_ref[...] + b_ref[...]
        o32_ref[...] = out
        o16_ref[...] = out.astype(BF16)


def _mm_res_ln(x, w, res, g, b, *, tm, tk, name):
    xs = x if isinstance(x, tuple) else (x,)
    m = xs[0].shape[0]
    kk, d = w.shape
    nk = kk // tk
    if len(xs) > 1:
        assert len(xs) == nk and all(p.shape == (m, tk) for p in xs)
        x_specs = [pl.BlockSpec((tm, tk), lambda i, k: (i, 0)) for _ in xs]
    else:
        x_specs = [pl.BlockSpec((tm, tk), lambda i, k: (i, k))]
    return pl.pallas_call(
        functools.partial(_mm_res_ln_kernel, nk=nk, parts=len(xs)),
        grid=(m // tm, nk),
        in_specs=x_specs + [
                  pl.BlockSpec((tk, d), lambda i, k: (k, 0)),
                  pl.BlockSpec((tm, d), lambda i, k: (i, 0)),
                  pl.BlockSpec((1, d), lambda i, k: (0, 0)),
                  pl.BlockSpec((1, d), lambda i, k: (0, 0))],
        out_specs=[pl.BlockSpec((tm, d), lambda i, k: (i, 0)),
                   pl.BlockSpec((tm, d), lambda i, k: (i, 0))],
        out_shape=[jax.ShapeDtypeStruct((m, d), F32), jax.ShapeDtypeStruct((m, d), BF16)],
        scratch_shapes=[pltpu.VMEM((tm, d), F32)],
        compiler_params=_params("parallel", "arbitrary"),
        name=name,
    )(*xs, w, res, g.reshape(1, d), b.reshape(1, d))


def _ffn_up_kernel(x_ref, wg_ref, wu_ref, o_ref, wgb_ref, wub_ref):
    @pl.when(pl.program_id(1) == 0)
    def _():
        wgb_ref[...] = wg_ref[...].astype(BF16)
        wub_ref[...] = wu_ref[...].astype(BF16)

    x = x_ref[...]
    g = jnp.dot(x, wgb_ref[...], preferred_element_type=F32)
    u = jnp.dot(x, wub_ref[...], preferred_element_type=F32)
    o_ref[...] = (g * jax.nn.sigmoid(g) * u).astype(o_ref.dtype)


def _ffn_up(x, wg, wu, *, tm, tf, name):
    m, d = x.shape
    f = wg.shape[1]
    return pl.pallas_call(
        _ffn_up_kernel,
        grid=(f // tf, m // tm),
        in_specs=[pl.BlockSpec((tm, d), lambda j, i: (i, 0)),
                  pl.BlockSpec((d, tf), lambda j, i: (0, j)),
                  pl.BlockSpec((d, tf), lambda j, i: (0, j))],
        out_specs=pl.BlockSpec((tm, tf), lambda j, i: (i, j)),
        out_shape=jax.ShapeDtypeStruct((m, f), BF16),
        scratch_shapes=[pltpu.VMEM((d, tf), BF16), pltpu.VMEM((d, tf), BF16)],
        compiler_params=_params("parallel", "arbitrary"),
        name=name,
    )(x, wg, wu)


def _ffn_block(h32, h16, w_gate, w_up, w_down, g, b, tag):
    m = h16.shape[0]
    f = w_gate.shape[1]
    hid = _ffn_up(h16, w_gate, w_up,
                  tm=_pick(m, (1664, 1280, 640, 128)), tf=_pick(f, (512, 256, 128)),
                  name=f"{tag}_ffn_up")
    return _mm_res_ln(hid, w_down.astype(BF16), h32, g, b,
                      tm=_pick(m, (640, 128)), tk=_pick(f, (1408, 512, 256, 128)), name=f"{tag}_ffn_down_ln")


def _s5_tables(log_dt, a_re, a_im, b_re, b_im, c_re, c_im):
    g_, p_ = a_re.shape
    j_ = b_re.shape[-1]
    q = S5_Q
    dt = jnp.exp(log_dt)[:, None]
    lr = dt * a_re
    li = dt * a_im
    ks = jnp.arange(q + 1, dtype=F32)[:, None, None]
    mag = jnp.exp(ks * lr)
    pw_re = mag * jnp.cos(ks * li)
    pw_im = mag * jnp.sin(ks * li)
    ab_re, ab_im = pw_re[1], pw_im[1]
    den = a_re * a_re + a_im * a_im
    nr = ab_re - 1.0
    f_re = (nr * a_re + ab_im * a_im) / den
    f_im = (ab_im * a_re - nr * a_im) / den
    bb_re = f_re[..., None] * b_re - f_im[..., None] * b_im
    bb_im = f_re[..., None] * b_im + f_im[..., None] * b_re
    cp_re = c_re[None] * pw_re[:, :, None, :] - c_im[None] * pw_im[:, :, None, :]
    cp_im = c_re[None] * pw_im[:, :, None, :] + c_im[None] * pw_re[:, :, None, :]
    kern = (jnp.einsum('lgjp,gpi->lgji', cp_re[:q], bb_re, precision=HI)
            - jnp.einsum('lgjp,gpi->lgji', cp_im[:q], bb_im, precision=HI))
    kr = (q - 1 - jnp.arange(q)).astype(F32)[:, None, None]
    rev_mag = jnp.exp(kr * lr)
    rev_re = (rev_mag * jnp.cos(kr * li))[:, :, :, None]
    rev_im = (rev_mag * jnp.sin(kr * li))[:, :, :, None]
    st_re = (rev_re * bb_re[None] - rev_im * bb_im[None]).transpose(1, 0, 3, 2)
    st_im = (rev_re * bb_im[None] + rev_im * bb_re[None]).transpose(1, 0, 3, 2)
    wo_re = cp_re[1:].transpose(1, 3, 0, 2)
    wo_im = (-cp_im[1:]).transpose(1, 3, 0, 2)

    gl = S5_LANE_GROUPS
    nblk = g_ // gl

    def block_diag(w, rows_per_group, cols_per_group):
        w = jnp.tile(w, (1,) * (w.ndim - 1) + (gl,))
        row_g = jnp.arange(gl * rows_per_group)[:, None] // rows_per_group
        col_g = jnp.arange(gl * cols_per_group)[None, :] // cols_per_group
        return jnp.where(row_g == col_g, w, 0.0)

    kc = kern.reshape(q, nblk, gl, j_, j_).transpose(1, 0, 2, 4, 3).reshape(nblk, q, gl * j_, j_)
    lag_b = block_diag(kc, j_, j_).astype(BF16)

    def st_blocks(st):
        w = st.reshape(nblk, gl, q, j_, p_).transpose(0, 2, 1, 3, 4).reshape(nblk, q, gl * j_, p_)
        return block_diag(w, j_, p_)

    st_b = jnp.concatenate([st_blocks(st_re), st_blocks(st_im)], axis=3)
    st_b = st_b.reshape(nblk, q * gl * j_, 2 * gl * p_).astype(BF16)

    def wo_blocks(wo):
        w = block_diag(wo.reshape(nblk, gl * p_, q, j_).transpose(0, 2, 1, 3), p_, j_)
        return w.transpose(0, 2, 1, 3).reshape(nblk, gl * p_, q * gl * j_).astype(BF16)

    aq_re = pw_re[q].reshape(1, g_ * p_)
    aq_im = pw_im[q].reshape(1, g_ * p_)
    return lag_b, st_b, wo_blocks(wo_re), wo_blocks(wo_im), aq_re, aq_im


def _s5_rows(u_ref, front_rows):
    rows = u_ref.shape[0] // S5_Q
    live = lax.broadcasted_iota(jnp.int32, (rows, u_ref.shape[1]), 0) >= front_rows
    pieces = [jnp.where(live, u_ref[pl.ds(i, rows, stride=S5_Q), :], 0.0) for i in range(S5_Q)]
    return pieces, jnp.concatenate([p.astype(BF16) for p in pieces], axis=1)


def _s5_local_kernel(u_ref, w_ref, zr_ref, zi_ref, *, front_rows):
    _, x = _s5_rows(u_ref, front_rows)
    z = jnp.dot(x, w_ref[0], preferred_element_type=F32)
    half = z.shape[1] // 2
    zr_ref[...] = z[:, :half]
    zi_ref[...] = z[:, half:]


def _s5_scan_kernel(zr_ref, zi_ref, ar_ref, ai_ref, sr_ref, si_ref, *, nb, nc):
    ar = ar_ref[...]
    ai = ai_ref[...]

    def body(c, carry):
        new = []
        for b in range(nb):
            s_re, s_im = carry[2 * b], carry[2 * b + 1]
            row = b * nc + c
            sr_ref[pl.ds(row, 1), :] = s_re
            si_ref[pl.ds(row, 1), :] = s_im
            z_re = zr_ref[pl.ds(row, 1), :]
            z_im = zi_ref[pl.ds(row, 1), :]
            new.append(ar * s_re - ai * s_im + z_re)
            new.append(ar * s_im + ai * s_re + z_im)
        return tuple(new)

    zero = jnp.zeros(ar.shape, F32)
    lax.fori_loop(0, nc, body, (zero,) * (2 * nb))


def _s5_out_kernel(u_ref, lag_ref, sr_ref, si_ref, wr_ref, wi_ref, d_ref, y_ref, toe_ref, *, front_rows):
    @pl.when(pl.program_id(1) == 0)
    def _():
        zero = jnp.zeros((LANES, LANES), toe_ref.dtype)
        for i in range(S5_Q):
            for j in range(S5_Q):
                toe_ref[i * LANES:(i + 1) * LANES, j * LANES:(j + 1) * LANES] = lag_ref[0, j - i] if j >= i else zero

    pieces, x = _s5_rows(u_ref, front_rows)
    rows = x.shape[0]
    y = jnp.dot(x, toe_ref[...], preferred_element_type=F32)
    y += jnp.dot(sr_ref[...].astype(BF16), wr_ref[0], preferred_element_type=F32)
    y += jnp.dot(si_ref[...].astype(BF16), wi_ref[0], preferred_element_type=F32)
    d = d_ref[...]
    for j in range(S5_Q):
        y_ref[pl.ds(j, rows, stride=S5_Q), :] = y[:, j * LANES:(j + 1) * LANES] + d * pieces[j]


def _glu_kernel(y_ref, w_ref, o_ref):
    g = jax.nn.gelu(y_ref[...])
    gate = jax.nn.sigmoid(jnp.dot(g.astype(BF16), w_ref[...], preferred_element_type=F32))
    o_ref[...] = (g * gate).astype(o_ref.dtype)


def _s5_mixer(proj, nb, lb, front, log_dt, a_re, a_im, b_re, b_im, c_re, c_im, d, w_glu):
    r = proj.shape[0]
    g_, p_ = a_re.shape
    w_ = g_ * S5_GROUP
    q = S5_Q
    m = r // q
    nc = lb // q
    nblk = w_ // LANES
    sw = S5_LANE_GROUPS * p_
    kw = q * LANES
    front_rows = front // q
    lag_b, st_b, wo_re_b, wo_im_b, aq_re, aq_im = _s5_tables(log_dt, a_re, a_im, b_re, b_im, c_re, c_im)

    z_re, z_im = pl.pallas_call(
        functools.partial(_s5_local_kernel, front_rows=front_rows),
        grid=(nblk, nb),
        in_specs=[pl.BlockSpec((lb, LANES), lambda c, b: (b, c)),
                  pl.BlockSpec((1, kw, 2 * sw), lambda c, b: (c, 0, 0))],
        out_specs=[pl.BlockSpec((nc, sw), lambda c, b: (b, c)),
                   pl.BlockSpec((nc, sw), lambda c, b: (b, c))],
        out_shape=[jax.ShapeDtypeStruct((m, g_ * p_), F32)] * 2,
        compiler_params=_params("parallel", "parallel"),
        name="s5_local",
    )(proj, st_b)

    cw = _pick(g_ * p_, (512, 256, 128))
    s_re, s_im = pl.pallas_call(
        functools.partial(_s5_scan_kernel, nb=nb, nc=nc),
        grid=(g_ * p_ // cw,),
        in_specs=[pl.BlockSpec((m, cw), lambda i: (0, i)),
                  pl.BlockSpec((m, cw), lambda i: (0, i)),
                  pl.BlockSpec((1, cw), lambda i: (0, i)),
                  pl.BlockSpec((1, cw), lambda i: (0, i))],
        out_specs=[pl.BlockSpec((m, cw), lambda i: (0, i)),
                   pl.BlockSpec((m, cw), lambda i: (0, i))],
        out_shape=[jax.ShapeDtypeStruct((m, g_ * p_), F32)] * 2,
        compiler_params=_params("parallel"),
        name="s5_scan",
    )(z_re, z_im, aq_re, aq_im)

    y = pl.pallas_call(
        functools.partial(_s5_out_kernel, front_rows=front_rows),
        grid=(nblk, nb),
        in_specs=[pl.BlockSpec((lb, LANES), lambda c, b: (b, c)),
                  pl.BlockSpec((1, q, LANES, LANES), lambda c, b: (c, 0, 0, 0)),
                  pl.BlockSpec((nc, sw), lambda c, b: (b, c)),
                  pl.BlockSpec((nc, sw), lambda c, b: (b, c)),
                  pl.BlockSpec((1, sw, kw), lambda c, b: (c, 0, 0)),
                  pl.BlockSpec((1, sw, kw), lambda c, b: (c, 0, 0)),
                  pl.BlockSpec((1, LANES), lambda c, b: (0, c))],
        out_specs=pl.BlockSpec((lb, LANES), lambda c, b: (b, c)),
        out_shape=jax.ShapeDtypeStruct((r, w_), F32),
        scratch_shapes=[pltpu.VMEM((kw, kw), BF16)],
        compiler_params=_params("parallel", "arbitrary"),
        name="s5_out",
    )(proj, lag_b, s_re, s_im, wo_re_b, wo_im_b, d.reshape(1, w_))

    tm = _pick(r, (832, 640, 128))
    return pl.pallas_call(
        _glu_kernel,
        grid=(r // tm,),
        in_specs=[pl.BlockSpec((tm, w_), lambda i: (i, 0)),
                  pl.BlockSpec((w_, w_), lambda i: (0, 0))],
        out_specs=pl.BlockSpec((tm, w_), lambda i: (i, 0)),
        out_shape=jax.ShapeDtypeStruct((r, w_), BF16),
        compiler_params=_params("parallel"),
        name="s5_glu",
    )(y, w_glu.astype(BF16))


def _rms(x, g):
    return x * lax.rsqrt(jnp.mean(x * x, axis=-1, keepdims=True) + RMS_EPS) * g


def _mla_prep_kernel(lat_ref, cc_ref, ss_ref, qg_ref, kg_ref, wqn_ref, wqr_ref, wqs_ref, wkn_ref, wvt_ref,
                     q_ref, k_ref, vt_ref, *, scale, tiles_per_batch, front):
    tm = lat_ref.shape[0]
    lat = lat_ref[...]
    o1 = MLA_Q_RANK
    o2 = o1 + MLA_KV_RANK
    o3 = o2 + LANES
    qn = _rms(lat[:, :o1], qg_ref[...]).astype(BF16)
    kvn = _rms(lat[:, o1:o2], kg_ref[...]).astype(BF16)
    cc = cc_ref[...]
    ss = ss_ref[...]
    lane = lax.broadcasted_iota(jnp.int32, (tm, LANES), 1)
    row = (pl.program_id(0) % tiles_per_batch) * tm + lax.broadcasted_iota(jnp.int32, (tm, LANES), 0)
    bias_lane = lane == MLA_ROPE
    k_rope = jnp.where(bias_lane & (row < front), MASK_VALUE, lat[:, o2:o3] * cc + lat[:, o3:] * ss).astype(BF16)
    q_nope = jnp.dot(qn, wqn_ref[...], preferred_element_type=F32) * scale
    q_r = jnp.dot(qn, wqr_ref[...], preferred_element_type=F32)
    q_s = jnp.dot(qn, wqs_ref[...], preferred_element_type=F32)
    k_nope = jnp.dot(kvn, wkn_ref[...], preferred_element_type=F32)
    v_t = lax.dot_general(wvt_ref[...], kvn, (((1,), (1,)), ((), ())), preferred_element_type=F32)
    for h in range(MLA_HEADS):
        sl = slice(h * LANES, (h + 1) * LANES)
        q_ref[h, :, 0:LANES] = q_nope[:, sl].astype(BF16)
        q_rope = jnp.where(bias_lane, 1.0, (q_r[:, sl] * cc + q_s[:, sl] * ss) * scale)
        q_ref[h, :, LANES:2 * LANES] = q_rope.astype(BF16)
        k_ref[h, :, 0:LANES] = k_nope[:, sl].astype(BF16)
        k_ref[h, :, LANES:2 * LANES] = k_rope
        vt_ref[h, 0] = v_t[h * MLA_V:(h + 1) * MLA_V, :].astype(BF16)


def _attn_kernel(q_ref, k_ref, vt_ref, o_ref, sa_ref, sb_ref, m_ref, l_ref, acc_ref, *, tq):
    i = pl.program_id(2)
    q = q_ref[0]
    m_ref[...] = jnp.full(m_ref.shape, MASK_VALUE, F32)
    l_ref[...] = jnp.zeros(l_ref.shape, F32)
    acc_ref[...] = jnp.zeros(acc_ref.shape, F32)
    shift = CHUNK.bit_length() - 1

    def scores(j):
        k = k_ref[0, pl.ds(pl.multiple_of(j * tq, tq), tq), :]
        return lax.dot_general(k, q, (((1,), (1,)), ((), ())), preferred_element_type=F32)

    def absorb(s_ref, j, diagonal):
        s = s_ref[...]
        if diagonal:
            k_chunk = lax.broadcasted_iota(jnp.int32, (tq, tq), 0) >> shift
            q_chunk = lax.broadcasted_iota(jnp.int32, (tq, tq), 1) >> shift
            s = jnp.where(k_chunk <= q_chunk, s, MASK_VALUE)
        m_prev = m_ref[...]
        m_new = jnp.maximum(m_prev, jnp.max(s, axis=0, keepdims=True))
        alpha = jnp.exp2(m_prev - m_new)
        p = jnp.exp2(s - m_new)
        l_ref[...] = alpha * l_ref[...] + jnp.sum(p, axis=0, keepdims=True)
        acc_ref[...] = alpha * acc_ref[...] + jnp.dot(vt_ref[0, j], p.astype(BF16), preferred_element_type=F32)
        m_ref[...] = m_new

    sa_ref[...] = scores(0)

    def body(t, carry):
        j = 2 * t
        sb_ref[...] = scores(j + 1)
        absorb(sa_ref, j, False)
        sa_ref[...] = scores(j + 2)
        absorb(sb_ref, j + 1, False)
        return carry

    lax.fori_loop(0, i // 2, body, 0)

    @pl.when(i % 2 == 0)
    def _():
        absorb(sa_ref, i, True)

    @pl.when(i % 2 == 1)
    def _():
        sb_ref[...] = scores(i)
        absorb(sa_ref, i - 1, False)
        absorb(sb_ref, i, True)

    o_ref[...] = (acc_ref[...] / l_ref[...]).T.astype(o_ref.dtype)


def _rope_tables(lb, front):
    pos = (jnp.arange(lb) - front).astype(F32)
    inv = ROPE_BASE ** (-jnp.arange(0, MLA_ROPE, 2, dtype=F32) / MLA_ROPE)
    ang = pos[:, None] * inv[None, :]
    cos, sin = jnp.cos(ang), jnp.sin(ang)
    zeros = jnp.zeros((lb, LANES - MLA_ROPE), F32)
    cc = jnp.concatenate([cos, cos, zeros], axis=1)
    ss = jnp.concatenate([-sin, sin, zeros], axis=1)
    return cc, ss


def _pad_heads(w, width, take):
    k = w.shape[0]
    w = w.reshape(k, MLA_HEADS, width)[:, :, take]
    pad = LANES - w.shape[-1]
    if pad:
        w = jnp.concatenate([w, jnp.zeros((k, MLA_HEADS, pad), w.dtype)], axis=-1)
    return w.reshape(k, MLA_HEADS * LANES).astype(BF16)


def _mla_mixer(proj, nb, lb, front, q_norm, w_uq, kv_norm, w_ukv):
    r = proj.shape[0]
    half = MLA_ROPE // 2
    qw = MLA_NOPE + MLA_ROPE
    rope_cols = jnp.arange(MLA_NOPE, qw)
    swap_cols = jnp.concatenate([rope_cols[half:], rope_cols[:half]])
    wqn = _pad_heads(w_uq, qw, jnp.arange(MLA_NOPE))
    wqr = _pad_heads(w_uq, qw, rope_cols)
    wqs = _pad_heads(w_uq, qw, swap_cols)
    wkn = _pad_heads(w_ukv, MLA_NOPE + MLA_V, jnp.arange(MLA_NOPE))
    wvt = _pad_heads(w_ukv, MLA_NOPE + MLA_V, jnp.arange(MLA_NOPE, MLA_NOPE + MLA_V)).T
    cc, ss = _rope_tables(lb, front)
    scale = qw ** -0.5 * math.log2(math.e)

    tq = _pick(lb, (640, 128))
    nq = lb // tq
    hl = MLA_HEADS * LANES
    lat_w = MLA_Q_RANK + MLA_KV_RANK + 2 * LANES
    const = lambda i: (0, 0)
    q, k, vt = pl.pallas_call(
        functools.partial(_mla_prep_kernel, scale=scale, tiles_per_batch=nq, front=front),
        grid=(r // tq,),
        in_specs=[pl.BlockSpec((tq, lat_w), lambda i: (i, 1)),
                  pl.BlockSpec((tq, LANES), lambda i: (i % nq, 0)),
                  pl.BlockSpec((tq, LANES), lambda i: (i % nq, 0)),
                  pl.BlockSpec((1, MLA_Q_RANK), const),
                  pl.BlockSpec((1, MLA_KV_RANK), const),
                  pl.BlockSpec((MLA_Q_RANK, hl), const),
                  pl.BlockSpec((MLA_Q_RANK, hl), const),
                  pl.BlockSpec((MLA_Q_RANK, hl), const),
                  pl.BlockSpec((MLA_KV_RANK, hl), const),
                  pl.BlockSpec((hl, MLA_KV_RANK), const)],
        out_specs=[pl.BlockSpec((MLA_HEADS, tq, 2 * LANES), lambda i: (0, i, 0)),
                   pl.BlockSpec((MLA_HEADS, tq, 2 * LANES), lambda i: (0, i, 0)),
                   pl.BlockSpec((MLA_HEADS, 1, MLA_V, tq), lambda i: (0, i, 0, 0))],
        out_shape=[jax.ShapeDtypeStruct((MLA_HEADS, r, 2 * LANES), BF16),
                   jax.ShapeDtypeStruct((MLA_HEADS, r, 2 * LANES), BF16),
                   jax.ShapeDtypeStruct((MLA_HEADS, r // tq, MLA_V, tq), BF16)],
        compiler_params=_params("parallel"),
        name="mla_prep",
    )(proj, cc, ss, q_norm.reshape(1, -1), kv_norm.reshape(1, -1), wqn, wqr, wqs, wkn, wvt)

    return pl.pallas_call(
        functools.partial(_attn_kernel, tq=tq),
        grid=(nb, MLA_HEADS, nq),
        in_specs=[pl.BlockSpec((1, tq, 2 * LANES), lambda b, h, i: (h, b * nq + i, 0)),
                  pl.BlockSpec((1, lb, 2 * LANES), lambda b, h, i: (h, b, 0)),
                  pl.BlockSpec((1, nq, MLA_V, tq), lambda b, h, i: (h, b, 0, 0))],
        out_specs=pl.BlockSpec((tq, LANES), lambda b, h, i: (b * nq + i, h)),
        out_shape=jax.ShapeDtypeStruct((r, hl), BF16),
        scratch_shapes=[pltpu.VMEM((tq, tq), F32), pltpu.VMEM((tq, tq), F32),
                        pltpu.VMEM((1, tq), F32), pltpu.VMEM((1, tq), F32), pltpu.VMEM((MLA_V, tq), F32)],
        compiler_params=_params("parallel", "parallel", "arbitrary"),
        name="mla_attention",
    )(q, k, vt)


def _conv_kernel(x_ref, halo_ref, w_ref, b_ref, o_ref, *, tm, lb, front):
    w = w_ref[...]
    bias = b_ref[...]
    k = w.shape[0]

    def taps(x, halo):
        acc = bias + w[k - 1:k, :] * x
        head = bias + w[k - 1:k, :] * x[:SUBLANES]
        sub = lax.broadcasted_iota(jnp.int32, halo.shape, 0)
        for d in range(1, k):
            wd = w[k - 1 - d:k - d, :]
            acc += wd * pltpu.roll(x, d, 0)
            head += wd * jnp.where(sub < d, pltpu.roll(halo, d, 0), pltpu.roll(x[:SUBLANES], d, 0))
        return acc, head

    def silu(v):
        return (v * jax.nn.sigmoid(v)).astype(o_ref.dtype)

    acc, head = taps(x_ref[...].astype(F32), halo_ref[...].astype(F32))
    o_ref[...] = silu(acc)
    o_ref[0:SUBLANES, :] = silu(head)

    hb = -(-(front + k) // (2 * SUBLANES)) * (2 * SUBLANES)
    @pl.when((pl.program_id(0) * tm) % lb == 0)
    def _():
        xh = x_ref[0:hb, :].astype(F32)
        xh = jnp.where(lax.broadcasted_iota(jnp.int32, xh.shape, 0) < front, 0.0, xh)
        acc_h, head_h = taps(xh, jnp.zeros((SUBLANES, xh.shape[1]), F32))
        o_ref[0:hb, :] = silu(acc_h)
        o_ref[0:SUBLANES, :] = silu(head_h)


def _split3(x):
    hi = x.astype(BF16).astype(F32)
    mid = (x - hi).astype(BF16).astype(F32)
    lo = (x - hi - mid).astype(BF16).astype(F32)
    return hi, mid, lo


def _ssd_kernel(xs_ref, bm_ref, cm_ref, dtc_ref, dtr_ref, bc_ref, br_ref, ac_ref, ar_ref,
                d_ref, z_ref, ng_ref, ex_ref, o_ref, state_ref, xdt_ref, *, front):
    c = pl.program_id(1)
    q = xs_ref.shape[0]
    groups, nh = dtr_ref.shape[0], dtr_ref.shape[1]
    hp = xs_ref.shape[1] // groups
    ns = bm_ref.shape[1] // groups
    hd = hp // nh
    wide = dtc_ref.shape[1]

    @pl.when(c == 0)
    def _():
        state_ref[...] = jnp.zeros(state_ref.shape, F32)

    first_live = jnp.where(c == 0, front, 0)
    live_c = lax.broadcasted_iota(jnp.int32, (q, wide), 0) >= first_live
    live_r = lax.broadcasted_iota(jnp.int32, (nh, q), 1) >= first_live
    li = lax.broadcasted_iota(jnp.int32, (q, q), 0)
    si = lax.broadcasted_iota(jnp.int32, (q, q), 1)
    lower = li >= si
    upper16 = (li <= si).astype(BF16)
    lane = lax.broadcasted_iota(jnp.int32, (q, 2 * hd), 1)

    def terms(x, axis):
        return jnp.concatenate(_split3(x), axis=axis).astype(BF16)

    dt_c = jnp.where(live_c, jax.nn.softplus(dtc_ref[...] + bc_ref[0]), 0.0)
    dt_terms = terms(dt_c, 1)
    cs3 = jnp.dot(lower.astype(BF16), terms(dt_c * ac_ref[0], 1), preferred_element_type=F32)
    cs_terms = terms(cs3[:, :wide] + cs3[:, wide:2 * wide] + cs3[:, 2 * wide:], 1)

    for g in range(groups):
        cols = slice(g * hp, (g + 1) * hp)
        expand3 = ex_ref[g]
        dt_full = jnp.dot(dt_terms, expand3, preferred_element_type=F32)
        cs_full = jnp.dot(cs_terms, expand3, preferred_element_type=F32)
        dt_r = jnp.where(live_r, jax.nn.softplus(dtr_ref[g] + br_ref[g]), 0.0)
        csr3 = jnp.dot(terms(dt_r * ar_ref[g], 0), upper16, preferred_element_type=F32)
        cs_r = csr3[:nh] + csr3[nh:2 * nh] + csr3[2 * nh:]
        last = cs_full[q - 1:q, :]

        xdt_ref[g] = xs_ref[:, cols].astype(F32) * dt_full
        bm = bm_ref[:, g * ns:(g + 1) * ns]
        cm = cm_ref[:, g * ns:(g + 1) * ns]
        cb = lax.dot_general(cm, bm, (((1,), (1,)), ((), ())), preferred_element_type=F32)

        prev = state_ref[g]
        y = jnp.dot(cm, prev.astype(BF16), preferred_element_type=F32) * jnp.exp2(cs_full)
        local = lax.dot_general(bm, (xdt_ref[g] * jnp.exp2(last - cs_full)).astype(BF16),
                                (((0,), (0,)), ((), ())), preferred_element_type=F32)
        state_ref[g] = prev * jnp.exp2(last) + local

        diag = []
        for pr in range(nh // 2):
            ms = []
            for h in (2 * pr, 2 * pr + 1):
                seg = cs_full[:, h * hd:h * hd + 1] - cs_r[h:h + 1, :]
                ms.append((cb * jnp.exp2(jnp.where(lower, seg, -jnp.inf))).astype(BF16))
            xp = xdt_ref[g, :, pr * 2 * hd:(pr + 1) * 2 * hd]
            rhs = jnp.concatenate([jnp.where(lane < hd, xp, 0.0), jnp.where(lane >= hd, xp, 0.0)], axis=0)
            diag.append(jnp.dot(jnp.concatenate(ms, axis=1), rhs.astype(BF16), preferred_element_type=F32))
        y = y + jnp.concatenate(diag, axis=1) + d_ref[g] * xs_ref[:, cols].astype(F32)

        z = z_ref[:, cols].astype(F32)
        y = y * (z * jax.nn.sigmoid(z))
        y = y * lax.rsqrt(jnp.mean(y * y, axis=-1, keepdims=True) + RMS_EPS) * ng_ref[:, cols]
        o_ref[:, cols] = y.astype(o_ref.dtype)


def _mamba2_mixer(h16, nb, lb, front, w_in, conv_w, conv_b, dt_bias, a_log, d, norm_g):
    r = h16.shape[0]
    inner = norm_g.shape[0]
    gn = SSD_GROUPS * SSD_STATE
    cdim = inner + 2 * gn
    tm = _pick(r, (1664, 1280, 640, 128))
    nheads = SSD_GROUPS * SSD_HPG
    tn = _pick(math.gcd(inner, cdim), (1024, 512, 128))
    z = _matmul_ws(h16, w_in, BF16, col0=0, ncols=inner, tm=tm, tn=tn, name="l1_in_z")
    xbc = _matmul_ws(h16, w_in, BF16, col0=inner, ncols=cdim, tm=tm, tn=tn, name="l1_in_xbc")
    w_dt = jnp.concatenate([w_in[:, inner + cdim:], jnp.zeros((w_in.shape[0], LANES - nheads), F32)], axis=1)
    dt_raw = _matmul(h16, w_dt.astype(BF16), F32, tm=tm, tn=LANES, name="l1_in_dt")

    tcv = _pick(cdim, (512, 256, 128))
    blocks = tm // SUBLANES
    assert lb % tm == 0
    act = pl.pallas_call(
        functools.partial(_conv_kernel, tm=tm, lb=lb, front=front),
        grid=(r // tm, cdim // tcv),
        in_specs=[pl.BlockSpec((tm, tcv), lambda i, j: (i, j)),
                  pl.BlockSpec((SUBLANES, tcv), lambda i, j: (jnp.maximum(i * blocks - 1, 0), j)),
                  pl.BlockSpec((SSD_CONV, tcv), lambda i, j: (0, j)),
                  pl.BlockSpec((1, tcv), lambda i, j: (0, j))],
        out_specs=pl.BlockSpec((tm, tcv), lambda i, j: (i, j)),
        out_shape=jax.ShapeDtypeStruct((r, cdim), BF16),
        compiler_params=_params("parallel", "parallel"),
        name="ssd_conv",
    )(xbc, xbc, conv_w, conv_b.reshape(1, cdim))

    gsz = inner // SSD_GROUPS
    nc = lb // SSD_BLOCK
    dt_r = dt_raw[:, :nheads].T.reshape(SSD_GROUPS, SSD_HPG, r)
    lane_pad = jnp.zeros((LANES - nheads,), F32)
    bias = dt_bias.reshape(SSD_GROUPS, SSD_HPG)
    bias_c = jnp.concatenate([dt_bias, lane_pad]).reshape(1, 1, LANES)
    a = -jnp.exp(a_log) * math.log2(math.e)
    a_c = jnp.concatenate([a, lane_pad]).reshape(1, 1, LANES)
    a = a.reshape(SSD_GROUPS, SSD_HPG)
    d_full = jnp.repeat(d, SSD_HEAD_DIM).reshape(SSD_GROUPS, 1, gsz)
    head_of_lane = (jnp.arange(gsz)[None, None, :] // SSD_HEAD_DIM
                    + SSD_HPG * jnp.arange(SSD_GROUPS)[:, None, None])
    expand = (jnp.arange(LANES)[None, :, None] == head_of_lane).astype(BF16)
    expand3 = jnp.concatenate([expand] * 3, axis=1)
    row = lambda b, c: b * nc + c
    whole3 = lambda b, c: (0, 0, 0)
    ng_ = SSD_GROUPS
    return pl.pallas_call(
        functools.partial(_ssd_kernel, front=front),
        grid=(nb, nc),
        in_specs=[pl.BlockSpec((SSD_BLOCK, inner), lambda b, c: (row(b, c), 0)),
                  pl.BlockSpec((SSD_BLOCK, gn), lambda b, c: (row(b, c), inner // gn)),
                  pl.BlockSpec((SSD_BLOCK, gn), lambda b, c: (row(b, c), inner // gn + 1)),
                  pl.BlockSpec((SSD_BLOCK, LANES), lambda b, c: (row(b, c), 0)),
                  pl.BlockSpec((ng_, SSD_HPG, SSD_BLOCK), lambda b, c: (0, 0, row(b, c))),
                  pl.BlockSpec((1, 1, LANES), whole3),
                  pl.BlockSpec((ng_, SSD_HPG, 1), whole3),
                  pl.BlockSpec((1, 1, LANES), whole3),
                  pl.BlockSpec((ng_, SSD_HPG, 1), whole3),
                  pl.BlockSpec((ng_, 1, gsz), whole3),
                  pl.BlockSpec((SSD_BLOCK, inner), lambda b, c: (row(b, c), 0)),
                  pl.BlockSpec((1, inner), lambda b, c: (0, 0)),
                  pl.BlockSpec((ng_, 3 * LANES, gsz), whole3)],
        out_specs=pl.BlockSpec((SSD_BLOCK, inner), lambda b, c: (row(b, c), 0)),
        out_shape=jax.ShapeDtypeStruct((r, inner), BF16),
        scratch_shapes=[pltpu.VMEM((ng_, SSD_STATE, gsz), F32), pltpu.VMEM((ng_, SSD_BLOCK, gsz), F32)],
        compiler_params=_params("parallel", "arbitrary"),
        name="ssd_scan",
    )(act, act, act, dt_raw, dt_r,
      bias_c, bias[:, :, None], a_c, a[:, :, None],
      d_full, z, norm_g.reshape(1, inner), expand3)


def kernel(x, meta_tokens, l0_w_in, l0_s5_log_dt, l0_s5_a_re, l0_s5_a_im, l0_s5_b_re, l0_s5_b_im, l0_s5_c_re, l0_s5_c_im, l0_s5_d, l0_s5_w_glu, l0_mla_q_norm, l0_mla_w_uq, l0_mla_kv_norm, l0_mla_w_ukv, l0_w_out, l0_ln1_g, l0_ln1_b, l0_ffn_w_gate, l0_ffn_w_up, l0_ffn_w_down, l0_ln2_g, l0_ln2_b, l1_w_in, l1_conv_w, l1_conv_b, l1_dt_bias, l1_a_log, l1_d, l1_norm_g, l1_w_out, l1_ln1_g, l1_ln1_b, l1_ffn_w_gate, l1_ffn_w_up, l1_ffn_w_down, l1_ln2_g, l1_ln2_b):
    nb, seq, dm = x.shape
    n_meta = meta_tokens.shape[0]
    front = (-n_meta) % SSD_BLOCK
    lb = front + n_meta + seq
    r = nb * lb
    meta = jnp.broadcast_to(meta_tokens[None].astype(x.dtype), (nb, n_meta, dm))
    h32 = jnp.concatenate([jnp.zeros((nb, front, dm), x.dtype), meta, x], axis=1).reshape(r, dm)
    h16 = h32.astype(BF16)

    s5w = l0_s5_d.shape[0]
    o3 = s5w + MLA_Q_RANK + MLA_KV_RANK
    kr = l0_w_in[:, o3:]
    half = MLA_ROPE // 2
    zpad = jnp.zeros((dm, LANES - MLA_ROPE), l0_w_in.dtype)
    w0 = jnp.concatenate([l0_w_in[:, :o3], kr, zpad, kr[:, half:], kr[:, :half], zpad], axis=1).astype(BF16)
    tm = _pick(r, (1664, 1280, 640, 128))
    proj = _matmul(h16, w0, F32, tm=tm, tn=_pick(w0.shape[1], (1024, 512, 128)), name="l0_in")
    a_out = _s5_mixer(proj, nb, lb, front, l0_s5_log_dt, l0_s5_a_re, l0_s5_a_im, l0_s5_b_re, l0_s5_b_im,
                      l0_s5_c_re, l0_s5_c_im, l0_s5_d, l0_s5_w_glu)
    b_out = _mla_mixer(proj, nb, lb, front, l0_mla_q_norm, l0_mla_w_uq, l0_mla_kv_norm, l0_mla_w_ukv)
    tml = _pick(r, (640, 128))
    h32, h16 = _mm_res_ln((a_out, b_out), l0_w_out.astype(BF16), h32, l0_ln1_g, l0_ln1_b, tm=tml,
                          tk=a_out.shape[1], name="l0_out_ln")
    h32, h16 = _ffn_block(h32, h16, l0_ffn_w_gate, l0_ffn_w_up, l0_ffn_w_down, l0_ln2_g, l0_ln2_b, "l0")

    y = _mamba2_mixer(h16, nb, lb, front, l1_w_in, l1_conv_w, l1_conv_b, l1_dt_bias, l1_a_log, l1_d, l1_norm_g)
    h32, h16 = _mm_res_ln(y, l1_w_out.astype(BF16), h32, l1_ln1_g, l1_ln1_b, tm=tml, tk=1024, name="l1_out_ln")
    h32, h16 = _ffn_block(h32, h16, l1_ffn_w_gate, l1_ffn_w_up, l1_ffn_w_down, l1_ln2_g, l1_ln2_b, "l1")
    return h32.reshape(nb, lb, dm)[:, front + n_meta:, :]
```

```python
import functools
import math

import jax
import jax.numpy as jnp
from jax import lax
from jax.experimental import pallas as pl
from jax.experimental.pallas import tpu as pltpu

F32 = jnp.float32
BF16 = jnp.bfloat16

DEPTH = 2
CHUNK = 64
N_META = 16
DN_ALPHA = (2 * DEPTH) ** 0.25
LN_EPS = 1e-5
RMS_EPS = 1e-6
S5_GROUP = 16
S5_STATE = 64
S5_Q = 16
S5_LANE_GROUPS = 128 // S5_GROUP
MLA_HEADS = 8
MLA_NOPE = 128
MLA_ROPE = 64
MLA_V = 128
MLA_Q_RANK = 512
MLA_KV_RANK = 256
ROPE_BASE = 10000.0
SSD_HEAD_DIM = 64
SSD_GROUPS = 8
SSD_HPG = 8
SSD_STATE = 128
SSD_CONV = 4
SSD_BLOCK = 128

LANES = 128
SUBLANES = 8
VMEM_LIMIT = 56 * 1024 * 1024
MASK_VALUE = -1e30
HI = lax.Precision.HIGHEST


def _pick(n, candidates):
    for c in candidates:
        if n % c == 0:
            return c
    raise ValueError(f"no tile for {n} in {candidates}")


def _params(*sem):
    return pltpu.CompilerParams(dimension_semantics=sem, vmem_limit_bytes=VMEM_LIMIT)


def _mm_kernel(x_ref, w_ref, o_ref):
    o_ref[...] = jnp.dot(x_ref[...].astype(BF16), w_ref[...], preferred_element_type=F32).astype(o_ref.dtype)


def _matmul(x, w, out_dtype, *, tm, tn, name):
    m, k = x.shape
    n = w.shape[1]
    return pl.pallas_call(
        _mm_kernel,
        grid=(m // tm, n // tn),
        in_specs=[pl.BlockSpec((tm, k), lambda i, j: (i, 0)),
                  pl.BlockSpec((k, tn), lambda i, j: (0, j))],
        out_specs=pl.BlockSpec((tm, tn), lambda i, j: (i, j)),
        out_shape=jax.ShapeDtypeStruct((m, n), out_dtype),
        compiler_params=_params("parallel", "arbitrary"),
        name=name,
    )(x, w)


def _mm_ws_kernel(x_ref, w_ref, o_ref, wb_ref):
    @pl.when(pl.program_id(1) == 0)
    def _():
        wb_ref[...] = w_ref[...].astype(BF16)

    o_ref[...] = jnp.dot(x_ref[...], wb_ref[...], preferred_element_type=F32).astype(o_ref.dtype)


def _matmul_ws(x, w, out_dtype, *, col0, ncols, tm, tn, name):
    m, k = x.shape
    assert col0 % tn == 0 and ncols % tn == 0
    c0 = col0 // tn
    return pl.pallas_call(
        _mm_ws_kernel,
        grid=(ncols // tn, m // tm),
        in_specs=[pl.BlockSpec((tm, k), lambda j, i: (i, 0)),
                  pl.BlockSpec((k, tn), lambda j, i: (0, c0 + j))],
        out_specs=pl.BlockSpec((tm, tn), lambda j, i: (i, j)),
        out_shape=jax.ShapeDtypeStruct((m, ncols), out_dtype),
        scratch_shapes=[pltpu.VMEM((k, tn), BF16)],
        compiler_params=_params("parallel", "arbitrary"),
        name=name,
    )(x, w)


def _mm_res_ln_kernel(*refs, nk, parts):
    x_refs = refs[:parts]
    w_ref, res_ref, g_ref, b_ref, o32_ref, o16_ref, acc_ref = refs[parts:]
    k = pl.program_id(1)

    @pl.when(k == 0)
    def _():
        acc_ref[...] = jnp.zeros(acc_ref.shape, F32)

    if parts == 1:
        acc_ref[...] += jnp.dot(x_refs[0][...], w_ref[...], preferred_element_type=F32)
    else:
        for p in range(parts):
            @pl.when(k == p)
            def _(p=p):
                acc_ref[...] += jnp.dot(x_refs[p][...], w_ref[...], preferred_element_type=F32)

    @pl.when(k == nk - 1)
    def _():
        y = DN_ALPHA * res_ref[...] + acc_ref[...]
        mu = jnp.mean(y, axis=-1, keepdims=True)
        yc = y - mu
        var = jnp.mean(yc * yc, axis=-1, keepdims=True)
        out = yc * lax.rsqrt(var + LN_EPS) * g_ref[...] + b_ref[...]
        o32_ref[...] = out
        o16_ref[...] = out.astype(BF16)


def _mm_res_ln(x, w, res, g, b, *, tm, tk, name):
    xs = x if isinstance(x, tuple) else (x,)
    m = xs[0].shape[0]
    kk, d = w.shape
    nk = kk // tk
    if len(xs) > 1:
        assert len(xs) == nk and all(p.shape == (m, tk) for p in xs)
        x_specs = [pl.BlockSpec((tm, tk), lambda i, k: (i, 0)) for _ in xs]
    else:
        x_specs = [pl.BlockSpec((tm, tk), lambda i, k: (i, k))]
    return pl.pallas_call(
        functools.partial(_mm_res_ln_kernel, nk=nk, parts=len(xs)),
        grid=(m // tm, nk),
        in_specs=x_specs + [
                  pl.BlockSpec((tk, d), lambda i, k: (k, 0)),
                  pl.BlockSpec((tm, d), lambda i, k: (i, 0)),
                  pl.BlockSpec((1, d), lambda i, k: (0, 0)),
                  pl.BlockSpec((1, d), lambda i, k: (0, 0))],
        out_specs=[pl.BlockSpec((tm, d), lambda i, k: (i, 0)),
                   pl.BlockSpec((tm, d), lambda i, k: (i, 0))],
        out_shape=[jax.ShapeDtypeStruct((m, d), F32), jax.ShapeDtypeStruct((m, d), BF16)],
        scratch_shapes=[pltpu.VMEM((tm, d), F32)],
        compiler_params=_params("parallel", "arbitrary"),
        name=name,
    )(*xs, w, res, g.reshape(1, d), b.reshape(1, d))


def _ffn_up_kernel(x_ref, wg_ref, wu_ref, o_ref, wgb_ref, wub_ref):
    @pl.when(pl.program_id(1) == 0)
    def _():
        wgb_ref[...] = wg_ref[...].astype(BF16)
        wub_ref[...] = wu_ref[...].astype(BF16)

    x = x_ref[...]
    g = jnp.dot(x, wgb_ref[...], preferred_element_type=F32)
    u = jnp.dot(x, wub_ref[...], preferred_element_type=F32)
    o_ref[...] = (g * jax.nn.sigmoid(g) * u).astype(o_ref.dtype)


def _ffn_up(x, wg, wu, *, tm, tf, name):
    m, d = x.shape
    f = wg.shape[1]
    return pl.pallas_call(
        _ffn_up_kernel,
        grid=(f // tf, m // tm),
        in_specs=[pl.BlockSpec((tm, d), lambda j, i: (i, 0)),
                  pl.BlockSpec((d, tf), lambda j, i: (0, j)),
                  pl.BlockSpec((d, tf), lambda j, i: (0, j))],
        out_specs=pl.BlockSpec((tm, tf), lambda j, i: (i, j)),
        out_shape=jax.ShapeDtypeStruct((m, f), BF16),
        scratch_shapes=[pltpu.VMEM((d, tf), BF16), pltpu.VMEM((d, tf), BF16)],
        compiler_params=_params("parallel", "arbitrary"),
        name=name,
    )(x, wg, wu)


def _ffn_block(h32, h16, w_gate, w_up, w_down, g, b, tag):
    m = h16.shape[0]
    f = w_gate.shape[1]
    hid = _ffn_up(h16, w_gate, w_up,
                  tm=_pick(m, (1664, 1280, 640, 128)), tf=_pick(f, (512, 256, 128)),
                  name=f"{tag}_ffn_up")
    return _mm_res_ln(hid, w_down.astype(BF16), h32, g, b,
                      tm=_pick(m, (640, 128)), tk=_pick(f, (1408, 512, 256, 128)), name=f"{tag}_ffn_down_ln")


def _s5_tables(log_dt, a_re, a_im, b_re, b_im, c_re, c_im):
    g_, p_ = a_re.shape
    j_ = b_re.shape[-1]
    q = S5_Q
    dt = jnp.exp(log_dt)[:, None]
    lr = dt * a_re
    li = dt * a_im
    ks = jnp.arange(q + 1, dtype=F32)[:, None, None]
    mag = jnp.exp(ks * lr)
    pw_re = mag * jnp.cos(ks * li)
    pw_im = mag * jnp.sin(ks * li)
    ab_re, ab_im = pw_re[1], pw_im[1]
    den = a_re * a_re + a_im * a_im
    nr = ab_re - 1.0
    f_re = (nr * a_re + ab_im * a_im) / den
    f_im = (ab_im * a_re - nr * a_im) / den
    bb_re = f_re[..., None] * b_re - f_im[..., None] * b_im
    bb_im = f_re[..., None] * b_im + f_im[..., None] * b_re
    cp_re = c_re[None] * pw_re[:, :, None, :] - c_im[None] * pw_im[:, :, None, :]
    cp_im = c_re[None] * pw_im[:, :, None, :] + c_im[None] * pw_re[:, :, None, :]
    kern = jnp.sum(cp_re[:q, :, :, :, None] * bb_re[None, :, None, :, :]
                   - cp_im[:q, :, :, :, None] * bb_im[None, :, None, :, :], axis=3)
    kr = (q - 1 - jnp.arange(q)).astype(F32)[:, None, None]
    rev_mag = jnp.exp(kr * lr)
    rev_re = (rev_mag * jnp.cos(kr * li))[:, :, :, None]
    rev_im = (rev_mag * jnp.sin(kr * li))[:, :, :, None]
    st_re = (rev_re * bb_re[None] - rev_im * bb_im[None]).transpose(1, 0, 3, 2)
    st_im = (rev_re * bb_im[None] + rev_im * bb_re[None]).transpose(1, 0, 3, 2)
    wo_re = cp_re[1:].transpose(1, 3, 0, 2)
    wo_im = (-cp_im[1:]).transpose(1, 3, 0, 2)

    gl = S5_LANE_GROUPS
    nblk = g_ // gl

    def block_diag(w, rows_per_group, cols_per_group):
        w = jnp.tile(w, (1,) * (w.ndim - 1) + (gl,))
        row_g = jnp.arange(gl * rows_per_group)[:, None] // rows_per_group
        col_g = jnp.arange(gl * cols_per_group)[None, :] // cols_per_group
        return jnp.where(row_g == col_g, w, 0.0)

    kc = kern.reshape(q, nblk, gl, j_, j_).transpose(1, 0, 2, 4, 3).reshape(nblk, q, gl * j_, j_)
    lag_b = block_diag(kc, j_, j_).astype(BF16)

    def st_blocks(st):
        w = st.reshape(nblk, gl, q, j_, p_).transpose(0, 2, 1, 3, 4).reshape(nblk, q, gl * j_, p_)
        return block_diag(w, j_, p_)

    st_b = jnp.concatenate([st_blocks(st_re), st_blocks(st_im)], axis=3)
    st_b = st_b.reshape(nblk, q * gl * j_, 2 * gl * p_).astype(BF16)

    def wo_blocks(wo):
        w = block_diag(wo.reshape(nblk, gl * p_, q, j_).transpose(0, 2, 1, 3), p_, j_)
        return w.transpose(0, 2, 1, 3).reshape(nblk, gl * p_, q * gl * j_).astype(BF16)

    aq_re = pw_re[q].reshape(1, g_ * p_)
    aq_im = pw_im[q].reshape(1, g_ * p_)
    return lag_b, st_b, wo_blocks(wo_re), wo_blocks(wo_im), aq_re, aq_im


def _s5_rows(u_ref, front_rows):
    rows = u_ref.shape[0] // S5_Q
    live = lax.broadcasted_iota(jnp.int32, (rows, u_ref.shape[1]), 0) >= front_rows
    pieces = [jnp.where(live, u_ref[pl.ds(i, rows, stride=S5_Q), :], 0.0) for i in range(S5_Q)]
    return pieces, jnp.concatenate([p.astype(BF16) for p in pieces], axis=1)


def _s5_local_kernel(u_ref, w_ref, zr_ref, zi_ref, *, front_rows):
    _, x = _s5_rows(u_ref, front_rows)
    z = jnp.dot(x, w_ref[0], preferred_element_type=F32)
    half = z.shape[1] // 2
    zr_ref[...] = z[:, :half]
    zi_ref[...] = z[:, half:]


def _s5_scan_kernel(zr_ref, zi_ref, ar_ref, ai_ref, sr_ref, si_ref, *, nb, nc):
    ar = ar_ref[...]
    ai = ai_ref[...]

    def body(c, carry):
        new = []
        for b in range(nb):
            s_re, s_im = carry[2 * b], carry[2 * b + 1]
            row = b * nc + c
            sr_ref[pl.ds(row, 1), :] = s_re
            si_ref[pl.ds(row, 1), :] = s_im
            z_re = zr_ref[pl.ds(row, 1), :]
            z_im = zi_ref[pl.ds(row, 1), :]
            new.append(ar * s_re - ai * s_im + z_re)
            new.append(ar * s_im + ai * s_re + z_im)
        return tuple(new)

    zero = jnp.zeros(ar.shape, F32)
    lax.fori_loop(0, nc, body, (zero,) * (2 * nb))


def _s5_out_kernel(u_ref, lag_ref, sr_ref, si_ref, wr_ref, wi_ref, d_ref, y_ref, toe_ref, *, front_rows):
    @pl.when(pl.program_id(1) == 0)
    def _():
        zero = jnp.zeros((LANES, LANES), toe_ref.dtype)
        for i in range(S5_Q):
            for j in range(S5_Q):
                toe_ref[i * LANES:(i + 1) * LANES, j * LANES:(j + 1) * LANES] = lag_ref[0, j - i] if j >= i else zero

    pieces, x = _s5_rows(u_ref, front_rows)
    rows = x.shape[0]
    y = jnp.dot(x, toe_ref[...], preferred_element_type=F32)
    y += jnp.dot(sr_ref[...].astype(BF16), wr_ref[0], preferred_element_type=F32)
    y += jnp.dot(si_ref[...].astype(BF16), wi_ref[0], preferred_element_type=F32)
    d = d_ref[...]
    for j in range(S5_Q):
        y_ref[pl.ds(j, rows, stride=S5_Q), :] = y[:, j * LANES:(j + 1) * LANES] + d * pieces[j]


def _glu_kernel(y_ref, w_ref, o_ref):
    g = jax.nn.gelu(y_ref[...])
    gate = jax.nn.sigmoid(jnp.dot(g.astype(BF16), w_ref[...], preferred_element_type=F32))
    o_ref[...] = (g * gate).astype(o_ref.dtype)


def _s5_mixer(proj, nb, lb, front, log_dt, a_re, a_im, b_re, b_im, c_re, c_im, d, w_glu):
    r = proj.shape[0]
    g_, p_ = a_re.shape
    w_ = g_ * S5_GROUP
    q = S5_Q
    m = r // q
    nc = lb // q
    nblk = w_ // LANES
    sw = S5_LANE_GROUPS * p_
    kw = q * LANES
    front_rows = front // q
    lag_b, st_b, wo_re_b, wo_im_b, aq_re, aq_im = _s5_tables(log_dt, a_re, a_im, b_re, b_im, c_re, c_im)

    z_re, z_im = pl.pallas_call(
        functools.partial(_s5_local_kernel, front_rows=front_rows),
        grid=(nblk, nb),
        in_specs=[pl.BlockSpec((lb, LANES), lambda c, b: (b, c)),
                  pl.BlockSpec((1, kw, 2 * sw), lambda c, b: (c, 0, 0))],
        out_specs=[pl.BlockSpec((nc, sw), lambda c, b: (b, c)),
                   pl.BlockSpec((nc, sw), lambda c, b: (b, c))],
        out_shape=[jax.ShapeDtypeStruct((m, g_ * p_), F32)] * 2,
        compiler_params=_params("parallel", "parallel"),
        name="s5_local",
    )(proj, st_b)

    cw = _pick(g_ * p_, (512, 256, 128))
    s_re, s_im = pl.pallas_call(
        functools.partial(_s5_scan_kernel, nb=nb, nc=nc),
        grid=(g_ * p_ // cw,),
        in_specs=[pl.BlockSpec((m, cw), lambda i: (0, i)),
                  pl.BlockSpec((m, cw), lambda i: (0, i)),
                  pl.BlockSpec((1, cw), lambda i: (0, i)),
                  pl.BlockSpec((1, cw), lambda i: (0, i))],
        out_specs=[pl.BlockSpec((m, cw), lambda i: (0, i)),
                   pl.BlockSpec((m, cw), lambda i: (0, i))],
        out_shape=[jax.ShapeDtypeStruct((m, g_ * p_), F32)] * 2,
        compiler_params=_params("parallel"),
        name="s5_scan",
    )(z_re, z_im, aq_re, aq_im)

    y = pl.pallas_call(
        functools.partial(_s5_out_kernel, front_rows=front_rows),
        grid=(nblk, nb),
        in_specs=[pl.BlockSpec((lb, LANES), lambda c, b: (b, c)),
                  pl.BlockSpec((1, q, LANES, LANES), lambda c, b: (c, 0, 0, 0)),
                  pl.BlockSpec((nc, sw), lambda c, b: (b, c)),
                  pl.BlockSpec((nc, sw), lambda c, b: (b, c)),
                  pl.BlockSpec((1, sw, kw), lambda c, b: (c, 0, 0)),
                  pl.BlockSpec((1, sw, kw), lambda c, b: (c, 0, 0)),
                  pl.BlockSpec((1, LANES), lambda c, b: (0, c))],
        out_specs=pl.BlockSpec((lb, LANES), lambda c, b: (b, c)),
        out_shape=jax.ShapeDtypeStruct((r, w_), F32),
        scratch_shapes=[pltpu.VMEM((kw, kw), BF16)],
        compiler_params=_params("parallel", "arbitrary"),
        name="s5_out",
    )(proj, lag_b, s_re, s_im, wo_re_b, wo_im_b, d.reshape(1, w_))

    tm = _pick(r, (832, 640, 128))
    return pl.pallas_call(
        _glu_kernel,
        grid=(r // tm,),
        in_specs=[pl.BlockSpec((tm, w_), lambda i: (i, 0)),
                  pl.BlockSpec((w_, w_), lambda i: (0, 0))],
        out_specs=pl.BlockSpec((tm, w_), lambda i: (i, 0)),
        out_shape=jax.ShapeDtypeStruct((r, w_), BF16),
        compiler_params=_params("parallel"),
        name="s5_glu",
    )(y, w_glu.astype(BF16))


def _rms(x, g):
    return x * lax.rsqrt(jnp.mean(x * x, axis=-1, keepdims=True) + RMS_EPS) * g


def _mla_prep_kernel(lat_ref, cc_ref, ss_ref, qg_ref, kg_ref, wqn_ref, wqr_ref, wqs_ref, wkn_ref, wvt_ref,
                     q_ref, k_ref, vt_ref, *, scale, tiles_per_batch, front):
    tm = lat_ref.shape[0]
    lat = lat_ref[...]
    o1 = MLA_Q_RANK
    o2 = o1 + MLA_KV_RANK
    o3 = o2 + LANES
    qn = _rms(lat[:, :o1], qg_ref[...]).astype(BF16)
    kvn = _rms(lat[:, o1:o2], kg_ref[...]).astype(BF16)
    cc = cc_ref[...]
    ss = ss_ref[...]
    lane = lax.broadcasted_iota(jnp.int32, (tm, LANES), 1)
    row = (pl.program_id(0) % tiles_per_batch) * tm + lax.broadcasted_iota(jnp.int32, (tm, LANES), 0)
    bias_lane = lane == MLA_ROPE
    k_rope = jnp.where(bias_lane & (row < front), MASK_VALUE, lat[:, o2:o3] * cc + lat[:, o3:] * ss).astype(BF16)
    q_nope = jnp.dot(qn, wqn_ref[...], preferred_element_type=F32) * scale
    q_r = jnp.dot(qn, wqr_ref[...], preferred_element_type=F32)
    q_s = jnp.dot(qn, wqs_ref[...], preferred_element_type=F32)
    k_nope = jnp.dot(kvn, wkn_ref[...], preferred_element_type=F32)
    v_t = lax.dot_general(wvt_ref[...], kvn, (((1,), (1,)), ((), ())), preferred_element_type=F32)
    for h in range(MLA_HEADS):
        sl = slice(h * LANES, (h + 1) * LANES)
        q_ref[h, :, 0:LANES] = q_nope[:, sl].astype(BF16)
        q_rope = jnp.where(bias_lane, 1.0, (q_r[:, sl] * cc + q_s[:, sl] * ss) * scale)
        q_ref[h, :, LANES:2 * LANES] = q_rope.astype(BF16)
        k_ref[h, :, 0:LANES] = k_nope[:, sl].astype(BF16)
        k_ref[h, :, LANES:2 * LANES] = k_rope
        vt_ref[h, 0] = v_t[h * MLA_V:(h + 1) * MLA_V, :].astype(BF16)


def _attn_kernel(q_ref, k_ref, vt_ref, o_ref, sa_ref, sb_ref, m_ref, l_ref, acc_ref, *, tq):
    i = pl.program_id(2)
    heads = q_ref.shape[0]
    m_ref[...] = jnp.full(m_ref.shape, MASK_VALUE, F32)
    l_ref[...] = jnp.zeros(l_ref.shape, F32)
    acc_ref[...] = jnp.zeros(acc_ref.shape, F32)
    shift = CHUNK.bit_length() - 1

    def scores(s_ref, j):
        for h in range(heads):
            k = k_ref[h, pl.ds(pl.multiple_of(j * tq, tq), tq), :]
            s_ref[h] = lax.dot_general(k, q_ref[h], (((1,), (1,)), ((), ())), preferred_element_type=F32)

    def absorb(s_ref, j, diagonal):
        for h in range(heads):
            s = s_ref[h]
            if diagonal:
                k_chunk = lax.broadcasted_iota(jnp.int32, (tq, tq), 0) >> shift
                q_chunk = lax.broadcasted_iota(jnp.int32, (tq, tq), 1) >> shift
                s = jnp.where(k_chunk <= q_chunk, s, MASK_VALUE)
            m_prev = m_ref[h]
            m_new = jnp.maximum(m_prev, jnp.max(s, axis=0, keepdims=True))
            alpha = jnp.exp2(m_prev - m_new)
            p = jnp.exp2(s - m_new)
            l_ref[h] = alpha * l_ref[h] + jnp.sum(p, axis=0, keepdims=True)
            acc_ref[h] = alpha * acc_ref[h] + jnp.dot(vt_ref[h, j], p.astype(BF16), preferred_element_type=F32)
            m_ref[h] = m_new

    scores(sa_ref, 0)

    def body(t, carry):
        j = 2 * t
        scores(sb_ref, j + 1)
        absorb(sa_ref, j, False)
        scores(sa_ref, j + 2)
        absorb(sb_ref, j + 1, False)
        return carry

    lax.fori_loop(0, i // 2, body, 0)

    @pl.when(i % 2 == 0)
    def _():
        absorb(sa_ref, i, True)

    @pl.when(i % 2 == 1)
    def _():
        scores(sb_ref, i)
        absorb(sa_ref, i - 1, False)
        absorb(sb_ref, i, True)

    for h in range(heads):
        o_ref[:, h * MLA_V:(h + 1) * MLA_V] = (acc_ref[h] / l_ref[h]).T.astype(o_ref.dtype)


def _rope_tables(lb, front):
    pos = (jnp.arange(lb) - front).astype(F32)
    inv = ROPE_BASE ** (-jnp.arange(0, MLA_ROPE, 2, dtype=F32) / MLA_ROPE)
    ang = pos[:, None] * inv[None, :]
    cos, sin = jnp.cos(ang), jnp.sin(ang)
    zeros = jnp.zeros((lb, LANES - MLA_ROPE), F32)
    cc = jnp.concatenate([cos, cos, zeros], axis=1)
    ss = jnp.concatenate([-sin, sin, zeros], axis=1)
    return cc, ss


def _pad_heads(w, width, take):
    k = w.shape[0]
    w = w.reshape(k, MLA_HEADS, width)[:, :, take]
    pad = LANES - w.shape[-1]
    if pad:
        w = jnp.concatenate([w, jnp.zeros((k, MLA_HEADS, pad), w.dtype)], axis=-1)
    return w.reshape(k, MLA_HEADS * LANES).astype(BF16)


def _mla_mixer(proj, nb, lb, front, q_norm, w_uq, kv_norm, w_ukv):
    r = proj.shape[0]
    half = MLA_ROPE // 2
    qw = MLA_NOPE + MLA_ROPE
    rope_cols = jnp.arange(MLA_NOPE, qw)
    swap_cols = jnp.concatenate([rope_cols[half:], rope_cols[:half]])
    wqn = _pad_heads(w_uq, qw, jnp.arange(MLA_NOPE))
    wqr = _pad_heads(w_uq, qw, rope_cols)
    wqs = _pad_heads(w_uq, qw, swap_cols)
    wkn = _pad_heads(w_ukv, MLA_NOPE + MLA_V, jnp.arange(MLA_NOPE))
    wvt = _pad_heads(w_ukv, MLA_NOPE + MLA_V, jnp.arange(MLA_NOPE, MLA_NOPE + MLA_V)).T
    cc, ss = _rope_tables(lb, front)
    scale = qw ** -0.5 * math.log2(math.e)

    tq = _pick(lb, (640, 128))
    nq = lb // tq
    hl = MLA_HEADS * LANES
    lat_w = MLA_Q_RANK + MLA_KV_RANK + 2 * LANES
    const = lambda i: (0, 0)
    q, k, vt = pl.pallas_call(
        functools.partial(_mla_prep_kernel, scale=scale, tiles_per_batch=nq, front=front),
        grid=(r // tq,),
        in_specs=[pl.BlockSpec((tq, lat_w), lambda i: (i, 1)),
                  pl.BlockSpec((tq, LANES), lambda i: (i % nq, 0)),
                  pl.BlockSpec((tq, LANES), lambda i: (i % nq, 0)),
                  pl.BlockSpec((1, MLA_Q_RANK), const),
                  pl.BlockSpec((1, MLA_KV_RANK), const),
                  pl.BlockSpec((MLA_Q_RANK, hl), const),
                  pl.BlockSpec((MLA_Q_RANK, hl), const),
                  pl.BlockSpec((MLA_Q_RANK, hl), const),
                  pl.BlockSpec((MLA_KV_RANK, hl), const),
                  pl.BlockSpec((hl, MLA_KV_RANK), const)],
        out_specs=[pl.BlockSpec((MLA_HEADS, tq, 2 * LANES), lambda i: (0, i, 0)),
                   pl.BlockSpec((MLA_HEADS, tq, 2 * LANES), lambda i: (0, i, 0)),
                   pl.BlockSpec((MLA_HEADS, 1, MLA_V, tq), lambda i: (0, i, 0, 0))],
        out_shape=[jax.ShapeDtypeStruct((MLA_HEADS, r, 2 * LANES), BF16),
                   jax.ShapeDtypeStruct((MLA_HEADS, r, 2 * LANES), BF16),
                   jax.ShapeDtypeStruct((MLA_HEADS, r // tq, MLA_V, tq), BF16)],
        compiler_params=_params("parallel"),
        name="mla_prep",
    )(proj, cc, ss, q_norm.reshape(1, -1), kv_norm.reshape(1, -1), wqn, wqr, wqs, wkn, wvt)

    hs = 2
    return pl.pallas_call(
        functools.partial(_attn_kernel, tq=tq),
        grid=(nb, MLA_HEADS // hs, nq),
        in_specs=[pl.BlockSpec((hs, tq, 2 * LANES), lambda b, h, i: (h, b * nq + i, 0)),
                  pl.BlockSpec((hs, lb, 2 * LANES), lambda b, h, i: (h, b, 0)),
                  pl.BlockSpec((hs, nq, MLA_V, tq), lambda b, h, i: (h, b, 0, 0))],
        out_specs=pl.BlockSpec((tq, hs * MLA_V), lambda b, h, i: (b * nq + i, h)),
        out_shape=jax.ShapeDtypeStruct((r, hl), BF16),
        scratch_shapes=[pltpu.VMEM((hs, tq, tq), F32), pltpu.VMEM((hs, tq, tq), F32),
                        pltpu.VMEM((hs, 1, tq), F32), pltpu.VMEM((hs, 1, tq), F32),
                        pltpu.VMEM((hs, MLA_V, tq), F32)],
        compiler_params=_params("parallel", "parallel", "arbitrary"),
        name="mla_attention",
    )(q, k, vt)


def _conv_kernel(x_ref, halo_ref, w_ref, b_ref, o_ref, *, tm, lb, front):
    w = w_ref[...]
    bias = b_ref[...]
    k = w.shape[0]

    def taps(x, halo):
        acc = bias + w[k - 1:k, :] * x
        head = bias + w[k - 1:k, :] * x[:SUBLANES]
        sub = lax.broadcasted_iota(jnp.int32, halo.shape, 0)
        for d in range(1, k):
            wd = w[k - 1 - d:k - d, :]
            acc += wd * pltpu.roll(x, d, 0)
            head += wd * jnp.where(sub < d, pltpu.roll(halo, d, 0), pltpu.roll(x[:SUBLANES], d, 0))
        return acc, head

    def silu(v):
        return (v * jax.nn.sigmoid(v)).astype(o_ref.dtype)

    acc, head = taps(x_ref[...].astype(F32), halo_ref[...].astype(F32))
    o_ref[...] = silu(acc)
    o_ref[0:SUBLANES, :] = silu(head)

    hb = -(-(front + k) // (2 * SUBLANES)) * (2 * SUBLANES)
    @pl.when((pl.program_id(0) * tm) % lb == 0)
    def _():
        xh = x_ref[0:hb, :].astype(F32)
        xh = jnp.where(lax.broadcasted_iota(jnp.int32, xh.shape, 0) < front, 0.0, xh)
        acc_h, head_h = taps(xh, jnp.zeros((SUBLANES, xh.shape[1]), F32))
        o_ref[0:hb, :] = silu(acc_h)
        o_ref[0:SUBLANES, :] = silu(head_h)


def _split3(x):
    hi = x.astype(BF16).astype(F32)
    mid = (x - hi).astype(BF16).astype(F32)
    lo = (x - hi - mid).astype(BF16).astype(F32)
    return hi, mid, lo


def _ssd_kernel(xs_ref, bm_ref, cm_ref, dtc_ref, dtr_ref, bc_ref, br_ref, ac_ref, ar_ref,
                d_ref, z_ref, ng_ref, ex_ref, o_ref, state_ref, xdt_ref, *, front):
    c = pl.program_id(1)
    q = xs_ref.shape[0]
    groups, nh = dtr_ref.shape[0], dtr_ref.shape[1]
    hp = xs_ref.shape[1] // groups
    ns = bm_ref.shape[1] // groups
    hd = hp // nh
    wide = dtc_ref.shape[1]

    @pl.when(c == 0)
    def _():
        state_ref[...] = jnp.zeros(state_ref.shape, F32)

    first_live = jnp.where(c == 0, front, 0)
    live_c = lax.broadcasted_iota(jnp.int32, (q, wide), 0) >= first_live
    live_r = lax.broadcasted_iota(jnp.int32, (nh, q), 1) >= first_live
    li = lax.broadcasted_iota(jnp.int32, (q, q), 0)
    si = lax.broadcasted_iota(jnp.int32, (q, q), 1)
    lower = li >= si
    upper16 = (li <= si).astype(BF16)
    lane = lax.broadcasted_iota(jnp.int32, (q, 2 * hd), 1)

    def terms(x, axis):
        return jnp.concatenate(_split3(x), axis=axis).astype(BF16)

    dt_c = jnp.where(live_c, jax.nn.softplus(dtc_ref[...] + bc_ref[0]), 0.0)
    dt_terms = terms(dt_c, 1)
    cs3 = jnp.dot(lower.astype(BF16), terms(dt_c * ac_ref[0], 1), preferred_element_type=F32)
    cs_terms = terms(cs3[:, :wide] + cs3[:, wide:2 * wide] + cs3[:, 2 * wide:], 1)

    for g in range(groups):
        cols = slice(g * hp, (g + 1) * hp)
        expand3 = ex_ref[g]
        dt_full = jnp.dot(dt_terms, expand3, preferred_element_type=F32)
        cs_full = jnp.dot(cs_terms, expand3, preferred_element_type=F32)
        dt_r = jnp.where(live_r, jax.nn.softplus(dtr_ref[g] + br_ref[g]), 0.0)
        csr3 = jnp.dot(terms(dt_r * ar_ref[g], 0), upper16, preferred_element_type=F32)
        cs_r = csr3[:nh] + csr3[nh:2 * nh] + csr3[2 * nh:]
        last = cs_full[q - 1:q, :]

        xdt_ref[g] = xs_ref[:, cols].astype(F32) * dt_full
        bm = bm_ref[:, g * ns:(g + 1) * ns]
        cm = cm_ref[:, g * ns:(g + 1) * ns]
        cb = lax.dot_general(cm, bm, (((1,), (1,)), ((), ())), preferred_element_type=F32)

        prev = state_ref[g]
        y = jnp.dot(cm, prev.astype(BF16), preferred_element_type=F32) * jnp.exp2(cs_full)
        local = lax.dot_general(bm, (xdt_ref[g] * jnp.exp2(last - cs_full)).astype(BF16),
                                (((0,), (0,)), ((), ())), preferred_element_type=F32)
        state_ref[g] = prev * jnp.exp2(last) + local

        diag = []
        for pr in range(nh // 2):
            ms = []
            for h in (2 * pr, 2 * pr + 1):
                seg = cs_full[:, h * hd:h * hd + 1] - cs_r[h:h + 1, :]
                ms.append((cb * jnp.exp2(jnp.where(lower, seg, -jnp.inf))).astype(BF16))
            xp = xdt_ref[g, :, pr * 2 * hd:(pr + 1) * 2 * hd]
            rhs = jnp.concatenate([jnp.where(lane < hd, xp, 0.0), jnp.where(lane >= hd, xp, 0.0)], axis=0)
            diag.append(jnp.dot(jnp.concatenate(ms, axis=1), rhs.astype(BF16), preferred_element_type=F32))
        y = y + jnp.concatenate(diag, axis=1) + d_ref[g] * xs_ref[:, cols].astype(F32)

        z = z_ref[:, cols].astype(F32)
        y = y * (z * jax.nn.sigmoid(z))
        y = y * lax.rsqrt(jnp.mean(y * y, axis=-1, keepdims=True) + RMS_EPS) * ng_ref[:, cols]
        o_ref[:, cols] = y.astype(o_ref.dtype)


def _mamba2_mixer(h16, nb, lb, front, w_in, conv_w, conv_b, dt_bias, a_log, d, norm_g):
    r = h16.shape[0]
    inner = norm_g.shape[0]
    gn = SSD_GROUPS * SSD_STATE
    cdim = inner + 2 * gn
    tm = _pick(r, (1664, 1280, 640, 128))
    nheads = SSD_GROUPS * SSD_HPG
    tn = _pick(math.gcd(inner, cdim), (1024, 512, 128))
    z = _matmul_ws(h16, w_in, BF16, col0=0, ncols=inner, tm=tm, tn=tn, name="l1_in_z")
    xbc = _matmul_ws(h16, w_in, BF16, col0=inner, ncols=cdim, tm=tm, tn=tn, name="l1_in_xbc")
    w_dt = jnp.concatenate([w_in[:, inner + cdim:], jnp.zeros((w_in.shape[0], LANES - nheads), F32)], axis=1)
    dt_raw = _matmul(h16, w_dt.astype(BF16), F32, tm=tm, tn=LANES, name="l1_in_dt")

    tcv = _pick(cdim, (512, 256, 128))
    blocks = tm // SUBLANES
    assert lb % tm == 0
    act = pl.pallas_call(
        functools.partial(_conv_kernel, tm=tm, lb=lb, front=front),
        grid=(r // tm, cdim // tcv),
        in_specs=[pl.BlockSpec((tm, tcv), lambda i, j: (i, j)),
                  pl.BlockSpec((SUBLANES, tcv), lambda i, j: (jnp.maximum(i * blocks - 1, 0), j)),
                  pl.BlockSpec((SSD_CONV, tcv), lambda i, j: (0, j)),
                  pl.BlockSpec((1, tcv), lambda i, j: (0, j))],
        out_specs=pl.BlockSpec((tm, tcv), lambda i, j: (i, j)),
        out_shape=jax.ShapeDtypeStruct((r, cdim), BF16),
        compiler_params=_params("parallel", "parallel"),
        name="ssd_conv",
    )(xbc, xbc, conv_w, conv_b.reshape(1, cdim))

    gsz = inner // SSD_GROUPS
    nc = lb // SSD_BLOCK
    dt_r = dt_raw[:, :nheads].T.reshape(SSD_GROUPS, SSD_HPG, r)
    lane_pad = jnp.zeros((LANES - nheads,), F32)
    bias = dt_bias.reshape(SSD_GROUPS, SSD_HPG)
    bias_c = jnp.concatenate([dt_bias, lane_pad]).reshape(1, 1, LANES)
    a = -jnp.exp(a_log) * math.log2(math.e)
    a_c = jnp.concatenate([a, lane_pad]).reshape(1, 1, LANES)
    a = a.reshape(SSD_GROUPS, SSD_HPG)
    d_full = jnp.repeat(d, SSD_HEAD_DIM).reshape(SSD_GROUPS, 1, gsz)
    head_of_lane = (jnp.arange(gsz)[None, None, :] // SSD_HEAD_DIM
                    + SSD_HPG * jnp.arange(SSD_GROUPS)[:, None, None])
    expand = (jnp.arange(LANES)[None, :, None] == head_of_lane).astype(BF16)
    expand3 = jnp.concatenate([expand] * 3, axis=1)
    row = lambda b, c: b * nc + c
    whole3 = lambda b, c: (0, 0, 0)
    ng_ = SSD_GROUPS
    return pl.pallas_call(
        functools.partial(_ssd_kernel, front=front),
        grid=(nb, nc),
        in_specs=[pl.BlockSpec((SSD_BLOCK, inner), lambda b, c: (row(b, c), 0)),
                  pl.BlockSpec((SSD_BLOCK, gn), lambda b, c: (row(b, c), inner // gn)),
                  pl.BlockSpec((SSD_BLOCK, gn), lambda b, c: (row(b, c), inner // gn + 1)),
                  pl.BlockSpec((SSD_BLOCK, LANES), lambda b, c: (row(b, c), 0)),
                  pl.BlockSpec((ng_, SSD_HPG, SSD_BLOCK), lambda b, c: (0, 0, row(b, c))),
                  pl.BlockSpec((1, 1, LANES), whole3),
                  pl.BlockSpec((ng_, SSD_HPG, 1), whole3),
                  pl.BlockSpec((1, 1, LANES), whole3),
                  pl.BlockSpec((ng_, SSD_HPG, 1), whole3),
                  pl.BlockSpec((ng_, 1, gsz), whole3),
                  pl.BlockSpec((SSD_BLOCK, inner), lambda b, c: (row(b, c), 0)),
                  pl.BlockSpec((1, inner), lambda b, c: (0, 0)),
                  pl.BlockSpec((ng_, 3 * LANES, gsz), whole3)],
        out_specs=pl.BlockSpec((SSD_BLOCK, inner), lambda b, c: (row(b, c), 0)),
        out_shape=jax.ShapeDtypeStruct((r, inner), BF16),
        scratch_shapes=[pltpu.VMEM((ng_, SSD_STATE, gsz), F32), pltpu.VMEM((ng_, SSD_BLOCK, gsz), F32)],
        compiler_params=_params("parallel", "arbitrary"),
        name="ssd_scan",
    )(act, act, act, dt_raw, dt_r,
      bias_c, bias[:, :, None], a_c, a[:, :, None],
      d_full, z, norm_g.reshape(1, inner), expand3)


def kernel(x, meta_tokens, l0_w_in, l0_s5_log_dt, l0_s5_a_re, l0_s5_a_im, l0_s5_b_re, l0_s5_b_im, l0_s5_c_re, l0_s5_c_im, l0_s5_d, l0_s5_w_glu, l0_mla_q_norm, l0_mla_w_uq, l0_mla_kv_norm, l0_mla_w_ukv, l0_w_out, l0_ln1_g, l0_ln1_b, l0_ffn_w_gate, l0_ffn_w_up, l0_ffn_w_down, l0_ln2_g, l0_ln2_b, l1_w_in, l1_conv_w, l1_conv_b, l1_dt_bias, l1_a_log, l1_d, l1_norm_g, l1_w_out, l1_ln1_g, l1_ln1_b, l1_ffn_w_gate, l1_ffn_w_up, l1_ffn_w_down, l1_ln2_g, l1_ln2_b):
    nb, seq, dm = x.shape
    n_meta = meta_tokens.shape[0]
    front = (-n_meta) % SSD_BLOCK
    lb = front + n_meta + seq
    r = nb * lb
    meta = jnp.broadcast_to(meta_tokens[None].astype(x.dtype), (nb, n_meta, dm))
    h32 = jnp.concatenate([jnp.zeros((nb, front, dm), x.dtype), meta, x], axis=1).reshape(r, dm)

    s5w = l0_s5_d.shape[0]
    o3 = s5w + MLA_Q_RANK + MLA_KV_RANK
    kr = l0_w_in[:, o3:]
    half = MLA_ROPE // 2
    zpad = jnp.zeros((dm, LANES - MLA_ROPE), l0_w_in.dtype)
    w0 = jnp.concatenate([l0_w_in[:, :o3], kr, zpad, kr[:, half:], kr[:, :half], zpad], axis=1).astype(BF16)
    proj = _matmul(h32, w0, F32, tm=_pick(r, (832, 640, 128)), tn=_pick(w0.shape[1], (1024, 512, 128)),
                   name="l0_in")
    a_out = _s5_mixer(proj, nb, lb, front, l0_s5_log_dt, l0_s5_a_re, l0_s5_a_im, l0_s5_b_re, l0_s5_b_im,
                      l0_s5_c_re, l0_s5_c_im, l0_s5_d, l0_s5_w_glu)
    b_out = _mla_mixer(proj, nb, lb, front, l0_mla_q_norm, l0_mla_w_uq, l0_mla_kv_norm, l0_mla_w_ukv)
    tml = _pick(r, (640, 128))
    h32, h16 = _mm_res_ln((a_out, b_out), l0_w_out.astype(BF16), h32, l0_ln1_g, l0_ln1_b, tm=tml,
                          tk=a_out.shape[1], name="l0_out_ln")
    h32, h16 = _ffn_block(h32, h16, l0_ffn_w_gate, l0_ffn_w_up, l0_ffn_w_down, l0_ln2_g, l0_ln2_b, "l0")

    y = _mamba2_mixer(h16, nb, lb, front, l1_w_in, l1_conv_w, l1_conv_b, l1_dt_bias, l1_a_log, l1_d, l1_norm_g)
    h32, h16 = _mm_res_ln(y, l1_w_out.astype(BF16), h32, l1_ln1_g, l1_ln1_b, tm=tml, tk=1024, name="l1_out_ln")
    h32, h16 = _ffn_block(h32, h16, l1_ffn_w_gate, l1_ffn_w_up, l1_ffn_w_down, l1_ln2_g, l1_ln2_b, "l1")
    return h32.reshape(nb, lb, dm)[:, front + n_meta:, :]
```

```python
import functools
import math

import jax
import jax.numpy as jnp
from jax import lax
from jax.experimental import pallas as pl
from jax.experimental.pallas import tpu as pltpu

F32 = jnp.float32
BF16 = jnp.bfloat16

DEPTH = 2
CHUNK = 64
N_META = 16
DN_ALPHA = (2 * DEPTH) ** 0.25
LN_EPS = 1e-5
RMS_EPS = 1e-6
S5_GROUP = 16
S5_STATE = 64
S5_Q = 16
S5_LANE_GROUPS = 128 // S5_GROUP
MLA_HEADS = 8
MLA_NOPE = 128
MLA_ROPE = 64
MLA_V = 128
MLA_Q_RANK = 512
MLA_KV_RANK = 256
ROPE_BASE = 10000.0
SSD_HEAD_DIM = 64
SSD_GROUPS = 8
SSD_HPG = 8
SSD_STATE = 128
SSD_CONV = 4
SSD_BLOCK = 128

LANES = 128
SUBLANES = 8
VMEM_LIMIT = 56 * 1024 * 1024
MASK_VALUE = -1e30
HI = lax.Precision.HIGHEST


def _pick(n, candidates):
    for c in candidates:
        if n % c == 0:
            return c
    raise ValueError(f"no tile for {n} in {candidates}")


def _params(*sem):
    return pltpu.CompilerParams(dimension_semantics=sem, vmem_limit_bytes=VMEM_LIMIT)


def _mm_kernel(x_ref, w_ref, o_ref):
    o_ref[...] = jnp.dot(x_ref[...].astype(BF16), w_ref[...], preferred_element_type=F32).astype(o_ref.dtype)


def _matmul(x, w, out_dtype, *, tm, tn, name):
    m, k = x.shape
    n = w.shape[1]
    return pl.pallas_call(
        _mm_kernel,
        grid=(m // tm, n // tn),
        in_specs=[pl.BlockSpec((tm, k), lambda i, j: (i, 0)),
                  pl.BlockSpec((k, tn), lambda i, j: (0, j))],
        out_specs=pl.BlockSpec((tm, tn), lambda i, j: (i, j)),
        out_shape=jax.ShapeDtypeStruct((m, n), out_dtype),
        compiler_params=_params("parallel", "arbitrary"),
        name=name,
    )(x, w)


def _mm_ws_kernel(x_ref, w_ref, o_ref, wb_ref):
    @pl.when(pl.program_id(1) == 0)
    def _():
        wb_ref[...] = w_ref[...].astype(BF16)

    o_ref[...] = jnp.dot(x_ref[...], wb_ref[...], preferred_element_type=F32).astype(o_ref.dtype)


def _matmul_ws(x, w, out_dtype, *, col0, ncols, tm, tn, name):
    m, k = x.shape
    assert col0 % tn == 0 and ncols % tn == 0
    c0 = col0 // tn
    return pl.pallas_call(
        _mm_ws_kernel,
        grid=(ncols // tn, m // tm),
        in_specs=[pl.BlockSpec((tm, k), lambda j, i: (i, 0)),
                  pl.BlockSpec((k, tn), lambda j, i: (0, c0 + j))],
        out_specs=pl.BlockSpec((tm, tn), lambda j, i: (i, j)),
        out_shape=jax.ShapeDtypeStruct((m, ncols), out_dtype),
        scratch_shapes=[pltpu.VMEM((k, tn), BF16)],
        compiler_params=_params("parallel", "arbitrary"),
        name=name,
    )(x, w)


def _mm_res_ln_kernel(*refs, nk, parts):
    x_refs = refs[:parts]
    w_ref, res_ref, g_ref, b_ref, o32_ref, o16_ref, acc_ref = refs[parts:]
    k = pl.program_id(1)

    @pl.when(k == 0)
    def _():
        acc_ref[...] = jnp.zeros(acc_ref.shape, F32)

    if parts == 1:
        acc_ref[...] += jnp.dot(x_refs[0][...], w_ref[...], preferred_element_type=F32)
    else:
        for p in range(parts):
            @pl.when(k == p)
            def _(p=p):
                acc_ref[...] += jnp.dot(x_refs[p][...], w_ref[...], preferred_element_type=F32)

    @pl.when(k == nk - 1)
    def _():
        y = DN_ALPHA * res_ref[...] + acc_ref[...]
        mu = jnp.mean(y, axis=-1, keepdims=True)
        yc = y - mu
        var = jnp.mean(yc * yc, axis=-1, keepdims=True)
        out = yc * lax.rsqrt(var + LN_EPS) * g_ref[...] + b_ref[...]
        o32_ref[...] = out
        o16_ref[...] = out.astype(BF16)


def _mm_res_ln(x, w, res, g, b, *, tm, tk, name):
    xs = x if isinstance(x, tuple) else (x,)
    m = xs[0].shape[0]
    kk, d = w.shape
    nk = kk // tk
    if len(xs) > 1:
        assert len(xs) == nk and all(p.shape == (m, tk) for p in xs)
        x_specs = [pl.BlockSpec((tm, tk), lambda i, k: (i, 0)) for _ in xs]
    else:
        x_specs = [pl.BlockSpec((tm, tk), lambda i, k: (i, k))]
    return pl.pallas_call(
        functools.partial(_mm_res_ln_kernel, nk=nk, parts=len(xs)),
        grid=(m // tm, nk),
        in_specs=x_specs + [
                  pl.BlockSpec((tk, d), lambda i, k: (k, 0)),
                  pl.BlockSpec((tm, d), lambda i, k: (i, 0)),
                  pl.BlockSpec((1, d), lambda i, k: (0, 0)),
                  pl.BlockSpec((1, d), lambda i, k: (0, 0))],
        out_specs=[pl.BlockSpec((tm, d), lambda i, k: (i, 0)),
                   pl.BlockSpec((tm, d), lambda i, k: (i, 0))],
        out_shape=[jax.ShapeDtypeStruct((m, d), F32), jax.ShapeDtypeStruct((m, d), BF16)],
        scratch_shapes=[pltpu.VMEM((tm, d), F32)],
        compiler_params=_params("parallel", "arbitrary"),
        name=name,
    )(*xs, w, res, g.reshape(1, d), b.reshape(1, d))


def _ffn_up_kernel(x_ref, wg_ref, wu_ref, o_ref, wgb_ref, wub_ref):
    @pl.when(pl.program_id(1) == 0)
    def _():
        wgb_ref[...] = wg_ref[...].astype(BF16)
        wub_ref[...] = wu_ref[...].astype(BF16)

    x = x_ref[...]
    g = jnp.dot(x, wgb_ref[...], preferred_element_type=F32)
    u = jnp.dot(x, wub_ref[...], preferred_element_type=F32)
    o_ref[...] = (g * jax.nn.sigmoid(g) * u).astype(o_ref.dtype)


def _ffn_up(x, wg, wu, *, tm, tf, name):
    m, d = x.shape
    f = wg.shape[1]
    return pl.pallas_call(
        _ffn_up_kernel,
        grid=(f // tf, m // tm),
        in_specs=[pl.BlockSpec((tm, d), lambda j, i: (i, 0)),
                  pl.BlockSpec((d, tf), lambda j, i: (0, j)),
                  pl.BlockSpec((d, tf), lambda j, i: (0, j))],
        out_specs=pl.BlockSpec((tm, tf), lambda j, i: (i, j)),
        out_shape=jax.ShapeDtypeStruct((m, f), BF16),
        scratch_shapes=[pltpu.VMEM((d, tf), BF16), pltpu.VMEM((d, tf), BF16)],
        compiler_params=_params("parallel", "arbitrary"),
        name=name,
    )(x, wg, wu)


def _ffn_block(h32, h16, w_gate, w_up, w_down, g, b, tag):
    m = h16.shape[0]
    f = w_gate.shape[1]
    hid = _ffn_up(h16, w_gate, w_up,
                  tm=_pick(m, (1664, 1280, 640, 128)), tf=_pick(f, (512, 256, 128)),
                  name=f"{tag}_ffn_up")
    return _mm_res_ln(hid, w_down.astype(BF16), h32, g, b,
                      tm=_pick(m, (640, 128)), tk=_pick(f, (1408, 512, 256, 128)), name=f"{tag}_ffn_down_ln")


def _s5_tables(log_dt, a_re, a_im, b_re, b_im, c_re, c_im):
    g_, p_ = a_re.shape
    j_ = b_re.shape[-1]
    q = S5_Q
    dt = jnp.exp(log_dt)[:, None]
    lr = dt * a_re
    li = dt * a_im
    ks = jnp.arange(q + 1, dtype=F32)[:, None, None]
    mag = jnp.exp(ks * lr)
    pw_re = mag * jnp.cos(ks * li)
    pw_im = mag * jnp.sin(ks * li)
    ab_re, ab_im = pw_re[1], pw_im[1]
    den = a_re * a_re + a_im * a_im
    nr = ab_re - 1.0
    f_re = (nr * a_re + ab_im * a_im) / den
    f_im = (ab_im * a_re - nr * a_im) / den
    bb_re = f_re[..., None] * b_re - f_im[..., None] * b_im
    bb_im = f_re[..., None] * b_im + f_im[..., None] * b_re
    cp_re = c_re[None] * pw_re[:, :, None, :] - c_im[None] * pw_im[:, :, None, :]
    cp_im = c_re[None] * pw_im[:, :, None, :] + c_im[None] * pw_re[:, :, None, :]
    kern = jnp.sum(cp_re[:q, :, :, :, None] * bb_re[None, :, None, :, :]
                   - cp_im[:q, :, :, :, None] * bb_im[None, :, None, :, :], axis=3)
    kr = (q - 1 - jnp.arange(q)).astype(F32)[:, None, None]
    rev_mag = jnp.exp(kr * lr)
    rev_re = (rev_mag * jnp.cos(kr * li))[:, :, :, None]
    rev_im = (rev_mag * jnp.sin(kr * li))[:, :, :, None]
    st_re = (rev_re * bb_re[None] - rev_im * bb_im[None]).transpose(1, 0, 3, 2)
    st_im = (rev_re * bb_im[None] + rev_im * bb_re[None]).transpose(1, 0, 3, 2)
    wo_re = cp_re[1:].transpose(1, 3, 0, 2)
    wo_im = (-cp_im[1:]).transpose(1, 3, 0, 2)

    gl = S5_LANE_GROUPS
    nblk = g_ // gl

    def block_diag(w, rows_per_group, cols_per_group):
        w = jnp.tile(w, (1,) * (w.ndim - 1) + (gl,))
        row_g = jnp.arange(gl * rows_per_group)[:, None] // rows_per_group
        col_g = jnp.arange(gl * cols_per_group)[None, :] // cols_per_group
        return jnp.where(row_g == col_g, w, 0.0)

    kc = kern.reshape(q, nblk, gl, j_, j_).transpose(1, 0, 2, 4, 3).reshape(nblk, q, gl * j_, j_)
    lag_b = block_diag(kc, j_, j_).astype(BF16)

    def st_pairs(st):
        w = st.reshape(nblk, gl, q, j_, p_).transpose(0, 2, 1, 3, 4).reshape(nblk, q, gl * j_, p_)
        return jnp.concatenate([w, w], axis=3)

    def wo_blocks(wo):
        w = jnp.tile(wo.reshape(nblk, gl * p_, q, j_), (1, 1, 1, gl))
        row_g = jnp.arange(gl * p_)[:, None, None] // p_
        col_g = jnp.arange(gl * j_)[None, None, :] // j_
        return jnp.where(row_g == col_g, w, 0.0).reshape(nblk, gl * p_, q * gl * j_).astype(BF16)

    aq_re = pw_re[q].reshape(1, g_ * p_)
    aq_im = pw_im[q].reshape(1, g_ * p_)
    return lag_b, st_pairs(st_re), st_pairs(st_im), wo_blocks(wo_re), wo_blocks(wo_im), aq_re, aq_im


def _s5_rows(u_ref, front_rows):
    rows = u_ref.shape[0] // S5_Q
    live = lax.broadcasted_iota(jnp.int32, (rows, u_ref.shape[1]), 0) >= front_rows
    pieces = [jnp.where(live, u_ref[pl.ds(i, rows, stride=S5_Q), :], 0.0) for i in range(S5_Q)]
    return pieces, jnp.concatenate([p.astype(BF16) for p in pieces], axis=1)


def _s5_local_kernel(u_ref, sre_ref, sim_ref, zr_ref, zi_ref, w_ref, *, front_rows):
    @pl.when(pl.program_id(1) == 0)
    def _():
        half = w_ref.shape[1] // 2
        per_col = LANES // S5_STATE
        row_g = lax.broadcasted_iota(jnp.int32, (LANES, LANES), 0) // S5_GROUP
        lane_g = lax.broadcasted_iota(jnp.int32, (LANES, LANES), 1) // S5_STATE
        for i in range(S5_Q):
            rows = slice(i * LANES, (i + 1) * LANES)
            for kb in range(half // LANES):
                keep = row_g == lane_g + kb * per_col
                w_ref[rows, kb * LANES:(kb + 1) * LANES] = jnp.where(keep, sre_ref[0, i], 0.0).astype(BF16)
                w_ref[rows, half + kb * LANES:half + (kb + 1) * LANES] = (
                    jnp.where(keep, sim_ref[0, i], 0.0).astype(BF16))

    _, x = _s5_rows(u_ref, front_rows)
    z = jnp.dot(x, w_ref[...], preferred_element_type=F32)
    half = z.shape[1] // 2
    zr_ref[...] = z[:, :half]
    zi_ref[...] = z[:, half:]


def _s5_scan_kernel(zr_ref, zi_ref, ar_ref, ai_ref, sr_ref, si_ref, *, nb, nc):
    ar = ar_ref[...]
    ai = ai_ref[...]

    def body(c, carry):
        new = []
        for b in range(nb):
            s_re, s_im = carry[2 * b], carry[2 * b + 1]
            row = b * nc + c
            sr_ref[pl.ds(row, 1), :] = s_re
            si_ref[pl.ds(row, 1), :] = s_im
            z_re = zr_ref[pl.ds(row, 1), :]
            z_im = zi_ref[pl.ds(row, 1), :]
            new.append(ar * s_re - ai * s_im + z_re)
            new.append(ar * s_im + ai * s_re + z_im)
        return tuple(new)

    zero = jnp.zeros(ar.shape, F32)
    lax.fori_loop(0, nc, body, (zero,) * (2 * nb))


def _s5_out_kernel(u_ref, lag_ref, sr_ref, si_ref, wr_ref, wi_ref, d_ref, y_ref, toe_ref, *, front_rows):
    @pl.when(pl.program_id(1) == 0)
    def _():
        zero = jnp.zeros((LANES, LANES), toe_ref.dtype)
        for i in range(S5_Q):
            for j in range(S5_Q):
                toe_ref[i * LANES:(i + 1) * LANES, j * LANES:(j + 1) * LANES] = lag_ref[0, j - i] if j >= i else zero

    pieces, x = _s5_rows(u_ref, front_rows)
    rows = x.shape[0]
    y = jnp.dot(x, toe_ref[...], preferred_element_type=F32)
    y += jnp.dot(sr_ref[...].astype(BF16), wr_ref[0], preferred_element_type=F32)
    y += jnp.dot(si_ref[...].astype(BF16), wi_ref[0], preferred_element_type=F32)
    d = d_ref[...]
    for j in range(S5_Q):
        y_ref[pl.ds(j, rows, stride=S5_Q), :] = y[:, j * LANES:(j + 1) * LANES] + d * pieces[j]


def _glu_kernel(y_ref, w_ref, o_ref):
    g = jax.nn.gelu(y_ref[...])
    gate = jax.nn.sigmoid(jnp.dot(g.astype(BF16), w_ref[...], preferred_element_type=F32))
    o_ref[...] = (g * gate).astype(o_ref.dtype)


def _s5_mixer(proj, nb, lb, front, log_dt, a_re, a_im, b_re, b_im, c_re, c_im, d, w_glu):
    r = proj.shape[0]
    g_, p_ = a_re.shape
    w_ = g_ * S5_GROUP
    q = S5_Q
    m = r // q
    nc = lb // q
    nblk = w_ // LANES
    sw = S5_LANE_GROUPS * p_
    kw = q * LANES
    front_rows = front // q
    lag_b, st_re_c, st_im_c, wo_re_b, wo_im_b, aq_re, aq_im = _s5_tables(
        log_dt, a_re, a_im, b_re, b_im, c_re, c_im)

    z_re, z_im = pl.pallas_call(
        functools.partial(_s5_local_kernel, front_rows=front_rows),
        grid=(nblk, nb),
        in_specs=[pl.BlockSpec((lb, LANES), lambda c, b: (b, c)),
                  pl.BlockSpec((1, q, LANES, LANES), lambda c, b: (c, 0, 0, 0)),
                  pl.BlockSpec((1, q, LANES, LANES), lambda c, b: (c, 0, 0, 0))],
        out_specs=[pl.BlockSpec((nc, sw), lambda c, b: (b, c)),
                   pl.BlockSpec((nc, sw), lambda c, b: (b, c))],
        out_shape=[jax.ShapeDtypeStruct((m, g_ * p_), F32)] * 2,
        scratch_shapes=[pltpu.VMEM((kw, 2 * sw), BF16)],
        compiler_params=_params("parallel", "arbitrary"),
        name="s5_local",
    )(proj, st_re_c, st_im_c)

    cw = _pick(g_ * p_, (512, 256, 128))
    s_re, s_im = pl.pallas_call(
        functools.partial(_s5_scan_kernel, nb=nb, nc=nc),
        grid=(g_ * p_ // cw,),
        in_specs=[pl.BlockSpec((m, cw), lambda i: (0, i)),
                  pl.BlockSpec((m, cw), lambda i: (0, i)),
                  pl.BlockSpec((1, cw), lambda i: (0, i)),
                  pl.BlockSpec((1, cw), lambda i: (0, i))],
        out_specs=[pl.BlockSpec((m, cw), lambda i: (0, i)),
                   pl.BlockSpec((m, cw), lambda i: (0, i))],
        out_shape=[jax.ShapeDtypeStruct((m, g_ * p_), F32)] * 2,
        compiler_params=_params("parallel"),
        name="s5_scan",
    )(z_re, z_im, aq_re, aq_im)

    y = pl.pallas_call(
        functools.partial(_s5_out_kernel, front_rows=front_rows),
        grid=(nblk, nb),
        in_specs=[pl.BlockSpec((lb, LANES), lambda c, b: (b, c)),
                  pl.BlockSpec((1, q, LANES, LANES), lambda c, b: (c, 0, 0, 0)),
                  pl.BlockSpec((nc, sw), lambda c, b: (b, c)),
                  pl.BlockSpec((nc, sw), lambda c, b: (b, c)),
                  pl.BlockSpec((1, sw, kw), lambda c, b: (c, 0, 0)),
                  pl.BlockSpec((1, sw, kw), lambda c, b: (c, 0, 0)),
                  pl.BlockSpec((1, LANES), lambda c, b: (0, c))],
        out_specs=pl.BlockSpec((lb, LANES), lambda c, b: (b, c)),
        out_shape=jax.ShapeDtypeStruct((r, w_), F32),
        scratch_shapes=[pltpu.VMEM((kw, kw), BF16)],
        compiler_params=_params("parallel", "arbitrary"),
        name="s5_out",
    )(proj, lag_b, s_re, s_im, wo_re_b, wo_im_b, d.reshape(1, w_))

    tm = _pick(r, (832, 640, 128))
    return pl.pallas_call(
        _glu_kernel,
        grid=(r // tm,),
        in_specs=[pl.BlockSpec((tm, w_), lambda i: (i, 0)),
                  pl.BlockSpec((w_, w_), lambda i: (0, 0))],
        out_specs=pl.BlockSpec((tm, w_), lambda i: (i, 0)),
        out_shape=jax.ShapeDtypeStruct((r, w_), BF16),
        compiler_params=_params("parallel"),
        name="s5_glu",
    )(y, w_glu.astype(BF16))


def _rms(x, g):
    return x * lax.rsqrt(jnp.mean(x * x, axis=-1, keepdims=True) + RMS_EPS) * g


def _mla_prep_kernel(lat_ref, cc_ref, ss_ref, qg_ref, kg_ref, wqn_ref, wqr_ref, wqs_ref, wkn_ref, wvt_ref,
                     q_ref, k_ref, vt_ref, *, scale, tiles_per_batch, front):
    tm = lat_ref.shape[0]
    lat = lat_ref[...]
    o1 = MLA_Q_RANK
    o2 = o1 + MLA_KV_RANK
    o3 = o2 + LANES
    qn = _rms(lat[:, :o1], qg_ref[...]).astype(BF16)
    kvn = _rms(lat[:, o1:o2], kg_ref[...]).astype(BF16)
    cc = cc_ref[...]
    ss = ss_ref[...]
    lane = lax.broadcasted_iota(jnp.int32, (tm, LANES), 1)
    row = (pl.program_id(0) % tiles_per_batch) * tm + lax.broadcasted_iota(jnp.int32, (tm, LANES), 0)
    bias_lane = lane == MLA_ROPE
    k_rope = jnp.where(bias_lane & (row < front), MASK_VALUE, lat[:, o2:o3] * cc + lat[:, o3:] * ss).astype(BF16)
    q_nope = jnp.dot(qn, wqn_ref[...], preferred_element_type=F32) * scale
    q_r = jnp.dot(qn, wqr_ref[...], preferred_element_type=F32)
    q_s = jnp.dot(qn, wqs_ref[...], preferred_element_type=F32)
    k_nope = jnp.dot(kvn, wkn_ref[...], preferred_element_type=F32)
    v_t = lax.dot_general(wvt_ref[...], kvn, (((1,), (1,)), ((), ())), preferred_element_type=F32)
    for h in range(MLA_HEADS):
        sl = slice(h * LANES, (h + 1) * LANES)
        q_ref[h, :, 0:LANES] = q_nope[:, sl].astype(BF16)
        q_rope = jnp.where(bias_lane, 1.0, (q_r[:, sl] * cc + q_s[:, sl] * ss) * scale)
        q_ref[h, :, LANES:2 * LANES] = q_rope.astype(BF16)
        k_ref[h, :, 0:LANES] = k_nope[:, sl].astype(BF16)
        k_ref[h, :, LANES:2 * LANES] = k_rope
        vt_ref[h, 0] = v_t[h * MLA_V:(h + 1) * MLA_V, :].astype(BF16)


def _attn_kernel(q_ref, k_ref, vt_ref, o_ref, sa_ref, sb_ref, m_ref, l_ref, acc_ref, *, tq):
    i = pl.program_id(2)
    heads = q_ref.shape[0]
    m_ref[...] = jnp.full(m_ref.shape, MASK_VALUE, F32)
    l_ref[...] = jnp.zeros(l_ref.shape, F32)
    acc_ref[...] = jnp.zeros(acc_ref.shape, F32)
    shift = CHUNK.bit_length() - 1

    def scores(s_ref, j):
        for h in range(heads):
            k = k_ref[h, pl.ds(pl.multiple_of(j * tq, tq), tq), :]
            s_ref[h] = lax.dot_general(k, q_ref[h], (((1,), (1,)), ((), ())), preferred_element_type=F32)

    def absorb(s_ref, j, diagonal):
        for h in range(heads):
            s = s_ref[h]
            if diagonal:
                k_chunk = lax.broadcasted_iota(jnp.int32, (tq, tq), 0) >> shift
                q_chunk = lax.broadcasted_iota(jnp.int32, (tq, tq), 1) >> shift
                s = jnp.where(k_chunk <= q_chunk, s, MASK_VALUE)
            m_prev = m_ref[h]
            m_new = jnp.maximum(m_prev, jnp.max(s, axis=0, keepdims=True))
            alpha = jnp.exp2(m_prev - m_new)
            p = jnp.exp2(s - m_new)
            l_ref[h] = alpha * l_ref[h] + jnp.sum(p, axis=0, keepdims=True)
            acc_ref[h] = alpha * acc_ref[h] + jnp.dot(vt_ref[h, j], p.astype(BF16), preferred_element_type=F32)
            m_ref[h] = m_new

    scores(sa_ref, 0)

    def body(t, carry):
        j = 2 * t
        scores(sb_ref, j + 1)
        absorb(sa_ref, j, False)
        scores(sa_ref, j + 2)
        absorb(sb_ref, j + 1, False)
        return carry

    lax.fori_loop(0, i // 2, body, 0)

    @pl.when(i % 2 == 0)
    def _():
        absorb(sa_ref, i, True)

    @pl.when(i % 2 == 1)
    def _():
        scores(sb_ref, i)
        absorb(sa_ref, i - 1, False)
        absorb(sb_ref, i, True)

    for h in range(heads):
        o_ref[:, h * MLA_V:(h + 1) * MLA_V] = (acc_ref[h] / l_ref[h]).T.astype(o_ref.dtype)


def _rope_tables(lb, front):
    pos = (jnp.arange(lb) - front).astype(F32)
    inv = ROPE_BASE ** (-jnp.arange(0, MLA_ROPE, 2, dtype=F32) / MLA_ROPE)
    ang = pos[:, None] * inv[None, :]
    cos, sin = jnp.cos(ang), jnp.sin(ang)
    zeros = jnp.zeros((lb, LANES - MLA_ROPE), F32)
    cc = jnp.concatenate([cos, cos, zeros], axis=1)
    ss = jnp.concatenate([-sin, sin, zeros], axis=1)
    return cc, ss


def _pad_heads(w, width, take):
    k = w.shape[0]
    w = w.reshape(k, MLA_HEADS, width)[:, :, take]
    pad = LANES - w.shape[-1]
    if pad:
        w = jnp.concatenate([w, jnp.zeros((k, MLA_HEADS, pad), w.dtype)], axis=-1)
    return w.reshape(k, MLA_HEADS * LANES).astype(BF16)


def _mla_mixer(proj, nb, lb, front, q_norm, w_uq, kv_norm, w_ukv):
    r = proj.shape[0]
    half = MLA_ROPE // 2
    qw = MLA_NOPE + MLA_ROPE
    rope_cols = jnp.arange(MLA_NOPE, qw)
    swap_cols = jnp.concatenate([rope_cols[half:], rope_cols[:half]])
    wqn = _pad_heads(w_uq, qw, jnp.arange(MLA_NOPE))
    wqr = _pad_heads(w_uq, qw, rope_cols)
    wqs = _pad_heads(w_uq, qw, swap_cols)
    wkn = _pad_heads(w_ukv, MLA_NOPE + MLA_V, jnp.arange(MLA_NOPE))
    wvt = _pad_heads(w_ukv, MLA_NOPE + MLA_V, jnp.arange(MLA_NOPE, MLA_NOPE + MLA_V)).T
    cc, ss = _rope_tables(lb, front)
    scale = qw ** -0.5 * math.log2(math.e)

    tq = _pick(lb, (640, 128))
    nq = lb // tq
    hl = MLA_HEADS * LANES
    lat_w = MLA_Q_RANK + MLA_KV_RANK + 2 * LANES
    const = lambda i: (0, 0)
    q, k, vt = pl.pallas_call(
        functools.partial(_mla_prep_kernel, scale=scale, tiles_per_batch=nq, front=front),
        grid=(r // tq,),
        in_specs=[pl.BlockSpec((tq, lat_w), lambda i: (i, 1)),
                  pl.BlockSpec((tq, LANES), lambda i: (i % nq, 0)),
                  pl.BlockSpec((tq, LANES), lambda i: (i % nq, 0)),
                  pl.BlockSpec((1, MLA_Q_RANK), const),
                  pl.BlockSpec((1, MLA_KV_RANK), const),
                  pl.BlockSpec((MLA_Q_RANK, hl), const),
                  pl.BlockSpec((MLA_Q_RANK, hl), const),
                  pl.BlockSpec((MLA_Q_RANK, hl), const),
                  pl.BlockSpec((MLA_KV_RANK, hl), const),
                  pl.BlockSpec((hl, MLA_KV_RANK), const)],
        out_specs=[pl.BlockSpec((MLA_HEADS, tq, 2 * LANES), lambda i: (0, i, 0)),
                   pl.BlockSpec((MLA_HEADS, tq, 2 * LANES), lambda i: (0, i, 0)),
                   pl.BlockSpec((MLA_HEADS, 1, MLA_V, tq), lambda i: (0, i, 0, 0))],
        out_shape=[jax.ShapeDtypeStruct((MLA_HEADS, r, 2 * LANES), BF16),
                   jax.ShapeDtypeStruct((MLA_HEADS, r, 2 * LANES), BF16),
                   jax.ShapeDtypeStruct((MLA_HEADS, r // tq, MLA_V, tq), BF16)],
        compiler_params=_params("parallel"),
        name="mla_prep",
    )(proj, cc, ss, q_norm.reshape(1, -1), kv_norm.reshape(1, -1), wqn, wqr, wqs, wkn, wvt)

    hs = 2
    return pl.pallas_call(
        functools.partial(_attn_kernel, tq=tq),
        grid=(nb, MLA_HEADS // hs, nq),
        in_specs=[pl.BlockSpec((hs, tq, 2 * LANES), lambda b, h, i: (h, b * nq + i, 0)),
                  pl.BlockSpec((hs, lb, 2 * LANES), lambda b, h, i: (h, b, 0)),
                  pl.BlockSpec((hs, nq, MLA_V, tq), lambda b, h, i: (h, b, 0, 0))],
        out_specs=pl.BlockSpec((tq, hs * MLA_V), lambda b, h, i: (b * nq + i, h)),
        out_shape=jax.ShapeDtypeStruct((r, hl), BF16),
        scratch_shapes=[pltpu.VMEM((hs, tq, tq), F32), pltpu.VMEM((hs, tq, tq), F32),
                        pltpu.VMEM((hs, 1, tq), F32), pltpu.VMEM((hs, 1, tq), F32),
                        pltpu.VMEM((hs, MLA_V, tq), F32)],
        compiler_params=_params("parallel", "parallel", "arbitrary"),
        name="mla_attention",
    )(q, k, vt)


def _proj_conv_kernel(x_ref, xh_ref, w_ref, cw_ref, cb_ref, o_ref, wb_ref, pre_ref, *, tm, lb, front):
    tile = pl.program_id(1)

    @pl.when(tile == 0)
    def _():
        wb_ref[...] = w_ref[...].astype(BF16)

    pre_ref[...] = jnp.dot(x_ref[...], wb_ref[...], preferred_element_type=F32)
    pre = pre_ref[...]
    pre_halo = jnp.dot(xh_ref[...], wb_ref[...], preferred_element_type=F32)
    w = cw_ref[...]
    bias = cb_ref[...]
    k = w.shape[0]

    def taps(x, halo):
        acc = bias + w[k - 1:k, :] * x
        head = bias + w[k - 1:k, :] * x[:SUBLANES]
        sub = lax.broadcasted_iota(jnp.int32, halo.shape, 0)
        for d in range(1, k):
            wd = w[k - 1 - d:k - d, :]
            acc += wd * pltpu.roll(x, d, 0)
            head += wd * jnp.where(sub < d, pltpu.roll(halo, d, 0), pltpu.roll(x[:SUBLANES], d, 0))
        return acc, head

    def silu(v):
        return (v * jax.nn.sigmoid(v)).astype(o_ref.dtype)

    acc, head = taps(pre, pre_halo)
    o_ref[...] = silu(acc)
    o_ref[0:SUBLANES, :] = silu(head)

    hb = -(-(front + k) // (2 * SUBLANES)) * (2 * SUBLANES)
    @pl.when((tile * tm) % lb == 0)
    def _():
        xh = pre_ref[0:hb, :]
        xh = jnp.where(lax.broadcasted_iota(jnp.int32, xh.shape, 0) < front, 0.0, xh)
        acc_h, head_h = taps(xh, jnp.zeros((SUBLANES, xh.shape[1]), F32))
        o_ref[0:hb, :] = silu(acc_h)
        o_ref[0:SUBLANES, :] = silu(head_h)


def _split3(x):
    hi = x.astype(BF16).astype(F32)
    mid = (x - hi).astype(BF16).astype(F32)
    lo = (x - hi - mid).astype(BF16).astype(F32)
    return hi, mid, lo


def _ssd_kernel(xs_ref, bm_ref, cm_ref, dtc_ref, dtr_ref, bc_ref, br_ref, ac_ref, ar_ref,
                d_ref, z_ref, ng_ref, ex_ref, o_ref, state_ref, xdt_ref, *, front):
    c = pl.program_id(1)
    q = xs_ref.shape[0]
    groups, nh = dtr_ref.shape[0], dtr_ref.shape[1]
    hp = xs_ref.shape[1] // groups
    ns = bm_ref.shape[1] // groups
    hd = hp // nh
    wide = dtc_ref.shape[1]

    @pl.when(c == 0)
    def _():
        state_ref[...] = jnp.zeros(state_ref.shape, F32)

    first_live = jnp.where(c == 0, front, 0)
    live_c = lax.broadcasted_iota(jnp.int32, (q, wide), 0) >= first_live
    live_r = lax.broadcasted_iota(jnp.int32, (nh, q), 1) >= first_live
    li = lax.broadcasted_iota(jnp.int32, (q, q), 0)
    si = lax.broadcasted_iota(jnp.int32, (q, q), 1)
    lower = li >= si
    upper16 = (li <= si).astype(BF16)
    lane = lax.broadcasted_iota(jnp.int32, (q, 2 * hd), 1)

    def terms(x, axis):
        return jnp.concatenate(_split3(x), axis=axis).astype(BF16)

    dt_c = jnp.where(live_c, jax.nn.softplus(dtc_ref[...] + bc_ref[0]), 0.0)
    dt_terms = terms(dt_c, 1)
    cs3 = jnp.dot(lower.astype(BF16), terms(dt_c * ac_ref[0], 1), preferred_element_type=F32)
    cs_terms = terms(cs3[:, :wide] + cs3[:, wide:2 * wide] + cs3[:, 2 * wide:], 1)

    for g in range(groups):
        cols = slice(g * hp, (g + 1) * hp)
        expand3 = ex_ref[g]
        dt_full = jnp.dot(dt_terms, expand3, preferred_element_type=F32)
        cs_full = jnp.dot(cs_terms, expand3, preferred_element_type=F32)
        dt_r = jnp.where(live_r, jax.nn.softplus(dtr_ref[g] + br_ref[g]), 0.0)
        csr3 = jnp.dot(terms(dt_r * ar_ref[g], 0), upper16, preferred_element_type=F32)
        cs_r = csr3[:nh] + csr3[nh:2 * nh] + csr3[2 * nh:]
        last = cs_full[q - 1:q, :]

        xdt_ref[g] = xs_ref[:, cols].astype(F32) * dt_full
        bm = bm_ref[:, g * ns:(g + 1) * ns]
        cm = cm_ref[:, g * ns:(g + 1) * ns]
        cb = lax.dot_general(cm, bm, (((1,), (1,)), ((), ())), preferred_element_type=F32)

        prev = state_ref[g]
        y = jnp.dot(cm, prev.astype(BF16), preferred_element_type=F32) * jnp.exp2(cs_full)
        local = lax.dot_general(bm, (xdt_ref[g] * jnp.exp2(last - cs_full)).astype(BF16),
                                (((0,), (0,)), ((), ())), preferred_element_type=F32)
        state_ref[g] = prev * jnp.exp2(last) + local

        diag = []
        for pr in range(nh // 2):
            ms = []
            for h in (2 * pr, 2 * pr + 1):
                seg = cs_full[:, h * hd:h * hd + 1] - cs_r[h:h + 1, :]
                ms.append((cb * jnp.exp2(jnp.where(lower, seg, -jnp.inf))).astype(BF16))
            xp = xdt_ref[g, :, pr * 2 * hd:(pr + 1) * 2 * hd]
            rhs = jnp.concatenate([jnp.where(lane < hd, xp, 0.0), jnp.where(lane >= hd, xp, 0.0)], axis=0)
            diag.append(jnp.dot(jnp.concatenate(ms, axis=1), rhs.astype(BF16), preferred_element_type=F32))
        y = y + jnp.concatenate(diag, axis=1) + d_ref[g] * xs_ref[:, cols].astype(F32)

        z = z_ref[:, cols].astype(F32)
        y = y * (z * jax.nn.sigmoid(z))
        y = y * lax.rsqrt(jnp.mean(y * y, axis=-1, keepdims=True) + RMS_EPS) * ng_ref[:, cols]
        o_ref[:, cols] = y.astype(o_ref.dtype)


def _mamba2_mixer(h16, nb, lb, front, w_in, conv_w, conv_b, dt_bias, a_log, d, norm_g):
    r = h16.shape[0]
    inner = norm_g.shape[0]
    gn = SSD_GROUPS * SSD_STATE
    cdim = inner + 2 * gn
    tm = _pick(r, (1664, 1280, 640, 128))
    nheads = SSD_GROUPS * SSD_HPG
    tn = _pick(math.gcd(inner, cdim), (1024, 512, 128))
    z = _matmul_ws(h16, w_in, BF16, col0=0, ncols=inner, tm=tm, tn=tn, name="l1_in_z")
    tcv = _pick(math.gcd(inner, cdim), (512, 256, 128))
    blocks = tm // SUBLANES
    dm = h16.shape[1]
    c0 = inner // tcv
    assert lb % tm == 0
    act = pl.pallas_call(
        functools.partial(_proj_conv_kernel, tm=tm, lb=lb, front=front),
        grid=(cdim // tcv, r // tm),
        in_specs=[pl.BlockSpec((tm, dm), lambda j, i: (i, 0)),
                  pl.BlockSpec((SUBLANES, dm), lambda j, i: (jnp.maximum(i * blocks - 1, 0), 0)),
                  pl.BlockSpec((dm, tcv), lambda j, i: (0, c0 + j)),
                  pl.BlockSpec((SSD_CONV, tcv), lambda j, i: (0, j)),
                  pl.BlockSpec((1, tcv), lambda j, i: (0, j))],
        out_specs=pl.BlockSpec((tm, tcv), lambda j, i: (i, j)),
        out_shape=jax.ShapeDtypeStruct((r, cdim), BF16),
        scratch_shapes=[pltpu.VMEM((dm, tcv), BF16), pltpu.VMEM((tm, tcv), F32)],
        compiler_params=_params("parallel", "arbitrary"),
        name="l1_in_xbc_conv",
    )(h16, h16, w_in, conv_w, conv_b.reshape(1, cdim))
    w_dt = jnp.concatenate([w_in[:, inner + cdim:], jnp.zeros((w_in.shape[0], LANES - nheads), F32)], axis=1)
    dt_raw = _matmul(h16, w_dt.astype(BF16), F32, tm=tm, tn=LANES, name="l1_in_dt")

    gsz = inner // SSD_GROUPS
    nc = lb // SSD_BLOCK
    dt_r = dt_raw[:, :nheads].T.reshape(SSD_GROUPS, SSD_HPG, r)
    lane_pad = jnp.zeros((LANES - nheads,), F32)
    bias = dt_bias.reshape(SSD_GROUPS, SSD_HPG)
    bias_c = jnp.concatenate([dt_bias, lane_pad]).reshape(1, 1, LANES)
    a = -jnp.exp(a_log) * math.log2(math.e)
    a_c = jnp.concatenate([a, lane_pad]).reshape(1, 1, LANES)
    a = a.reshape(SSD_GROUPS, SSD_HPG)
    d_full = jnp.repeat(d, SSD_HEAD_DIM).reshape(SSD_GROUPS, 1, gsz)
    head_of_lane = (jnp.arange(gsz)[None, None, :] // SSD_HEAD_DIM
                    + SSD_HPG * jnp.arange(SSD_GROUPS)[:, None, None])
    expand = (jnp.arange(LANES)[None, :, None] == head_of_lane).astype(BF16)
    expand3 = jnp.concatenate([expand] * 3, axis=1)
    row = lambda b, c: b * nc + c
    whole3 = lambda b, c: (0, 0, 0)
    ng_ = SSD_GROUPS
    return pl.pallas_call(
        functools.partial(_ssd_kernel, front=front),
        grid=(nb, nc),
        in_specs=[pl.BlockSpec((SSD_BLOCK, inner), lambda b, c: (row(b, c), 0)),
                  pl.BlockSpec((SSD_BLOCK, gn), lambda b, c: (row(b, c), inner // gn)),
                  pl.BlockSpec((SSD_BLOCK, gn), lambda b, c: (row(b, c), inner // gn + 1)),
                  pl.BlockSpec((SSD_BLOCK, LANES), lambda b, c: (row(b, c), 0)),
                  pl.BlockSpec((ng_, SSD_HPG, SSD_BLOCK), lambda b, c: (0, 0, row(b, c))),
                  pl.BlockSpec((1, 1, LANES), whole3),
                  pl.BlockSpec((ng_, SSD_HPG, 1), whole3),
                  pl.BlockSpec((1, 1, LANES), whole3),
                  pl.BlockSpec((ng_, SSD_HPG, 1), whole3),
                  pl.BlockSpec((ng_, 1, gsz), whole3),
                  pl.BlockSpec((SSD_BLOCK, inner), lambda b, c: (row(b, c), 0)),
                  pl.BlockSpec((1, inner), lambda b, c: (0, 0)),
                  pl.BlockSpec((ng_, 3 * LANES, gsz), whole3)],
        out_specs=pl.BlockSpec((SSD_BLOCK, inner), lambda b, c: (row(b, c), 0)),
        out_shape=jax.ShapeDtypeStruct((r, inner), BF16),
        scratch_shapes=[pltpu.VMEM((ng_, SSD_STATE, gsz), F32), pltpu.VMEM((ng_, SSD_BLOCK, gsz), F32)],
        compiler_params=_params("parallel", "arbitrary"),
        name="ssd_scan",
    )(act, act, act, dt_raw, dt_r,
      bias_c, bias[:, :, None], a_c, a[:, :, None],
      d_full, z, norm_g.reshape(1, inner), expand3)


def kernel(x, meta_tokens, l0_w_in, l0_s5_log_dt, l0_s5_a_re, l0_s5_a_im, l0_s5_b_re, l0_s5_b_im, l0_s5_c_re, l0_s5_c_im, l0_s5_d, l0_s5_w_glu, l0_mla_q_norm, l0_mla_w_uq, l0_mla_kv_norm, l0_mla_w_ukv, l0_w_out, l0_ln1_g, l0_ln1_b, l0_ffn_w_gate, l0_ffn_w_up, l0_ffn_w_down, l0_ln2_g, l0_ln2_b, l1_w_in, l1_conv_w, l1_conv_b, l1_dt_bias, l1_a_log, l1_d, l1_norm_g, l1_w_out, l1_ln1_g, l1_ln1_b, l1_ffn_w_gate, l1_ffn_w_up, l1_ffn_w_down, l1_ln2_g, l1_ln2_b):
    nb, seq, dm = x.shape
    n_meta = meta_tokens.shape[0]
    front = (-n_meta) % SSD_BLOCK
    lb = front + n_meta + seq
    r = nb * lb
    meta = jnp.broadcast_to(meta_tokens[None].astype(x.dtype), (nb, n_meta, dm))
    h32 = jnp.concatenate([jnp.zeros((nb, front, dm), x.dtype), meta, x], axis=1).reshape(r, dm)

    s5w = l0_s5_d.shape[0]
    o3 = s5w + MLA_Q_RANK + MLA_KV_RANK
    kr = l0_w_in[:, o3:]
    half = MLA_ROPE // 2
    zpad = jnp.zeros((dm, LANES - MLA_ROPE), l0_w_in.dtype)
    w0 = jnp.concatenate([l0_w_in[:, :o3], kr, zpad, kr[:, half:], kr[:, :half], zpad], axis=1).astype(BF16)
    proj = _matmul(h32, w0, F32, tm=_pick(r, (832, 640, 128)), tn=_pick(w0.shape[1], (1024, 512, 128)),
                   name="l0_in")
    a_out = _s5_mixer(proj, nb, lb, front, l0_s5_log_dt, l0_s5_a_re, l0_s5_a_im, l0_s5_b_re, l0_s5_b_im,
                      l0_s5_c_re, l0_s5_c_im, l0_s5_d, l0_s5_w_glu)
    b_out = _mla_mixer(proj, nb, lb, front, l0_mla_q_norm, l0_mla_w_uq, l0_mla_kv_norm, l0_mla_w_ukv)
    tml = _pick(r, (640, 128))
    h32, h16 = _mm_res_ln((a_out, b_out), l0_w_out.astype(BF16), h32, l0_ln1_g, l0_ln1_b, tm=tml,
                          tk=a_out.shape[1], name="l0_out_ln")
    h32, h16 = _ffn_block(h32, h16, l0_ffn_w_gate, l0_ffn_w_up, l0_ffn_w_down, l0_ln2_g, l0_ln2_b, "l0")

    y = _mamba2_mixer(h16, nb, lb, front, l1_w_in, l1_conv_w, l1_conv_b, l1_dt_bias, l1_a_log, l1_d, l1_norm_g)
    h32, h16 = _mm_res_ln(y, l1_w_out.astype(BF16), h32, l1_ln1_g, l1_ln1_b, tm=tml, tk=1024, name="l1_out_ln")
    h32, h16 = _ffn_block(h32, h16, l1_ffn_w_gate, l1_ffn_w_up, l1_ffn_w_down, l1_ln2_g, l1_ln2_b, "l1")
    return h32.reshape(nb, lb, dm)[:, front + n_meta:, :]
```

```python
import functools
import math

import jax
import jax.numpy as jnp
from jax import lax
from jax.experimental import pallas as pl
from jax.experimental.pallas import tpu as pltpu

F32 = jnp.float32
BF16 = jnp.bfloat16

DEPTH = 2
CHUNK = 64
N_META = 16
DN_ALPHA = (2 * DEPTH) ** 0.25
LN_EPS = 1e-5
RMS_EPS = 1e-6
S5_GROUP = 16
S5_STATE = 64
S5_Q = 16
S5_LANE_GROUPS = 128 // S5_GROUP
MLA_HEADS = 8
MLA_NOPE = 128
MLA_ROPE = 64
MLA_V = 128
MLA_Q_RANK = 512
MLA_KV_RANK = 256
ROPE_BASE = 10000.0
SSD_HEAD_DIM = 64
SSD_GROUPS = 8
SSD_HPG = 8
SSD_STATE = 128
SSD_CONV = 4
SSD_BLOCK = 128

LANES = 128
SUBLANES = 8
VMEM_LIMIT = 56 * 1024 * 1024
MASK_VALUE = -1e30
HI = lax.Precision.HIGHEST


def _pick(n, candidates):
    for c in candidates:
        if n % c == 0:
            return c
    raise ValueError(f"no tile for {n} in {candidates}")


def _params(*sem):
    return pltpu.CompilerParams(dimension_semantics=sem, vmem_limit_bytes=VMEM_LIMIT)


def _mm_kernel(x_ref, w_ref, o_ref):
    o_ref[...] = jnp.dot(x_ref[...].astype(BF16), w_ref[...], preferred_element_type=F32).astype(o_ref.dtype)


def _matmul(x, w, out_dtype, *, tm, tn, name):
    m, k = x.shape
    n = w.shape[1]
    return pl.pallas_call(
        _mm_kernel,
        grid=(m // tm, n // tn),
        in_specs=[pl.BlockSpec((tm, k), lambda i, j: (i, 0)),
                  pl.BlockSpec((k, tn), lambda i, j: (0, j))],
        out_specs=pl.BlockSpec((tm, tn), lambda i, j: (i, j)),
        out_shape=jax.ShapeDtypeStruct((m, n), out_dtype),
        compiler_params=_params("parallel", "arbitrary"),
        name=name,
    )(x, w)


NT_DIMS = (((1,), (1,)), ((), ()))


def _mm_ws_kernel(x_ref, wt_ref, o_ref, wb_ref, *, valid):
    @pl.when(pl.program_id(1) == 0)
    def _():
        w = wt_ref[...]
        if valid < w.shape[0]:
            w = jnp.where(lax.broadcasted_iota(jnp.int32, w.shape, 0) < valid, w, 0.0)
        wb_ref[...] = w.astype(BF16)

    o_ref[...] = lax.dot_general(x_ref[...], wb_ref[...], NT_DIMS, preferred_element_type=F32).astype(o_ref.dtype)


def _matmul_ws(x, wt, out_dtype, *, col0, ncols, tm, tn, name):
    m, k = x.shape
    assert col0 % tn == 0 and ncols % tn == 0
    c0 = col0 // tn
    valid = min(tn, wt.shape[0] - (col0 + ncols - tn))
    assert valid == tn or ncols == tn
    return pl.pallas_call(
        functools.partial(_mm_ws_kernel, valid=valid),
        grid=(ncols // tn, m // tm),
        in_specs=[pl.BlockSpec((tm, k), lambda j, i: (i, 0)),
                  pl.BlockSpec((tn, k), lambda j, i: (c0 + j, 0))],
        out_specs=pl.BlockSpec((tm, tn), lambda j, i: (i, j)),
        out_shape=jax.ShapeDtypeStruct((m, ncols), out_dtype),
        scratch_shapes=[pltpu.VMEM((tn, k), BF16)],
        compiler_params=_params("parallel", "arbitrary"),
        name=name,
    )(x, wt)


def _mm_res_ln_kernel(*refs, nk, parts):
    x_refs = refs[:parts]
    w_ref, res_ref, g_ref, b_ref, o32_ref, o16_ref, acc_ref = refs[parts:]
    k = pl.program_id(1)

    def partial_sum(x_ref):
        return jnp.dot(x_ref[...], w_ref[...], preferred_element_type=F32)

    @pl.when(k == 0)
    def _():
        acc_ref[...] = partial_sum(x_refs[0])

    if parts == 1:
        @pl.when(k > 0)
        def _():
            acc_ref[...] += partial_sum(x_refs[0])
    else:
        for p in range(1, parts):
            @pl.when(k == p)
            def _(p=p):
                acc_ref[...] += partial_sum(x_refs[p])

    @pl.when(k == nk - 1)
    def _():
        y = DN_ALPHA * res_ref[...] + acc_ref[...]
        mu = jnp.mean(y, axis=-1, keepdims=True)
        yc = y - mu
        var = jnp.mean(yc * yc, axis=-1, keepdims=True)
        out = yc * lax.rsqrt(var + LN_EPS) * g_ref[...] + b_ref[...]
        o32_ref[...] = out
        o16_ref[...] = out.astype(BF16)


def _mm_res_ln(x, w, res, g, b, *, tm, tk, name):
    xs = x if isinstance(x, tuple) else (x,)
    m = xs[0].shape[0]
    kk, d = w.shape
    nk = kk // tk
    if len(xs) > 1:
        assert len(xs) == nk and all(p.shape == (m, tk) for p in xs)
        x_specs = [pl.BlockSpec((tm, tk), lambda i, k: (i, 0)) for _ in xs]
    else:
        x_specs = [pl.BlockSpec((tm, tk), lambda i, k: (i, k))]
    return pl.pallas_call(
        functools.partial(_mm_res_ln_kernel, nk=nk, parts=len(xs)),
        grid=(m // tm, nk),
        in_specs=x_specs + [
                  pl.BlockSpec((tk, d), lambda i, k: (k, 0)),
                  pl.BlockSpec((tm, d), lambda i, k: (i, 0)),
                  pl.BlockSpec((1, d), lambda i, k: (0, 0)),
                  pl.BlockSpec((1, d), lambda i, k: (0, 0))],
        out_specs=[pl.BlockSpec((tm, d), lambda i, k: (i, 0)),
                   pl.BlockSpec((tm, d), lambda i, k: (i, 0))],
        out_shape=[jax.ShapeDtypeStruct((m, d), F32), jax.ShapeDtypeStruct((m, d), BF16)],
        scratch_shapes=[pltpu.VMEM((tm, d), F32)],
        compiler_params=_params("parallel", "arbitrary"),
        name=name,
    )(*xs, w, res, g.reshape(1, d), b.reshape(1, d))


def _ffn_up_kernel(x_ref, wg_ref, wu_ref, o_ref, wgb_ref, wub_ref):
    @pl.when(pl.program_id(1) == 0)
    def _():
        wgb_ref[...] = wg_ref[...].astype(BF16)
        wub_ref[...] = wu_ref[...].astype(BF16)

    x = x_ref[...]
    g = jnp.dot(x, wgb_ref[...], preferred_element_type=F32)
    u = jnp.dot(x, wub_ref[...], preferred_element_type=F32)
    o_ref[...] = (g * jax.nn.sigmoid(g) * u).astype(o_ref.dtype)


def _ffn_up(x, wg, wu, *, tm, tf, name):
    m, d = x.shape
    f = wg.shape[1]
    return pl.pallas_call(
        _ffn_up_kernel,
        grid=(f // tf, m // tm),
        in_specs=[pl.BlockSpec((tm, d), lambda j, i: (i, 0)),
                  pl.BlockSpec((d, tf), lambda j, i: (0, j)),
                  pl.BlockSpec((d, tf), lambda j, i: (0, j))],
        out_specs=pl.BlockSpec((tm, tf), lambda j, i: (i, j)),
        out_shape=jax.ShapeDtypeStruct((m, f), BF16),
        scratch_shapes=[pltpu.VMEM((d, tf), BF16), pltpu.VMEM((d, tf), BF16)],
        compiler_params=_params("parallel", "arbitrary"),
        name=name,
    )(x, wg, wu)


def _ffn_block(h32, h16, w_gate, w_up, w_down, g, b, tag):
    m = h16.shape[0]
    f = w_gate.shape[1]
    hid = _ffn_up(h16, w_gate, w_up,
                  tm=_pick(m, (1664, 1280, 640, 128)), tf=_pick(f, (512, 256, 128)),
                  name=f"{tag}_ffn_up")
    return _mm_res_ln(hid, w_down.astype(BF16), h32, g, b,
                      tm=_pick(m, (640, 128)), tk=_pick(f, (1408, 512, 256, 128)), name=f"{tag}_ffn_down_ln")


def _s5_tables(log_dt, a_re, a_im, b_re, b_im, c_re, c_im):
    g_, p_ = a_re.shape
    j_ = b_re.shape[-1]
    q = S5_Q
    dt = jnp.exp(log_dt)[:, None]
    lr = dt * a_re
    li = dt * a_im
    ks = jnp.arange(q + 1, dtype=F32)[:, None, None]
    mag = jnp.exp(ks * lr)
    pw_re = mag * jnp.cos(ks * li)
    pw_im = mag * jnp.sin(ks * li)
    ab_re, ab_im = pw_re[1], pw_im[1]
    den = a_re * a_re + a_im * a_im
    nr = ab_re - 1.0
    f_re = (nr * a_re + ab_im * a_im) / den
    f_im = (ab_im * a_re - nr * a_im) / den
    bb_re = f_re[..., None] * b_re - f_im[..., None] * b_im
    bb_im = f_re[..., None] * b_im + f_im[..., None] * b_re
    cp_re = c_re[None] * pw_re[:, :, None, :] - c_im[None] * pw_im[:, :, None, :]
    cp_im = c_re[None] * pw_im[:, :, None, :] + c_im[None] * pw_re[:, :, None, :]
    kern = jnp.sum(cp_re[:q, :, :, :, None] * bb_re[None, :, None, :, :]
                   - cp_im[:q, :, :, :, None] * bb_im[None, :, None, :, :], axis=3)
    kr = (q - 1 - jnp.arange(q)).astype(F32)[:, None, None]
    rev_mag = jnp.exp(kr * lr)
    rev_re = (rev_mag * jnp.cos(kr * li))[:, :, :, None]
    rev_im = (rev_mag * jnp.sin(kr * li))[:, :, :, None]
    st_re = (rev_re * bb_re[None] - rev_im * bb_im[None]).transpose(1, 0, 3, 2)
    st_im = (rev_re * bb_im[None] + rev_im * bb_re[None]).transpose(1, 0, 3, 2)
    wo_re = cp_re[1:].transpose(1, 3, 0, 2)
    wo_im = (-cp_im[1:]).transpose(1, 3, 0, 2)

    gl = S5_LANE_GROUPS
    nblk = g_ // gl

    def block_diag(w, rows_per_group, cols_per_group):
        w = jnp.tile(w, (1,) * (w.ndim - 1) + (gl,))
        row_g = jnp.arange(gl * rows_per_group)[:, None] // rows_per_group
        col_g = jnp.arange(gl * cols_per_group)[None, :] // cols_per_group
        return jnp.where(row_g == col_g, w, 0.0)

    kc = kern.reshape(q, nblk, gl, j_, j_).transpose(1, 0, 2, 4, 3).reshape(nblk, q, gl * j_, j_)
    lag_b = block_diag(kc, j_, j_).astype(BF16)

    def st_pairs(st):
        w = st.reshape(nblk, gl, q, j_, p_).transpose(0, 2, 1, 3, 4).reshape(nblk, q, gl * j_, p_)
        return jnp.concatenate([w, w], axis=3)

    def wo_blocks(wo):
        w = jnp.tile(wo.reshape(nblk, gl * p_, q, j_), (1, 1, 1, gl))
        row_g = jnp.arange(gl * p_)[:, None, None] // p_
        col_g = jnp.arange(gl * j_)[None, None, :] // j_
        return jnp.where(row_g == col_g, w, 0.0).reshape(nblk, gl * p_, q * gl * j_).astype(BF16)

    aq_re = pw_re[q].reshape(1, g_ * p_)
    aq_im = pw_im[q].reshape(1, g_ * p_)
    return lag_b, st_pairs(st_re), st_pairs(st_im), wo_blocks(wo_re), wo_blocks(wo_im), aq_re, aq_im


def _s5_rows(u_ref, front_rows):
    rows = u_ref.shape[0] // S5_Q
    live = lax.broadcasted_iota(jnp.int32, (rows, u_ref.shape[1]), 0) >= front_rows
    pieces = [jnp.where(live, u_ref[pl.ds(i, rows, stride=S5_Q), :], 0.0) for i in range(S5_Q)]
    return pieces, jnp.concatenate([p.astype(BF16) for p in pieces], axis=1)


def _s5_local_kernel(u_ref, sre_ref, sim_ref, zr_ref, zi_ref, w_ref, *, front_rows):
    @pl.when(pl.program_id(1) == 0)
    def _():
        half = w_ref.shape[1] // 2
        per_col = LANES // S5_STATE
        row_g = lax.broadcasted_iota(jnp.int32, (LANES, LANES), 0) // S5_GROUP
        lane_g = lax.broadcasted_iota(jnp.int32, (LANES, LANES), 1) // S5_STATE
        for i in range(S5_Q):
            rows = slice(i * LANES, (i + 1) * LANES)
            for kb in range(half // LANES):
                keep = row_g == lane_g + kb * per_col
                w_ref[rows, kb * LANES:(kb + 1) * LANES] = jnp.where(keep, sre_ref[0, i], 0.0).astype(BF16)
                w_ref[rows, half + kb * LANES:half + (kb + 1) * LANES] = (
                    jnp.where(keep, sim_ref[0, i], 0.0).astype(BF16))

    _, x = _s5_rows(u_ref, front_rows)
    z = jnp.dot(x, w_ref[...], preferred_element_type=F32)
    half = z.shape[1] // 2
    zr_ref[...] = z[:, :half]
    zi_ref[...] = z[:, half:]


def _s5_scan_kernel(zr_ref, zi_ref, ar_ref, ai_ref, sr_ref, si_ref, *, nb, nc):
    ar = ar_ref[...]
    ai = ai_ref[...]

    def body(c, carry):
        new = []
        for b in range(nb):
            s_re, s_im = carry[2 * b], carry[2 * b + 1]
            row = b * nc + c
            sr_ref[pl.ds(row, 1), :] = s_re
            si_ref[pl.ds(row, 1), :] = s_im
            z_re = zr_ref[pl.ds(row, 1), :]
            z_im = zi_ref[pl.ds(row, 1), :]
            new.append(ar * s_re - ai * s_im + z_re)
            new.append(ar * s_im + ai * s_re + z_im)
        return tuple(new)

    zero = jnp.zeros(ar.shape, F32)
    lax.fori_loop(0, nc, body, (zero,) * (2 * nb))


def _s5_out_kernel(u_ref, lag_ref, sr_ref, si_ref, wr_ref, wi_ref, d_ref, y_ref, toe_ref, *, front_rows):
    @pl.when(pl.program_id(1) == 0)
    def _():
        zero = jnp.zeros((LANES, LANES), toe_ref.dtype)
        for i in range(S5_Q):
            for j in range(S5_Q):
                toe_ref[i * LANES:(i + 1) * LANES, j * LANES:(j + 1) * LANES] = lag_ref[0, j - i] if j >= i else zero

    pieces, x = _s5_rows(u_ref, front_rows)
    rows = x.shape[0]
    y = jnp.dot(x, toe_ref[...], preferred_element_type=F32)
    y += jnp.dot(sr_ref[...].astype(BF16), wr_ref[0], preferred_element_type=F32)
    y += jnp.dot(si_ref[...].astype(BF16), wi_ref[0], preferred_element_type=F32)
    d = d_ref[...]
    for j in range(S5_Q):
        y_ref[pl.ds(j, rows, stride=S5_Q), :] = y[:, j * LANES:(j + 1) * LANES] + d * pieces[j]


def _glu_kernel(y_ref, w_ref, o_ref):
    g = jax.nn.gelu(y_ref[...])
    gate = jax.nn.sigmoid(jnp.dot(g.astype(BF16), w_ref[...], preferred_element_type=F32))
    o_ref[...] = (g * gate).astype(o_ref.dtype)


def _s5_mixer(proj, nb, lb, front, log_dt, a_re, a_im, b_re, b_im, c_re, c_im, d, w_glu):
    r = proj.shape[0]
    g_, p_ = a_re.shape
    w_ = g_ * S5_GROUP
    q = S5_Q
    m = r // q
    nc = lb // q
    nblk = w_ // LANES
    sw = S5_LANE_GROUPS * p_
    kw = q * LANES
    front_rows = front // q
    lag_b, st_re_c, st_im_c, wo_re_b, wo_im_b, aq_re, aq_im = _s5_tables(
        log_dt, a_re, a_im, b_re, b_im, c_re, c_im)

    z_re, z_im = pl.pallas_call(
        functools.partial(_s5_local_kernel, front_rows=front_rows),
        grid=(nblk, nb),
        in_specs=[pl.BlockSpec((lb, LANES), lambda c, b: (b, c)),
                  pl.BlockSpec((1, q, LANES, LANES), lambda c, b: (c, 0, 0, 0)),
                  pl.BlockSpec((1, q, LANES, LANES), lambda c, b: (c, 0, 0, 0))],
        out_specs=[pl.BlockSpec((nc, sw), lambda c, b: (b, c)),
                   pl.BlockSpec((nc, sw), lambda c, b: (b, c))],
        out_shape=[jax.ShapeDtypeStruct((m, g_ * p_), F32)] * 2,
        scratch_shapes=[pltpu.VMEM((kw, 2 * sw), BF16)],
        compiler_params=_params("parallel", "arbitrary"),
        name="s5_local",
    )(proj, st_re_c, st_im_c)

    cw = _pick(g_ * p_, (512, 256, 128))
    s_re, s_im = pl.pallas_call(
        functools.partial(_s5_scan_kernel, nb=nb, nc=nc),
        grid=(g_ * p_ // cw,),
        in_specs=[pl.BlockSpec((m, cw), lambda i: (0, i)),
                  pl.BlockSpec((m, cw), lambda i: (0, i)),
                  pl.BlockSpec((1, cw), lambda i: (0, i)),
                  pl.BlockSpec((1, cw), lambda i: (0, i))],
        out_specs=[pl.BlockSpec((m, cw), lambda i: (0, i)),
                   pl.BlockSpec((m, cw), lambda i: (0, i))],
        out_shape=[jax.ShapeDtypeStruct((m, g_ * p_), F32)] * 2,
        compiler_params=_params("parallel"),
        name="s5_scan",
    )(z_re, z_im, aq_re, aq_im)

    y = pl.pallas_call(
        functools.partial(_s5_out_kernel, front_rows=front_rows),
        grid=(nblk, nb),
        in_specs=[pl.BlockSpec((lb, LANES), lambda c, b: (b, c)),
                  pl.BlockSpec((1, q, LANES, LANES), lambda c, b: (c, 0, 0, 0)),
                  pl.BlockSpec((nc, sw), lambda c, b: (b, c)),
                  pl.BlockSpec((nc, sw), lambda c, b: (b, c)),
                  pl.BlockSpec((1, sw, kw), lambda c, b: (c, 0, 0)),
                  pl.BlockSpec((1, sw, kw), lambda c, b: (c, 0, 0)),
                  pl.BlockSpec((1, LANES), lambda c, b: (0, c))],
        out_specs=pl.BlockSpec((lb, LANES), lambda c, b: (b, c)),
        out_shape=jax.ShapeDtypeStruct((r, w_), F32),
        scratch_shapes=[pltpu.VMEM((kw, kw), BF16)],
        compiler_params=_params("parallel", "arbitrary"),
        name="s5_out",
    )(proj, lag_b, s_re, s_im, wo_re_b, wo_im_b, d.reshape(1, w_))

    tm = _pick(r, (832, 640, 128))
    return pl.pallas_call(
        _glu_kernel,
        grid=(r // tm,),
        in_specs=[pl.BlockSpec((tm, w_), lambda i: (i, 0)),
                  pl.BlockSpec((w_, w_), lambda i: (0, 0))],
        out_specs=pl.BlockSpec((tm, w_), lambda i: (i, 0)),
        out_shape=jax.ShapeDtypeStruct((r, w_), BF16),
        compiler_params=_params("parallel"),
        name="s5_glu",
    )(y, w_glu.astype(BF16))


def _rms(x, g):
    return x * lax.rsqrt(jnp.mean(x * x, axis=-1, keepdims=True) + RMS_EPS) * g


def _mla_prep_kernel(lat_ref, cc_ref, ss_ref, qg_ref, kg_ref, wqn_ref, wqr_ref, wqs_ref, wkn_ref, wvt_ref,
                     q_ref, k_ref, vt_ref, *, scale, tiles_per_batch, front):
    tm = lat_ref.shape[0]
    lat = lat_ref[...]
    o1 = MLA_Q_RANK
    o2 = o1 + MLA_KV_RANK
    o3 = o2 + LANES
    qn = _rms(lat[:, :o1], qg_ref[...]).astype(BF16)
    kvn = _rms(lat[:, o1:o2], kg_ref[...]).astype(BF16)
    cc = cc_ref[...]
    ss = ss_ref[...]
    lane = lax.broadcasted_iota(jnp.int32, (tm, LANES), 1)
    row = (pl.program_id(0) % tiles_per_batch) * tm + lax.broadcasted_iota(jnp.int32, (tm, LANES), 0)
    bias_lane = lane == MLA_ROPE
    k_rope = jnp.where(bias_lane & (row < front), MASK_VALUE, lat[:, o2:o3] * cc + lat[:, o3:] * ss).astype(BF16)
    q_nope = jnp.dot(qn, wqn_ref[...], preferred_element_type=F32) * scale
    q_r = jnp.dot(qn, wqr_ref[...], preferred_element_type=F32)
    q_s = jnp.dot(qn, wqs_ref[...], preferred_element_type=F32)
    k_nope = jnp.dot(kvn, wkn_ref[...], preferred_element_type=F32)
    v_t = lax.dot_general(wvt_ref[...], kvn, (((1,), (1,)), ((), ())), preferred_element_type=F32)
    for h in range(MLA_HEADS):
        sl = slice(h * LANES, (h + 1) * LANES)
        q_ref[h, :, 0:LANES] = q_nope[:, sl].astype(BF16)
        q_rope = jnp.where(bias_lane, 1.0, (q_r[:, sl] * cc + q_s[:, sl] * ss) * scale)
        q_ref[h, :, LANES:2 * LANES] = q_rope.astype(BF16)
        k_ref[h, :, 0:LANES] = k_nope[:, sl].astype(BF16)
        k_ref[h, :, LANES:2 * LANES] = k_rope
        vt_ref[h, 0] = v_t[h * MLA_V:(h + 1) * MLA_V, :].astype(BF16)


def _attn_kernel(q_ref, k_ref, vt_ref, o_ref, sa_ref, sb_ref, m_ref, l_ref, acc_ref, *, tq):
    i = pl.program_id(2)
    heads = q_ref.shape[0]
    m_ref[...] = jnp.full(m_ref.shape, MASK_VALUE, F32)
    l_ref[...] = jnp.zeros(l_ref.shape, F32)
    acc_ref[...] = jnp.zeros(acc_ref.shape, F32)
    shift = CHUNK.bit_length() - 1

    def scores(s_ref, j):
        for h in range(heads):
            k = k_ref[h, pl.ds(pl.multiple_of(j * tq, tq), tq), :]
            s_ref[h] = lax.dot_general(k, q_ref[h], (((1,), (1,)), ((), ())), preferred_element_type=F32)

    def absorb(s_ref, j, diagonal):
        for h in range(heads):
            s = s_ref[h]
            if diagonal:
                k_chunk = lax.broadcasted_iota(jnp.int32, (tq, tq), 0) >> shift
                q_chunk = lax.broadcasted_iota(jnp.int32, (tq, tq), 1) >> shift
                s = jnp.where(k_chunk <= q_chunk, s, MASK_VALUE)
            m_prev = m_ref[h]
            m_new = jnp.maximum(m_prev, jnp.max(s, axis=0, keepdims=True))
            alpha = jnp.exp2(m_prev - m_new)
            p = jnp.exp2(s - m_new)
            l_ref[h] = alpha * l_ref[h] + jnp.sum(p, axis=0, keepdims=True)
            acc_ref[h] = alpha * acc_ref[h] + jnp.dot(vt_ref[h, j], p.astype(BF16), preferred_element_type=F32)
            m_ref[h] = m_new

    scores(sa_ref, 0)

    def body(t, carry):
        j = 2 * t
        scores(sb_ref, j + 1)
        absorb(sa_ref, j, False)
        scores(sa_ref, j + 2)
        absorb(sb_ref, j + 1, False)
        return carry

    lax.fori_loop(0, i // 2, body, 0)

    @pl.when(i % 2 == 0)
    def _():
        absorb(sa_ref, i, True)

    @pl.when(i % 2 == 1)
    def _():
        scores(sb_ref, i)
        absorb(sa_ref, i - 1, False)
        absorb(sb_ref, i, True)

    for h in range(heads):
        o_ref[:, h * MLA_V:(h + 1) * MLA_V] = (acc_ref[h] / l_ref[h]).T.astype(o_ref.dtype)


def _rope_tables(lb, front):
    pos = (jnp.arange(lb) - front).astype(F32)
    inv = ROPE_BASE ** (-jnp.arange(0, MLA_ROPE, 2, dtype=F32) / MLA_ROPE)
    ang = pos[:, None] * inv[None, :]
    cos, sin = jnp.cos(ang), jnp.sin(ang)
    zeros = jnp.zeros((lb, LANES - MLA_ROPE), F32)
    cc = jnp.concatenate([cos, cos, zeros], axis=1)
    ss = jnp.concatenate([-sin, sin, zeros], axis=1)
    return cc, ss


def _pad_heads(w, width, take):
    k = w.shape[0]
    w = w.reshape(k, MLA_HEADS, width)[:, :, take]
    pad = LANES - w.shape[-1]
    if pad:
        w = jnp.concatenate([w, jnp.zeros((k, MLA_HEADS, pad), w.dtype)], axis=-1)
    return w.reshape(k, MLA_HEADS * LANES).astype(BF16)


def _mla_mixer(proj, nb, lb, front, q_norm, w_uq, kv_norm, w_ukv):
    r = proj.shape[0]
    half = MLA_ROPE // 2
    qw = MLA_NOPE + MLA_ROPE
    rope_cols = jnp.arange(MLA_NOPE, qw)
    swap_cols = jnp.concatenate([rope_cols[half:], rope_cols[:half]])
    wqn = _pad_heads(w_uq, qw, jnp.arange(MLA_NOPE))
    wqr = _pad_heads(w_uq, qw, rope_cols)
    wqs = _pad_heads(w_uq, qw, swap_cols)
    wkn = _pad_heads(w_ukv, MLA_NOPE + MLA_V, jnp.arange(MLA_NOPE))
    wvt = _pad_heads(w_ukv, MLA_NOPE + MLA_V, jnp.arange(MLA_NOPE, MLA_NOPE + MLA_V)).T
    cc, ss = _rope_tables(lb, front)
    scale = qw ** -0.5 * math.log2(math.e)

    tq = _pick(lb, (640, 128))
    nq = lb // tq
    hl = MLA_HEADS * LANES
    lat_w = MLA_Q_RANK + MLA_KV_RANK + 2 * LANES
    const = lambda i: (0, 0)
    q, k, vt = pl.pallas_call(
        functools.partial(_mla_prep_kernel, scale=scale, tiles_per_batch=nq, front=front),
        grid=(r // tq,),
        in_specs=[pl.BlockSpec((tq, lat_w), lambda i: (i, 1)),
                  pl.BlockSpec((tq, LANES), lambda i: (i % nq, 0)),
                  pl.BlockSpec((tq, LANES), lambda i: (i % nq, 0)),
                  pl.BlockSpec((1, MLA_Q_RANK), const),
                  pl.BlockSpec((1, MLA_KV_RANK), const),
                  pl.BlockSpec((MLA_Q_RANK, hl), const),
                  pl.BlockSpec((MLA_Q_RANK, hl), const),
                  pl.BlockSpec((MLA_Q_RANK, hl), const),
                  pl.BlockSpec((MLA_KV_RANK, hl), const),
                  pl.BlockSpec((hl, MLA_KV_RANK), const)],
        out_specs=[pl.BlockSpec((MLA_HEADS, tq, 2 * LANES), lambda i: (0, i, 0)),
                   pl.BlockSpec((MLA_HEADS, tq, 2 * LANES), lambda i: (0, i, 0)),
                   pl.BlockSpec((MLA_HEADS, 1, MLA_V, tq), lambda i: (0, i, 0, 0))],
        out_shape=[jax.ShapeDtypeStruct((MLA_HEADS, r, 2 * LANES), BF16),
                   jax.ShapeDtypeStruct((MLA_HEADS, r, 2 * LANES), BF16),
                   jax.ShapeDtypeStruct((MLA_HEADS, r // tq, MLA_V, tq), BF16)],
        compiler_params=_params("parallel"),
        name="mla_prep",
    )(proj, cc, ss, q_norm.reshape(1, -1), kv_norm.reshape(1, -1), wqn, wqr, wqs, wkn, wvt)

    hs = 2
    return pl.pallas_call(
        functools.partial(_attn_kernel, tq=tq),
        grid=(nb, MLA_HEADS // hs, nq),
        in_specs=[pl.BlockSpec((hs, tq, 2 * LANES), lambda b, h, i: (h, b * nq + i, 0)),
                  pl.BlockSpec((hs, lb, 2 * LANES), lambda b, h, i: (h, b, 0)),
                  pl.BlockSpec((hs, nq, MLA_V, tq), lambda b, h, i: (h, b, 0, 0))],
        out_specs=pl.BlockSpec((tq, hs * MLA_V), lambda b, h, i: (b * nq + i, h)),
        out_shape=jax.ShapeDtypeStruct((r, hl), BF16),
        scratch_shapes=[pltpu.VMEM((hs, tq, tq), F32), pltpu.VMEM((hs, tq, tq), F32),
                        pltpu.VMEM((hs, 1, tq), F32), pltpu.VMEM((hs, 1, tq), F32),
                        pltpu.VMEM((hs, MLA_V, tq), F32)],
        compiler_params=_params("parallel", "parallel", "arbitrary"),
        name="mla_attention",
    )(q, k, vt)


def _proj_conv_kernel(x_ref, xh_ref, w_ref, cw_ref, cb_ref, o_ref, wb_ref, pre_ref, *, tm, lb, front):
    tile = pl.program_id(1)

    @pl.when(tile == 0)
    def _():
        wb_ref[...] = w_ref[...].astype(BF16)

    pre_ref[...] = lax.dot_general(x_ref[...], wb_ref[...], NT_DIMS, preferred_element_type=F32)
    pre = pre_ref[...]
    pre_halo = lax.dot_general(xh_ref[...], wb_ref[...], NT_DIMS, preferred_element_type=F32)
    w = cw_ref[...]
    bias = cb_ref[...]
    k = w.shape[0]

    def taps(x, halo):
        acc = bias + w[k - 1:k, :] * x
        head = bias + w[k - 1:k, :] * x[:SUBLANES]
        sub = lax.broadcasted_iota(jnp.int32, halo.shape, 0)
        for d in range(1, k):
            wd = w[k - 1 - d:k - d, :]
            acc += wd * pltpu.roll(x, d, 0)
            head += wd * jnp.where(sub < d, pltpu.roll(halo, d, 0), pltpu.roll(x[:SUBLANES], d, 0))
        return acc, head

    def silu(v):
        return (v * jax.nn.sigmoid(v)).astype(o_ref.dtype)

    acc, head = taps(pre, pre_halo)
    o_ref[...] = silu(acc)
    o_ref[0:SUBLANES, :] = silu(head)

    hb = -(-(front + k) // (2 * SUBLANES)) * (2 * SUBLANES)
    @pl.when((tile * tm) % lb == 0)
    def _():
        xh = pre_ref[0:hb, :]
        xh = jnp.where(lax.broadcasted_iota(jnp.int32, xh.shape, 0) < front, 0.0, xh)
        acc_h, head_h = taps(xh, jnp.zeros((SUBLANES, xh.shape[1]), F32))
        o_ref[0:hb, :] = silu(acc_h)
        o_ref[0:SUBLANES, :] = silu(head_h)


def _split3(x):
    hi = x.astype(BF16).astype(F32)
    mid = (x - hi).astype(BF16).astype(F32)
    lo = (x - hi - mid).astype(BF16).astype(F32)
    return hi, mid, lo


def _ssd_kernel(xs_ref, bm_ref, cm_ref, dtc_ref, dtr_ref, bc_ref, br_ref, ac_ref, ar_ref,
                d_ref, z_ref, ng_ref, ex_ref, o_ref, state_ref, xdt_ref, *, front):
    c = pl.program_id(1)
    q = xs_ref.shape[0]
    groups, nh = dtr_ref.shape[0], dtr_ref.shape[1]
    hp = xs_ref.shape[1] // groups
    ns = bm_ref.shape[1] // groups
    hd = hp // nh
    wide = dtc_ref.shape[1]

    @pl.when(c == 0)
    def _():
        state_ref[...] = jnp.zeros(state_ref.shape, F32)

    first_live = jnp.where(c == 0, front, 0)
    live_c = lax.broadcasted_iota(jnp.int32, (q, wide), 0) >= first_live
    live_r = lax.broadcasted_iota(jnp.int32, (nh, q), 1) >= first_live
    li = lax.broadcasted_iota(jnp.int32, (q, q), 0)
    si = lax.broadcasted_iota(jnp.int32, (q, q), 1)
    lower = li >= si
    upper16 = (li <= si).astype(BF16)
    lane = lax.broadcasted_iota(jnp.int32, (q, 2 * hd), 1)

    def terms(x, axis):
        return jnp.concatenate(_split3(x), axis=axis).astype(BF16)

    dt_c = jnp.where(live_c, jax.nn.softplus(dtc_ref[...] + bc_ref[0]), 0.0)
    dt_terms = terms(dt_c, 1)
    cs3 = jnp.dot(lower.astype(BF16), terms(dt_c * ac_ref[0], 1), preferred_element_type=F32)
    cs_terms = terms(cs3[:, :wide] + cs3[:, wide:2 * wide] + cs3[:, 2 * wide:], 1)

    for g in range(groups):
        cols = slice(g * hp, (g + 1) * hp)
        expand3 = ex_ref[g]
        dt_full = jnp.dot(dt_terms, expand3, preferred_element_type=F32)
        cs_full = jnp.dot(cs_terms, expand3, preferred_element_type=F32)
        dt_r = jnp.where(live_r, jax.nn.softplus(dtr_ref[g] + br_ref[g]), 0.0)
        csr3 = jnp.dot(terms(dt_r * ar_ref[g], 0), upper16, preferred_element_type=F32)
        cs_r = csr3[:nh] + csr3[nh:2 * nh] + csr3[2 * nh:]
        last = cs_full[q - 1:q, :]

        xdt_ref[g] = xs_ref[:, cols].astype(F32) * dt_full
        bm = bm_ref[:, g * ns:(g + 1) * ns]
        cm = cm_ref[:, g * ns:(g + 1) * ns]
        cb = lax.dot_general(cm, bm, (((1,), (1,)), ((), ())), preferred_element_type=F32)

        prev = state_ref[g]
        y = jnp.dot(cm, prev.astype(BF16), preferred_element_type=F32) * jnp.exp2(cs_full)
        local = lax.dot_general(bm, (xdt_ref[g] * jnp.exp2(last - cs_full)).astype(BF16),
                                (((0,), (0,)), ((), ())), preferred_element_type=F32)
        state_ref[g] = prev * jnp.exp2(last) + local

        diag = []
        for pr in range(nh // 2):
            ms = []
            for h in (2 * pr, 2 * pr + 1):
                seg = cs_full[:, h * hd:h * hd + 1] - cs_r[h:h + 1, :]
                ms.append((cb * jnp.exp2(jnp.where(lower, seg, -jnp.inf))).astype(BF16))
            xp = xdt_ref[g, :, pr * 2 * hd:(pr + 1) * 2 * hd]
            rhs = jnp.concatenate([jnp.where(lane < hd, xp, 0.0), jnp.where(lane >= hd, xp, 0.0)], axis=0)
            diag.append(jnp.dot(jnp.concatenate(ms, axis=1), rhs.astype(BF16), preferred_element_type=F32))
        y = y + jnp.concatenate(diag, axis=1) + d_ref[g] * xs_ref[:, cols].astype(F32)

        z = z_ref[:, cols].astype(F32)
        y = y * (z * jax.nn.sigmoid(z))
        y = y * lax.rsqrt(jnp.mean(y * y, axis=-1, keepdims=True) + RMS_EPS) * ng_ref[:, cols]
        o_ref[:, cols] = y.astype(o_ref.dtype)


def _mamba2_mixer(h16, nb, lb, front, w_in, conv_w, conv_b, dt_bias, a_log, d, norm_g):
    r = h16.shape[0]
    inner = norm_g.shape[0]
    gn = SSD_GROUPS * SSD_STATE
    cdim = inner + 2 * gn
    tm = _pick(r, (1664, 1280, 640, 128))
    nheads = SSD_GROUPS * SSD_HPG
    tn = _pick(math.gcd(inner, cdim), (1024, 512, 128))
    wt = w_in.T
    z = _matmul_ws(h16, wt, BF16, col0=0, ncols=inner, tm=tm, tn=tn, name="l1_in_z")
    tcv = _pick(math.gcd(inner, cdim), (512, 256, 128))
    blocks = tm // SUBLANES
    dm = h16.shape[1]
    c0 = inner // tcv
    assert lb % tm == 0
    act = pl.pallas_call(
        functools.partial(_proj_conv_kernel, tm=tm, lb=lb, front=front),
        grid=(cdim // tcv, r // tm),
        in_specs=[pl.BlockSpec((tm, dm), lambda j, i: (i, 0)),
                  pl.BlockSpec((SUBLANES, dm), lambda j, i: (jnp.maximum(i * blocks - 1, 0), 0)),
                  pl.BlockSpec((tcv, dm), lambda j, i: (c0 + j, 0)),
                  pl.BlockSpec((SSD_CONV, tcv), lambda j, i: (0, j)),
                  pl.BlockSpec((1, tcv), lambda j, i: (0, j))],
        out_specs=pl.BlockSpec((tm, tcv), lambda j, i: (i, j)),
        out_shape=jax.ShapeDtypeStruct((r, cdim), BF16),
        scratch_shapes=[pltpu.VMEM((tcv, dm), BF16), pltpu.VMEM((tm, tcv), F32)],
        compiler_params=_params("parallel", "arbitrary"),
        name="l1_in_xbc_conv",
    )(h16, h16, wt, conv_w, conv_b.reshape(1, cdim))
    assert w_in.shape[1] == inner + cdim + nheads and nheads <= LANES
    dt_raw = _matmul_ws(h16, wt, F32, col0=inner + cdim, ncols=LANES, tm=tm, tn=LANES, name="l1_in_dt")

    gsz = inner // SSD_GROUPS
    nc = lb // SSD_BLOCK
    dt_r = dt_raw[:, :nheads].T.reshape(SSD_GROUPS, SSD_HPG, r)
    lane_pad = jnp.zeros((LANES - nheads,), F32)
    bias = dt_bias.reshape(SSD_GROUPS, SSD_HPG)
    bias_c = jnp.concatenate([dt_bias, lane_pad]).reshape(1, 1, LANES)
    a = -jnp.exp(a_log) * math.log2(math.e)
    a_c = jnp.concatenate([a, lane_pad]).reshape(1, 1, LANES)
    a = a.reshape(SSD_GROUPS, SSD_HPG)
    d_full = jnp.repeat(d, SSD_HEAD_DIM).reshape(SSD_GROUPS, 1, gsz)
    head_of_lane = (jnp.arange(gsz)[None, None, :] // SSD_HEAD_DIM
                    + SSD_HPG * jnp.arange(SSD_GROUPS)[:, None, None])
    expand = (jnp.arange(LANES)[None, :, None] == head_of_lane).astype(BF16)
    expand3 = jnp.concatenate([expand] * 3, axis=1)
    row = lambda b, c: b * nc + c
    whole3 = lambda b, c: (0, 0, 0)
    ng_ = SSD_GROUPS
    return pl.pallas_call(
        functools.partial(_ssd_kernel, front=front),
        grid=(nb, nc),
        in_specs=[pl.BlockSpec((SSD_BLOCK, inner), lambda b, c: (row(b, c), 0)),
                  pl.BlockSpec((SSD_BLOCK, gn), lambda b, c: (row(b, c), inner // gn)),
                  pl.BlockSpec((SSD_BLOCK, gn), lambda b, c: (row(b, c), inner // gn + 1)),
                  pl.BlockSpec((SSD_BLOCK, LANES), lambda b, c: (row(b, c), 0)),
                  pl.BlockSpec((ng_, SSD_HPG, SSD_BLOCK), lambda b, c: (0, 0, row(b, c))),
                  pl.BlockSpec((1, 1, LANES), whole3),
                  pl.BlockSpec((ng_, SSD_HPG, 1), whole3),
                  pl.BlockSpec((1, 1, LANES), whole3),
                  pl.BlockSpec((ng_, SSD_HPG, 1), whole3),
                  pl.BlockSpec((ng_, 1, gsz), whole3),
                  pl.BlockSpec((SSD_BLOCK, inner), lambda b, c: (row(b, c), 0)),
                  pl.BlockSpec((1, inner), lambda b, c: (0, 0)),
                  pl.BlockSpec((ng_, 3 * LANES, gsz), whole3)],
        out_specs=pl.BlockSpec((SSD_BLOCK, inner), lambda b, c: (row(b, c), 0)),
        out_shape=jax.ShapeDtypeStruct((r, inner), BF16),
        scratch_shapes=[pltpu.VMEM((ng_, SSD_STATE, gsz), F32), pltpu.VMEM((ng_, SSD_BLOCK, gsz), F32)],
        compiler_params=_params("parallel", "arbitrary"),
        name="ssd_scan",
    )(act, act, act, dt_raw, dt_r,
      bias_c, bias[:, :, None], a_c, a[:, :, None],
      d_full, z, norm_g.reshape(1, inner), expand3)


def kernel(x, meta_tokens, l0_w_in, l0_s5_log_dt, l0_s5_a_re, l0_s5_a_im, l0_s5_b_re, l0_s5_b_im, l0_s5_c_re, l0_s5_c_im, l0_s5_d, l0_s5_w_glu, l0_mla_q_norm, l0_mla_w_uq, l0_mla_kv_norm, l0_mla_w_ukv, l0_w_out, l0_ln1_g, l0_ln1_b, l0_ffn_w_gate, l0_ffn_w_up, l0_ffn_w_down, l0_ln2_g, l0_ln2_b, l1_w_in, l1_conv_w, l1_conv_b, l1_dt_bias, l1_a_log, l1_d, l1_norm_g, l1_w_out, l1_ln1_g, l1_ln1_b, l1_ffn_w_gate, l1_ffn_w_up, l1_ffn_w_down, l1_ln2_g, l1_ln2_b):
    nb, seq, dm = x.shape
    n_meta = meta_tokens.shape[0]
    front = (-n_meta) % SSD_BLOCK
    lb = front + n_meta + seq
    r = nb * lb
    meta = jnp.broadcast_to(meta_tokens[None].astype(x.dtype), (nb, n_meta, dm))
    h32 = jnp.concatenate([jnp.zeros((nb, front, dm), x.dtype), meta, x], axis=1).reshape(r, dm)

    s5w = l0_s5_d.shape[0]
    o3 = s5w + MLA_Q_RANK + MLA_KV_RANK
    kr = l0_w_in[:, o3:]
    half = MLA_ROPE // 2
    zpad = jnp.zeros((dm, LANES - MLA_ROPE), l0_w_in.dtype)
    w0 = jnp.concatenate([l0_w_in[:, :o3], kr, zpad, kr[:, half:], kr[:, :half], zpad], axis=1).astype(BF16)
    proj = _matmul(h32, w0, F32, tm=_pick(r, (832, 640, 128)), tn=_pick(w0.shape[1], (1024, 512, 128)),
                   name="l0_in")
    a_out = _s5_mixer(proj, nb, lb, front, l0_s5_log_dt, l0_s5_a_re, l0_s5_a_im, l0_s5_b_re, l0_s5_b_im,
                      l0_s5_c_re, l0_s5_c_im, l0_s5_d, l0_s5_w_glu)
    b_out = _mla_mixer(proj, nb, lb, front, l0_mla_q_norm, l0_mla_w_uq, l0_mla_kv_norm, l0_mla_w_ukv)
    tml = _pick(r, (640, 128))
    h32, h16 = _mm_res_ln((a_out, b_out), l0_w_out.astype(BF16), h32, l0_ln1_g, l0_ln1_b, tm=tml,
                          tk=a_out.shape[1], name="l0_out_ln")
    h32, h16 = _ffn_block(h32, h16, l0_ffn_w_gate, l0_ffn_w_up, l0_ffn_w_down, l0_ln2_g, l0_ln2_b, "l0")

    y = _mamba2_mixer(h16, nb, lb, front, l1_w_in, l1_conv_w, l1_conv_b, l1_dt_bias, l1_a_log, l1_d, l1_norm_g)
    h32, h16 = _mm_res_ln(y, l1_w_out.astype(BF16), h32, l1_ln1_g, l1_ln1_b, tm=tml, tk=1024, name="l1_out_ln")
    h32, h16 = _ffn_block(h32, h16, l1_ffn_w_gate, l1_ffn_w_up, l1_ffn_w_down, l1_ln2_g, l1_ln2_b, "l1")
    return h32.reshape(nb, lb, dm)[:, front + n_meta:, :]
```

```python
import functools
import math

import jax
import jax.numpy as jnp
from jax import lax
from jax.experimental import pallas as pl
from jax.experimental.pallas import tpu as pltpu

F32 = jnp.float32
BF16 = jnp.bfloat16

DEPTH = 2
CHUNK = 64
N_META = 16
DN_ALPHA = (2 * DEPTH) ** 0.25
LN_EPS = 1e-5
RMS_EPS = 1e-6
S5_GROUP = 16
S5_STATE = 64
S5_Q = 16
S5_LANE_GROUPS = 128 // S5_GROUP
MLA_HEADS = 8
MLA_NOPE = 128
MLA_ROPE = 64
MLA_V = 128
MLA_Q_RANK = 512
MLA_KV_RANK = 256
ROPE_BASE = 10000.0
SSD_HEAD_DIM = 64
SSD_GROUPS = 8
SSD_HPG = 8
SSD_STATE = 128
SSD_CONV = 4
SSD_BLOCK = 128

LANES = 128
SUBLANES = 8
VMEM_LIMIT = 56 * 1024 * 1024
MASK_VALUE = -1e30
HI = lax.Precision.HIGHEST


def _pick(n, candidates):
    for c in candidates:
        if n % c == 0:
            return c
    raise ValueError(f"no tile for {n} in {candidates}")


def _params(*sem):
    return pltpu.CompilerParams(dimension_semantics=sem, vmem_limit_bytes=VMEM_LIMIT)


def _mm_kernel(x_ref, w_ref, o_ref):
    o_ref[...] = jnp.dot(x_ref[...].astype(BF16), w_ref[...], preferred_element_type=F32).astype(o_ref.dtype)


def _matmul(x, w, out_dtype, *, tm, tn, name):
    m, k = x.shape
    n = w.shape[1]
    return pl.pallas_call(
        _mm_kernel,
        grid=(m // tm, n // tn),
        in_specs=[pl.BlockSpec((tm, k), lambda i, j: (i, 0)),
                  pl.BlockSpec((k, tn), lambda i, j: (0, j))],
        out_specs=pl.BlockSpec((tm, tn), lambda i, j: (i, j)),
        out_shape=jax.ShapeDtypeStruct((m, n), out_dtype),
        compiler_params=_params("parallel", "arbitrary"),
        name=name,
    )(x, w)


NT_DIMS = (((1,), (1,)), ((), ()))


def _mm_ws_kernel(x_ref, wt_ref, o_ref, wb_ref, *, valid):
    @pl.when(pl.program_id(1) == 0)
    def _():
        w = wt_ref[...]
        if valid < w.shape[0]:
            w = jnp.where(lax.broadcasted_iota(jnp.int32, w.shape, 0) < valid, w, 0.0)
        wb_ref[...] = w.astype(BF16)

    o_ref[...] = lax.dot_general(x_ref[...], wb_ref[...], NT_DIMS, preferred_element_type=F32).astype(o_ref.dtype)


def _matmul_ws(x, wt, out_dtype, *, col0, ncols, tm, tn, name):
    m, k = x.shape
    assert col0 % tn == 0 and ncols % tn == 0
    c0 = col0 // tn
    valid = min(tn, wt.shape[0] - (col0 + ncols - tn))
    assert valid == tn or ncols == tn
    return pl.pallas_call(
        functools.partial(_mm_ws_kernel, valid=valid),
        grid=(ncols // tn, m // tm),
        in_specs=[pl.BlockSpec((tm, k), lambda j, i: (i, 0)),
                  pl.BlockSpec((tn, k), lambda j, i: (c0 + j, 0))],
        out_specs=pl.BlockSpec((tm, tn), lambda j, i: (i, j)),
        out_shape=jax.ShapeDtypeStruct((m, ncols), out_dtype),
        scratch_shapes=[pltpu.VMEM((tn, k), BF16)],
        compiler_params=_params("parallel", "arbitrary"),
        name=name,
    )(x, wt)


def _mm_res_ln_kernel(*refs, nk, parts):
    x_refs = refs[:parts]
    w_ref, res_ref, g_ref, b_ref, o32_ref, o16_ref, acc_ref = refs[parts:]
    k = pl.program_id(1)

    def partial_sum(x_ref):
        return jnp.dot(x_ref[...], w_ref[...], preferred_element_type=F32)

    @pl.when(k == 0)
    def _():
        acc_ref[...] = partial_sum(x_refs[0])

    if parts == 1:
        @pl.when(k > 0)
        def _():
            acc_ref[...] += partial_sum(x_refs[0])
    else:
        for p in range(1, parts):
            @pl.when(k == p)
            def _(p=p):
                acc_ref[...] += partial_sum(x_refs[p])

    @pl.when(k == nk - 1)
    def _():
        y = DN_ALPHA * res_ref[...] + acc_ref[...]
        mu = jnp.mean(y, axis=-1, keepdims=True)
        yc = y - mu
        var = jnp.mean(yc * yc, axis=-1, keepdims=True)
        out = yc * lax.rsqrt(var + LN_EPS) * g_ref[...] + b_ref[...]
        o32_ref[...] = out
        o16_ref[...] = out.astype(BF16)


def _mm_res_ln(x, w, res, g, b, *, tm, tk, name):
    xs = x if isinstance(x, tuple) else (x,)
    m = xs[0].shape[0]
    kk, d = w.shape
    nk = kk // tk
    if len(xs) > 1:
        assert len(xs) == nk and all(p.shape == (m, tk) for p in xs)
        x_specs = [pl.BlockSpec((tm, tk), lambda i, k: (i, 0)) for _ in xs]
    else:
        x_specs = [pl.BlockSpec((tm, tk), lambda i, k: (i, k))]
    return pl.pallas_call(
        functools.partial(_mm_res_ln_kernel, nk=nk, parts=len(xs)),
        grid=(m // tm, nk),
        in_specs=x_specs + [
                  pl.BlockSpec((tk, d), lambda i, k: (k, 0)),
                  pl.BlockSpec((tm, d), lambda i, k: (i, 0)),
                  pl.BlockSpec((1, d), lambda i, k: (0, 0)),
                  pl.BlockSpec((1, d), lambda i, k: (0, 0))],
        out_specs=[pl.BlockSpec((tm, d), lambda i, k: (i, 0)),
                   pl.BlockSpec((tm, d), lambda i, k: (i, 0))],
        out_shape=[jax.ShapeDtypeStruct((m, d), F32), jax.ShapeDtypeStruct((m, d), BF16)],
        scratch_shapes=[pltpu.VMEM((tm, d), F32)],
        compiler_params=_params("parallel", "arbitrary"),
        name=name,
    )(*xs, w, res, g.reshape(1, d), b.reshape(1, d))


def _ffn_up_kernel(x_ref, wg_ref, wu_ref, o_ref, wgb_ref, wub_ref):
    @pl.when(pl.program_id(1) == 0)
    def _():
        wgb_ref[...] = wg_ref[...].astype(BF16)
        wub_ref[...] = wu_ref[...].astype(BF16)

    x = x_ref[...]
    g = jnp.dot(x, wgb_ref[...], preferred_element_type=F32)
    u = jnp.dot(x, wub_ref[...], preferred_element_type=F32)
    o_ref[...] = (g * jax.nn.sigmoid(g) * u).astype(o_ref.dtype)


def _ffn_up(x, wg, wu, *, tm, tf, name):
    m, d = x.shape
    f = wg.shape[1]
    return pl.pallas_call(
        _ffn_up_kernel,
        grid=(f // tf, m // tm),
        in_specs=[pl.BlockSpec((tm, d), lambda j, i: (i, 0)),
                  pl.BlockSpec((d, tf), lambda j, i: (0, j)),
                  pl.BlockSpec((d, tf), lambda j, i: (0, j))],
        out_specs=pl.BlockSpec((tm, tf), lambda j, i: (i, j)),
        out_shape=jax.ShapeDtypeStruct((m, f), BF16),
        scratch_shapes=[pltpu.VMEM((d, tf), BF16), pltpu.VMEM((d, tf), BF16)],
        compiler_params=_params("parallel", "arbitrary"),
        name=name,
    )(x, wg, wu)


def _ffn_block(h32, h16, w_gate, w_up, w_down, g, b, tag):
    m = h16.shape[0]
    f = w_gate.shape[1]
    hid = _ffn_up(h16, w_gate, w_up,
                  tm=_pick(m, (1664, 1280, 640, 128)), tf=_pick(f, (512, 256, 128)),
                  name=f"{tag}_ffn_up")
    return _mm_res_ln(hid, w_down.astype(BF16), h32, g, b,
                      tm=_pick(m, (640, 128)), tk=_pick(f, (1408, 512, 256, 128)), name=f"{tag}_ffn_down_ln")


def _s5_tables(log_dt, a_re, a_im, b_re, b_im, c_re, c_im):
    g_, p_ = a_re.shape
    j_ = b_re.shape[-1]
    q = S5_Q
    dt = jnp.exp(log_dt)[:, None]
    lr = dt * a_re
    li = dt * a_im
    ks = jnp.arange(q + 1, dtype=F32)[:, None, None]
    mag = jnp.exp(ks * lr)
    pw_re = mag * jnp.cos(ks * li)
    pw_im = mag * jnp.sin(ks * li)
    ab_re, ab_im = pw_re[1], pw_im[1]
    den = a_re * a_re + a_im * a_im
    nr = ab_re - 1.0
    f_re = (nr * a_re + ab_im * a_im) / den
    f_im = (ab_im * a_re - nr * a_im) / den
    bb_re = f_re[..., None] * b_re - f_im[..., None] * b_im
    bb_im = f_re[..., None] * b_im + f_im[..., None] * b_re
    cp_re = c_re[None] * pw_re[:, :, None, :] - c_im[None] * pw_im[:, :, None, :]
    cp_im = c_re[None] * pw_im[:, :, None, :] + c_im[None] * pw_re[:, :, None, :]
    kern = jnp.sum(cp_re[:q, :, :, :, None] * bb_re[None, :, None, :, :]
                   - cp_im[:q, :, :, :, None] * bb_im[None, :, None, :, :], axis=3)
    kr = (q - 1 - jnp.arange(q)).astype(F32)[:, None, None]
    rev_mag = jnp.exp(kr * lr)
    rev_re = (rev_mag * jnp.cos(kr * li))[:, :, :, None]
    rev_im = (rev_mag * jnp.sin(kr * li))[:, :, :, None]
    st_re = (rev_re * bb_re[None] - rev_im * bb_im[None]).transpose(1, 0, 3, 2)
    st_im = (rev_re * bb_im[None] + rev_im * bb_re[None]).transpose(1, 0, 3, 2)
    wo_re = cp_re[1:].transpose(1, 3, 0, 2)
    wo_im = (-cp_im[1:]).transpose(1, 3, 0, 2)

    gl = S5_LANE_GROUPS
    nblk = g_ // gl

    def block_diag(w, rows_per_group, cols_per_group):
        w = jnp.tile(w, (1,) * (w.ndim - 1) + (gl,))
        row_g = jnp.arange(gl * rows_per_group)[:, None] // rows_per_group
        col_g = jnp.arange(gl * cols_per_group)[None, :] // cols_per_group
        return jnp.where(row_g == col_g, w, 0.0)

    kc = kern.reshape(q, nblk, gl, j_, j_).transpose(1, 0, 2, 4, 3).reshape(nblk, q, gl * j_, j_)
    lag_b = block_diag(kc, j_, j_).astype(BF16)

    def st_pairs(st):
        w = st.reshape(nblk, gl, q, j_, p_).transpose(0, 2, 1, 3, 4).reshape(nblk, q, gl * j_, p_)
        return jnp.concatenate([w, w], axis=3)

    def wo_blocks(wo):
        w = jnp.tile(wo.reshape(nblk, gl * p_, q, j_), (1, 1, 1, gl))
        row_g = jnp.arange(gl * p_)[:, None, None] // p_
        col_g = jnp.arange(gl * j_)[None, None, :] // j_
        return jnp.where(row_g == col_g, w, 0.0).reshape(nblk, gl * p_, q * gl * j_).astype(BF16)

    aq_re = pw_re[q].reshape(1, g_ * p_)
    aq_im = pw_im[q].reshape(1, g_ * p_)
    return lag_b, st_pairs(st_re), st_pairs(st_im), wo_blocks(wo_re), wo_blocks(wo_im), aq_re, aq_im


def _s5_rows(u_ref, front_rows):
    rows = u_ref.shape[0] // S5_Q
    live = lax.broadcasted_iota(jnp.int32, (rows, u_ref.shape[1]), 0) >= front_rows
    pieces = [jnp.where(live, u_ref[pl.ds(i, rows, stride=S5_Q), :], 0.0) for i in range(S5_Q)]
    return pieces, jnp.concatenate([p.astype(BF16) for p in pieces], axis=1)


def _s5_local_kernel(u_ref, sre_ref, sim_ref, zr_ref, zi_ref, w_ref, *, front_rows):
    @pl.when(pl.program_id(1) == 0)
    def _():
        half = w_ref.shape[1] // 2
        per_col = LANES // S5_STATE
        row_g = lax.broadcasted_iota(jnp.int32, (LANES, LANES), 0) // S5_GROUP
        lane_g = lax.broadcasted_iota(jnp.int32, (LANES, LANES), 1) // S5_STATE
        for i in range(S5_Q):
            rows = slice(i * LANES, (i + 1) * LANES)
            for kb in range(half // LANES):
                keep = row_g == lane_g + kb * per_col
                w_ref[rows, kb * LANES:(kb + 1) * LANES] = jnp.where(keep, sre_ref[0, i], 0.0).astype(BF16)
                w_ref[rows, half + kb * LANES:half + (kb + 1) * LANES] = (
                    jnp.where(keep, sim_ref[0, i], 0.0).astype(BF16))

    _, x = _s5_rows(u_ref, front_rows)
    z = jnp.dot(x, w_ref[...], preferred_element_type=F32)
    half = z.shape[1] // 2
    zr_ref[...] = z[:, :half]
    zi_ref[...] = z[:, half:]


def _s5_scan_kernel(zr_ref, zi_ref, ar_ref, ai_ref, sr_ref, si_ref, *, nb, nc):
    ar = ar_ref[...]
    ai = ai_ref[...]

    def body(c, carry):
        new = []
        for b in range(nb):
            s_re, s_im = carry[2 * b], carry[2 * b + 1]
            row = b * nc + c
            sr_ref[pl.ds(row, 1), :] = s_re
            si_ref[pl.ds(row, 1), :] = s_im
            z_re = zr_ref[pl.ds(row, 1), :]
            z_im = zi_ref[pl.ds(row, 1), :]
            new.append(ar * s_re - ai * s_im + z_re)
            new.append(ar * s_im + ai * s_re + z_im)
        return tuple(new)

    zero = jnp.zeros(ar.shape, F32)
    lax.fori_loop(0, nc, body, (zero,) * (2 * nb))


def _s5_out_kernel(u_ref, lag_ref, sr_ref, si_ref, wr_ref, wi_ref, d_ref, y_ref, toe_ref, *, front_rows):
    @pl.when(pl.program_id(1) == 0)
    def _():
        zero = jnp.zeros((LANES, LANES), toe_ref.dtype)
        for i in range(S5_Q):
            for j in range(S5_Q):
                toe_ref[i * LANES:(i + 1) * LANES, j * LANES:(j + 1) * LANES] = lag_ref[0, j - i] if j >= i else zero

    pieces, x = _s5_rows(u_ref, front_rows)
    rows = x.shape[0]
    half = toe_ref.shape[0] // 2
    y = jnp.concatenate([jnp.dot(x[:, :half], toe_ref[:half, :half], preferred_element_type=F32),
                         jnp.dot(x, toe_ref[:, half:], preferred_element_type=F32)], axis=1)
    y += jnp.dot(sr_ref[...].astype(BF16), wr_ref[0], preferred_element_type=F32)
    y += jnp.dot(si_ref[...].astype(BF16), wi_ref[0], preferred_element_type=F32)
    d = d_ref[...]
    for j in range(S5_Q):
        y_ref[pl.ds(j, rows, stride=S5_Q), :] = y[:, j * LANES:(j + 1) * LANES] + d * pieces[j]


def _glu_kernel(y_ref, w_ref, o_ref):
    g = jax.nn.gelu(y_ref[...])
    gate = jax.nn.sigmoid(jnp.dot(g.astype(BF16), w_ref[...], preferred_element_type=F32))
    o_ref[...] = (g * gate).astype(o_ref.dtype)


def _s5_mixer(proj, nb, lb, front, log_dt, a_re, a_im, b_re, b_im, c_re, c_im, d, w_glu):
    r = proj.shape[0]
    g_, p_ = a_re.shape
    w_ = g_ * S5_GROUP
    q = S5_Q
    m = r // q
    nc = lb // q
    nblk = w_ // LANES
    sw = S5_LANE_GROUPS * p_
    kw = q * LANES
    front_rows = front // q
    lag_b, st_re_c, st_im_c, wo_re_b, wo_im_b, aq_re, aq_im = _s5_tables(
        log_dt, a_re, a_im, b_re, b_im, c_re, c_im)

    z_re, z_im = pl.pallas_call(
        functools.partial(_s5_local_kernel, front_rows=front_rows),
        grid=(nblk, nb),
        in_specs=[pl.BlockSpec((lb, LANES), lambda c, b: (b, c)),
                  pl.BlockSpec((1, q, LANES, LANES), lambda c, b: (c, 0, 0, 0)),
                  pl.BlockSpec((1, q, LANES, LANES), lambda c, b: (c, 0, 0, 0))],
        out_specs=[pl.BlockSpec((nc, sw), lambda c, b: (b, c)),
                   pl.BlockSpec((nc, sw), lambda c, b: (b, c))],
        out_shape=[jax.ShapeDtypeStruct((m, g_ * p_), F32)] * 2,
        scratch_shapes=[pltpu.VMEM((kw, 2 * sw), BF16)],
        compiler_params=_params("parallel", "arbitrary"),
        name="s5_local",
    )(proj, st_re_c, st_im_c)

    cw = _pick(g_ * p_, (512, 256, 128))
    s_re, s_im = pl.pallas_call(
        functools.partial(_s5_scan_kernel, nb=nb, nc=nc),
        grid=(g_ * p_ // cw,),
        in_specs=[pl.BlockSpec((m, cw), lambda i: (0, i)),
                  pl.BlockSpec((m, cw), lambda i: (0, i)),
                  pl.BlockSpec((1, cw), lambda i: (0, i)),
                  pl.BlockSpec((1, cw), lambda i: (0, i))],
        out_specs=[pl.BlockSpec((m, cw), lambda i: (0, i)),
                   pl.BlockSpec((m, cw), lambda i: (0, i))],
        out_shape=[jax.ShapeDtypeStruct((m, g_ * p_), F32)] * 2,
        compiler_params=_params("parallel"),
        name="s5_scan",
    )(z_re, z_im, aq_re, aq_im)

    y = pl.pallas_call(
        functools.partial(_s5_out_kernel, front_rows=front_rows),
        grid=(nblk, nb),
        in_specs=[pl.BlockSpec((lb, LANES), lambda c, b: (b, c)),
                  pl.BlockSpec((1, q, LANES, LANES), lambda c, b: (c, 0, 0, 0)),
                  pl.BlockSpec((nc, sw), lambda c, b: (b, c)),
                  pl.BlockSpec((nc, sw), lambda c, b: (b, c)),
                  pl.BlockSpec((1, sw, kw), lambda c, b: (c, 0, 0)),
                  pl.BlockSpec((1, sw, kw), lambda c, b: (c, 0, 0)),
                  pl.BlockSpec((1, LANES), lambda c, b: (0, c))],
        out_specs=pl.BlockSpec((lb, LANES), lambda c, b: (b, c)),
        out_shape=jax.ShapeDtypeStruct((r, w_), F32),
        scratch_shapes=[pltpu.VMEM((kw, kw), BF16)],
        compiler_params=_params("parallel", "arbitrary"),
        name="s5_out",
    )(proj, lag_b, s_re, s_im, wo_re_b, wo_im_b, d.reshape(1, w_))

    tm = _pick(r, (832, 640, 128))
    return pl.pallas_call(
        _glu_kernel,
        grid=(r // tm,),
        in_specs=[pl.BlockSpec((tm, w_), lambda i: (i, 0)),
                  pl.BlockSpec((w_, w_), lambda i: (0, 0))],
        out_specs=pl.BlockSpec((tm, w_), lambda i: (i, 0)),
        out_shape=jax.ShapeDtypeStruct((r, w_), BF16),
        compiler_params=_params("parallel"),
        name="s5_glu",
    )(y, w_glu.astype(BF16))


def _rms(x, g):
    return x * lax.rsqrt(jnp.mean(x * x, axis=-1, keepdims=True) + RMS_EPS) * g


def _mla_prep_kernel(lat_ref, cc_ref, ss_ref, qg_ref, kg_ref, wqn_ref, wqr_ref, wqs_ref, wkn_ref, wvt_ref,
                     q_ref, k_ref, vt_ref, *, scale, tiles_per_batch, front):
    tm = lat_ref.shape[0]
    lat = lat_ref[...]
    o1 = MLA_Q_RANK
    o2 = o1 + MLA_KV_RANK
    o3 = o2 + LANES
    qn = _rms(lat[:, :o1], qg_ref[...]).astype(BF16)
    kvn = _rms(lat[:, o1:o2], kg_ref[...]).astype(BF16)
    cc = cc_ref[...]
    ss = ss_ref[...]
    lane = lax.broadcasted_iota(jnp.int32, (tm, LANES), 1)
    row = (pl.program_id(0) % tiles_per_batch) * tm + lax.broadcasted_iota(jnp.int32, (tm, LANES), 0)
    bias_lane = lane == MLA_ROPE
    k_rope = jnp.where(bias_lane & (row < front), MASK_VALUE, lat[:, o2:o3] * cc + lat[:, o3:] * ss).astype(BF16)
    q_nope = jnp.dot(qn, wqn_ref[...], preferred_element_type=F32) * scale
    q_r = jnp.dot(qn, wqr_ref[...], preferred_element_type=F32)
    q_s = jnp.dot(qn, wqs_ref[...], preferred_element_type=F32)
    k_nope = jnp.dot(kvn, wkn_ref[...], preferred_element_type=F32)
    v_t = lax.dot_general(wvt_ref[...], kvn, (((1,), (1,)), ((), ())), preferred_element_type=F32)
    for h in range(MLA_HEADS):
        sl = slice(h * LANES, (h + 1) * LANES)
        q_ref[h, :, 0:LANES] = q_nope[:, sl].astype(BF16)
        q_rope = jnp.where(bias_lane, 1.0, (q_r[:, sl] * cc + q_s[:, sl] * ss) * scale)
        q_ref[h, :, LANES:2 * LANES] = q_rope.astype(BF16)
        k_ref[h, :, 0:LANES] = k_nope[:, sl].astype(BF16)
        k_ref[h, :, LANES:2 * LANES] = k_rope
        vt_ref[h, 0] = v_t[h * MLA_V:(h + 1) * MLA_V, :].astype(BF16)


def _attn_kernel(q_ref, k_ref, vt_ref, o_ref, sa_ref, sb_ref, m_ref, l_ref, acc_ref, *, tq):
    i = pl.program_id(2)
    heads = q_ref.shape[0]
    m_ref[...] = jnp.full(m_ref.shape, MASK_VALUE, F32)
    l_ref[...] = jnp.zeros(l_ref.shape, F32)
    acc_ref[...] = jnp.zeros(acc_ref.shape, F32)
    shift = CHUNK.bit_length() - 1

    def scores(s_ref, j):
        for h in range(heads):
            k = k_ref[h, pl.ds(pl.multiple_of(j * tq, tq), tq), :]
            s_ref[h] = lax.dot_general(k, q_ref[h], (((1,), (1,)), ((), ())), preferred_element_type=F32)

    def absorb(s_ref, j, diagonal):
        for h in range(heads):
            s = s_ref[h]
            if diagonal:
                k_chunk = lax.broadcasted_iota(jnp.int32, (tq, tq), 0) >> shift
                q_chunk = lax.broadcasted_iota(jnp.int32, (tq, tq), 1) >> shift
                s = jnp.where(k_chunk <= q_chunk, s, MASK_VALUE)
            m_prev = m_ref[h]
            m_new = jnp.maximum(m_prev, jnp.max(s, axis=0, keepdims=True))
            alpha = jnp.exp2(m_prev - m_new)
            p = jnp.exp2(s - m_new)
            l_ref[h] = alpha * l_ref[h] + jnp.sum(p, axis=0, keepdims=True)
            acc_ref[h] = alpha * acc_ref[h] + jnp.dot(vt_ref[h, j], p.astype(BF16), preferred_element_type=F32)
            m_ref[h] = m_new

    scores(sa_ref, 0)

    def body(t, carry):
        j = 2 * t
        scores(sb_ref, j + 1)
        absorb(sa_ref, j, False)
        scores(sa_ref, j + 2)
        absorb(sb_ref, j + 1, False)
        return carry

    lax.fori_loop(0, i // 2, body, 0)

    @pl.when(i % 2 == 0)
    def _():
        absorb(sa_ref, i, True)

    @pl.when(i % 2 == 1)
    def _():
        scores(sb_ref, i)
        absorb(sa_ref, i - 1, False)
        absorb(sb_ref, i, True)

    for h in range(heads):
        o_ref[:, h * MLA_V:(h + 1) * MLA_V] = (acc_ref[h] / l_ref[h]).T.astype(o_ref.dtype)


def _rope_tables(lb, front):
    pos = (jnp.arange(lb) - front).astype(F32)
    inv = ROPE_BASE ** (-jnp.arange(0, MLA_ROPE, 2, dtype=F32) / MLA_ROPE)
    ang = pos[:, None] * inv[None, :]
    cos, sin = jnp.cos(ang), jnp.sin(ang)
    zeros = jnp.zeros((lb, LANES - MLA_ROPE), F32)
    cc = jnp.concatenate([cos, cos, zeros], axis=1)
    ss = jnp.concatenate([-sin, sin, zeros], axis=1)
    return cc, ss


def _pad_heads(w, width, take):
    k = w.shape[0]
    w = w.reshape(k, MLA_HEADS, width)[:, :, take]
    pad = LANES - w.shape[-1]
    if pad:
        w = jnp.concatenate([w, jnp.zeros((k, MLA_HEADS, pad), w.dtype)], axis=-1)
    return w.reshape(k, MLA_HEADS * LANES).astype(BF16)


def _mla_mixer(proj, nb, lb, front, q_norm, w_uq, kv_norm, w_ukv):
    r = proj.shape[0]
    half = MLA_ROPE // 2
    qw = MLA_NOPE + MLA_ROPE
    rope_cols = jnp.arange(MLA_NOPE, qw)
    swap_cols = jnp.concatenate([rope_cols[half:], rope_cols[:half]])
    wqn = _pad_heads(w_uq, qw, jnp.arange(MLA_NOPE))
    wqr = _pad_heads(w_uq, qw, rope_cols)
    wqs = _pad_heads(w_uq, qw, swap_cols)
    wkn = _pad_heads(w_ukv, MLA_NOPE + MLA_V, jnp.arange(MLA_NOPE))
    wvt = _pad_heads(w_ukv, MLA_NOPE + MLA_V, jnp.arange(MLA_NOPE, MLA_NOPE + MLA_V)).T
    cc, ss = _rope_tables(lb, front)
    scale = qw ** -0.5 * math.log2(math.e)

    tq = _pick(lb, (640, 128))
    nq = lb // tq
    hl = MLA_HEADS * LANES
    lat_w = MLA_Q_RANK + MLA_KV_RANK + 2 * LANES
    const = lambda i: (0, 0)
    q, k, vt = pl.pallas_call(
        functools.partial(_mla_prep_kernel, scale=scale, tiles_per_batch=nq, front=front),
        grid=(r // tq,),
        in_specs=[pl.BlockSpec((tq, lat_w), lambda i: (i, 1)),
                  pl.BlockSpec((tq, LANES), lambda i: (i % nq, 0)),
                  pl.BlockSpec((tq, LANES), lambda i: (i % nq, 0)),
                  pl.BlockSpec((1, MLA_Q_RANK), const),
                  pl.BlockSpec((1, MLA_KV_RANK), const),
                  pl.BlockSpec((MLA_Q_RANK, hl), const),
                  pl.BlockSpec((MLA_Q_RANK, hl), const),
                  pl.BlockSpec((MLA_Q_RANK, hl), const),
                  pl.BlockSpec((MLA_KV_RANK, hl), const),
                  pl.BlockSpec((hl, MLA_KV_RANK), const)],
        out_specs=[pl.BlockSpec((MLA_HEADS, tq, 2 * LANES), lambda i: (0, i, 0)),
                   pl.BlockSpec((MLA_HEADS, tq, 2 * LANES), lambda i: (0, i, 0)),
                   pl.BlockSpec((MLA_HEADS, 1, MLA_V, tq), lambda i: (0, i, 0, 0))],
        out_shape=[jax.ShapeDtypeStruct((MLA_HEADS, r, 2 * LANES), BF16),
                   jax.ShapeDtypeStruct((MLA_HEADS, r, 2 * LANES), BF16),
                   jax.ShapeDtypeStruct((MLA_HEADS, r // tq, MLA_V, tq), BF16)],
        compiler_params=_params("parallel"),
        name="mla_prep",
    )(proj, cc, ss, q_norm.reshape(1, -1), kv_norm.reshape(1, -1), wqn, wqr, wqs, wkn, wvt)

    hs = 2
    return pl.pallas_call(
        functools.partial(_attn_kernel, tq=tq),
        grid=(nb, MLA_HEADS // hs, nq),
        in_specs=[pl.BlockSpec((hs, tq, 2 * LANES), lambda b, h, i: (h, b * nq + i, 0)),
                  pl.BlockSpec((hs, lb, 2 * LANES), lambda b, h, i: (h, b, 0)),
                  pl.BlockSpec((hs, nq, MLA_V, tq), lambda b, h, i: (h, b, 0, 0))],
        out_specs=pl.BlockSpec((tq, hs * MLA_V), lambda b, h, i: (b * nq + i, h)),
        out_shape=jax.ShapeDtypeStruct((r, hl), BF16),
        scratch_shapes=[pltpu.VMEM((hs, tq, tq), F32), pltpu.VMEM((hs, tq, tq), F32),
                        pltpu.VMEM((hs, 1, tq), F32), pltpu.VMEM((hs, 1, tq), F32),
                        pltpu.VMEM((hs, MLA_V, tq), F32)],
        compiler_params=_params("parallel", "parallel", "arbitrary"),
        name="mla_attention",
    )(q, k, vt)


def _proj_conv_kernel(x_ref, xh_ref, w_ref, cw_ref, cb_ref, o_ref, wb_ref, pre_ref, *, tm, lb, front):
    tile = pl.program_id(1)

    @pl.when(tile == 0)
    def _():
        wb_ref[...] = w_ref[...].astype(BF16)

    pre_ref[...] = lax.dot_general(x_ref[...], wb_ref[...], NT_DIMS, preferred_element_type=F32)
    pre = pre_ref[...]
    pre_halo = lax.dot_general(xh_ref[...], wb_ref[...], NT_DIMS, preferred_element_type=F32)
    w = cw_ref[...]
    bias = cb_ref[...]
    k = w.shape[0]

    def taps(x, halo):
        acc = bias + w[k - 1:k, :] * x
        head = bias + w[k - 1:k, :] * x[:SUBLANES]
        sub = lax.broadcasted_iota(jnp.int32, halo.shape, 0)
        for d in range(1, k):
            wd = w[k - 1 - d:k - d, :]
            acc += wd * pltpu.roll(x, d, 0)
            head += wd * jnp.where(sub < d, pltpu.roll(halo, d, 0), pltpu.roll(x[:SUBLANES], d, 0))
        return acc, head

    def silu(v):
        return (v * jax.nn.sigmoid(v)).astype(o_ref.dtype)

    acc, head = taps(pre, pre_halo)
    o_ref[...] = silu(acc)
    o_ref[0:SUBLANES, :] = silu(head)

    hb = -(-(front + k) // (2 * SUBLANES)) * (2 * SUBLANES)
    @pl.when((tile * tm) % lb == 0)
    def _():
        xh = pre_ref[0:hb, :]
        xh = jnp.where(lax.broadcasted_iota(jnp.int32, xh.shape, 0) < front, 0.0, xh)
        acc_h, head_h = taps(xh, jnp.zeros((SUBLANES, xh.shape[1]), F32))
        o_ref[0:hb, :] = silu(acc_h)
        o_ref[0:SUBLANES, :] = silu(head_h)


def _split3(x):
    hi = x.astype(BF16).astype(F32)
    mid = (x - hi).astype(BF16).astype(F32)
    lo = (x - hi - mid).astype(BF16).astype(F32)
    return hi, mid, lo


def _ssd_kernel(xs_ref, bm_ref, cm_ref, dtc_ref, dtr_ref, bc_ref, br_ref, ac_ref, ar_ref,
                d_ref, z_ref, ng_ref, ex_ref, o_ref, state_ref, xdt_ref, *, front):
    c = pl.program_id(1)
    q = xs_ref.shape[0]
    groups, nh = dtr_ref.shape[0], dtr_ref.shape[1]
    hp = xs_ref.shape[1] // groups
    ns = bm_ref.shape[1] // groups
    hd = hp // nh
    wide = dtc_ref.shape[1]

    @pl.when(c == 0)
    def _():
        state_ref[...] = jnp.zeros(state_ref.shape, F32)

    first_live = jnp.where(c == 0, front, 0)
    live_c = lax.broadcasted_iota(jnp.int32, (q, wide), 0) >= first_live
    live_r = lax.broadcasted_iota(jnp.int32, (nh, q), 1) >= first_live
    li = lax.broadcasted_iota(jnp.int32, (q, q), 0)
    si = lax.broadcasted_iota(jnp.int32, (q, q), 1)
    lower = li >= si
    upper16 = (li <= si).astype(BF16)
    lane = lax.broadcasted_iota(jnp.int32, (q, 2 * hd), 1)

    def terms(x, axis):
        return jnp.concatenate(_split3(x), axis=axis).astype(BF16)

    dt_c = jnp.where(live_c, jax.nn.softplus(dtc_ref[...] + bc_ref[0]), 0.0)
    dt_terms = terms(dt_c, 1)
    cs3 = jnp.dot(lower.astype(BF16), terms(dt_c * ac_ref[0], 1), preferred_element_type=F32)
    cs_terms = terms(cs3[:, :wide] + cs3[:, wide:2 * wide] + cs3[:, 2 * wide:], 1)

    for g in range(groups):
        cols = slice(g * hp, (g + 1) * hp)
        expand3 = ex_ref[g]
        dt_full = jnp.dot(dt_terms, expand3, preferred_element_type=F32)
        cs_full = jnp.dot(cs_terms, expand3, preferred_element_type=F32)
        dt_r = jnp.where(live_r, jax.nn.softplus(dtr_ref[g] + br_ref[g]), 0.0)
        csr3 = jnp.dot(terms(dt_r * ar_ref[g], 0), upper16, preferred_element_type=F32)
        cs_r = csr3[:nh] + csr3[nh:2 * nh] + csr3[2 * nh:]
        last = cs_full[q - 1:q, :]

        xdt_ref[g] = xs_ref[:, cols].astype(F32) * dt_full
        bm = bm_ref[:, g * ns:(g + 1) * ns]
        cm = cm_ref[:, g * ns:(g + 1) * ns]
        cb = lax.dot_general(cm, bm, (((1,), (1,)), ((), ())), preferred_element_type=F32)

        prev = state_ref[g]
        y = jnp.dot(cm, prev.astype(BF16), preferred_element_type=F32) * jnp.exp2(cs_full)
        local = lax.dot_general(bm, (xdt_ref[g] * jnp.exp2(last - cs_full)).astype(BF16),
                                (((0,), (0,)), ((), ())), preferred_element_type=F32)
        state_ref[g] = prev * jnp.exp2(last) + local

        diag = []
        for pr in range(nh // 2):
            ms = []
            for h in (2 * pr, 2 * pr + 1):
                seg = cs_full[:, h * hd:h * hd + 1] - cs_r[h:h + 1, :]
                ms.append((cb * jnp.exp2(jnp.where(lower, seg, -jnp.inf))).astype(BF16))
            xp = xdt_ref[g, :, pr * 2 * hd:(pr + 1) * 2 * hd]
            rhs = jnp.concatenate([jnp.where(lane < hd, xp, 0.0), jnp.where(lane >= hd, xp, 0.0)], axis=0)
            diag.append(jnp.dot(jnp.concatenate(ms, axis=1), rhs.astype(BF16), preferred_element_type=F32))
        y = y + jnp.concatenate(diag, axis=1) + d_ref[g] * xs_ref[:, cols].astype(F32)

        z = z_ref[:, cols].astype(F32)
        y = y * (z * jax.nn.sigmoid(z))
        y = y * lax.rsqrt(jnp.mean(y * y, axis=-1, keepdims=True) + RMS_EPS) * ng_ref[:, cols]
        o_ref[:, cols] = y.astype(o_ref.dtype)


def _mamba2_mixer(h16, nb, lb, front, w_in, conv_w, conv_b, dt_bias, a_log, d, norm_g):
    r = h16.shape[0]
    inner = norm_g.shape[0]
    gn = SSD_GROUPS * SSD_STATE
    cdim = inner + 2 * gn
    tm = _pick(r, (1664, 1280, 640, 128))
    nheads = SSD_GROUPS * SSD_HPG
    tn = _pick(math.gcd(inner, cdim), (1024, 512, 128))
    wt = w_in.T
    z = _matmul_ws(h16, wt, BF16, col0=0, ncols=inner, tm=tm, tn=tn, name="l1_in_z")
    tcv = _pick(math.gcd(inner, cdim), (512, 256, 128))
    blocks = tm // SUBLANES
    dm = h16.shape[1]
    c0 = inner // tcv
    assert lb % tm == 0
    act = pl.pallas_call(
        functools.partial(_proj_conv_kernel, tm=tm, lb=lb, front=front),
        grid=(cdim // tcv, r // tm),
        in_specs=[pl.BlockSpec((tm, dm), lambda j, i: (i, 0)),
                  pl.BlockSpec((SUBLANES, dm), lambda j, i: (jnp.maximum(i * blocks - 1, 0), 0)),
                  pl.BlockSpec((tcv, dm), lambda j, i: (c0 + j, 0)),
                  pl.BlockSpec((SSD_CONV, tcv), lambda j, i: (0, j)),
                  pl.BlockSpec((1, tcv), lambda j, i: (0, j))],
        out_specs=pl.BlockSpec((tm, tcv), lambda j, i: (i, j)),
        out_shape=jax.ShapeDtypeStruct((r, cdim), BF16),
        scratch_shapes=[pltpu.VMEM((tcv, dm), BF16), pltpu.VMEM((tm, tcv), F32)],
        compiler_params=_params("parallel", "arbitrary"),
        name="l1_in_xbc_conv",
    )(h16, h16, wt, conv_w, conv_b.reshape(1, cdim))
    assert w_in.shape[1] == inner + cdim + nheads and nheads <= LANES
    dt_raw = _matmul_ws(h16, wt, F32, col0=inner + cdim, ncols=LANES, tm=tm, tn=LANES, name="l1_in_dt")

    gsz = inner // SSD_GROUPS
    nc = lb // SSD_BLOCK
    dt_r = dt_raw[:, :nheads].T.reshape(SSD_GROUPS, SSD_HPG, r)
    lane_pad = jnp.zeros((LANES - nheads,), F32)
    bias = dt_bias.reshape(SSD_GROUPS, SSD_HPG)
    bias_c = jnp.concatenate([dt_bias, lane_pad]).reshape(1, 1, LANES)
    a = -jnp.exp(a_log) * math.log2(math.e)
    a_c = jnp.concatenate([a, lane_pad]).reshape(1, 1, LANES)
    a = a.reshape(SSD_GROUPS, SSD_HPG)
    d_full = jnp.repeat(d, SSD_HEAD_DIM).reshape(SSD_GROUPS, 1, gsz)
    head_of_lane = (jnp.arange(gsz)[None, None, :] // SSD_HEAD_DIM
                    + SSD_HPG * jnp.arange(SSD_GROUPS)[:, None, None])
    expand = (jnp.arange(LANES)[None, :, None] == head_of_lane).astype(BF16)
    expand3 = jnp.concatenate([expand] * 3, axis=1)
    row = lambda b, c: b * nc + c
    whole3 = lambda b, c: (0, 0, 0)
    ng_ = SSD_GROUPS
    return pl.pallas_call(
        functools.partial(_ssd_kernel, front=front),
        grid=(nb, nc),
        in_specs=[pl.BlockSpec((SSD_BLOCK, inner), lambda b, c: (row(b, c), 0)),
                  pl.BlockSpec((SSD_BLOCK, gn), lambda b, c: (row(b, c), inner // gn)),
                  pl.BlockSpec((SSD_BLOCK, gn), lambda b, c: (row(b, c), inner // gn + 1)),
                  pl.BlockSpec((SSD_BLOCK, LANES), lambda b, c: (row(b, c), 0)),
                  pl.BlockSpec((ng_, SSD_HPG, SSD_BLOCK), lambda b, c: (0, 0, row(b, c))),
                  pl.BlockSpec((1, 1, LANES), whole3),
                  pl.BlockSpec((ng_, SSD_HPG, 1), whole3),
                  pl.BlockSpec((1, 1, LANES), whole3),
                  pl.BlockSpec((ng_, SSD_HPG, 1), whole3),
                  pl.BlockSpec((ng_, 1, gsz), whole3),
                  pl.BlockSpec((SSD_BLOCK, inner), lambda b, c: (row(b, c), 0)),
                  pl.BlockSpec((1, inner), lambda b, c: (0, 0)),
                  pl.BlockSpec((ng_, 3 * LANES, gsz), whole3)],
        out_specs=pl.BlockSpec((SSD_BLOCK, inner), lambda b, c: (row(b, c), 0)),
        out_shape=jax.ShapeDtypeStruct((r, inner), BF16),
        scratch_shapes=[pltpu.VMEM((ng_, SSD_STATE, gsz), F32), pltpu.VMEM((ng_, SSD_BLOCK, gsz), F32)],
        compiler_params=_params("parallel", "arbitrary"),
        name="ssd_scan",
    )(act, act, act, dt_raw, dt_r,
      bias_c, bias[:, :, None], a_c, a[:, :, None],
      d_full, z, norm_g.reshape(1, inner), expand3)


def kernel(x, meta_tokens, l0_w_in, l0_s5_log_dt, l0_s5_a_re, l0_s5_a_im, l0_s5_b_re, l0_s5_b_im, l0_s5_c_re, l0_s5_c_im, l0_s5_d, l0_s5_w_glu, l0_mla_q_norm, l0_mla_w_uq, l0_mla_kv_norm, l0_mla_w_ukv, l0_w_out, l0_ln1_g, l0_ln1_b, l0_ffn_w_gate, l0_ffn_w_up, l0_ffn_w_down, l0_ln2_g, l0_ln2_b, l1_w_in, l1_conv_w, l1_conv_b, l1_dt_bias, l1_a_log, l1_d, l1_norm_g, l1_w_out, l1_ln1_g, l1_ln1_b, l1_ffn_w_gate, l1_ffn_w_up, l1_ffn_w_down, l1_ln2_g, l1_ln2_b):
    nb, seq, dm = x.shape
    n_meta = meta_tokens.shape[0]
    front = (-n_meta) % SSD_BLOCK
    lb = front + n_meta + seq
    r = nb * lb
    meta = jnp.broadcast_to(meta_tokens[None].astype(x.dtype), (nb, n_meta, dm))
    h32 = jnp.concatenate([jnp.zeros((nb, front, dm), x.dtype), meta, x], axis=1).reshape(r, dm)

    s5w = l0_s5_d.shape[0]
    o3 = s5w + MLA_Q_RANK + MLA_KV_RANK
    kr = l0_w_in[:, o3:]
    half = MLA_ROPE // 2
    zpad = jnp.zeros((dm, LANES - MLA_ROPE), l0_w_in.dtype)
    w0 = jnp.concatenate([l0_w_in[:, :o3], kr, zpad, kr[:, half:], kr[:, :half], zpad], axis=1).astype(BF16)
    proj = _matmul(h32, w0, F32, tm=_pick(r, (1664, 1280, 640, 128)), tn=_pick(w0.shape[1], (512, 256, 128)),
                   name="l0_in")
    a_out = _s5_mixer(proj, nb, lb, front, l0_s5_log_dt, l0_s5_a_re, l0_s5_a_im, l0_s5_b_re, l0_s5_b_im,
                      l0_s5_c_re, l0_s5_c_im, l0_s5_d, l0_s5_w_glu)
    b_out = _mla_mixer(proj, nb, lb, front, l0_mla_q_norm, l0_mla_w_uq, l0_mla_kv_norm, l0_mla_w_ukv)
    tml = _pick(r, (640, 128))
    h32, h16 = _mm_res_ln((a_out, b_out), l0_w_out.astype(BF16), h32, l0_ln1_g, l0_ln1_b, tm=tml,
                          tk=a_out.shape[1], name="l0_out_ln")
    h32, h16 = _ffn_block(h32, h16, l0_ffn_w_gate, l0_ffn_w_up, l0_ffn_w_down, l0_ln2_g, l0_ln2_b, "l0")

    y = _mamba2_mixer(h16, nb, lb, front, l1_w_in, l1_conv_w, l1_conv_b, l1_dt_bias, l1_a_log, l1_d, l1_norm_g)
    h32, h16 = _mm_res_ln(y, l1_w_out.astype(BF16), h32, l1_ln1_g, l1_ln1_b, tm=tml, tk=1024, name="l1_out_ln")
    h32, h16 = _ffn_block(h32, h16, l1_ffn_w_gate, l1_ffn_w_up, l1_ffn_w_down, l1_ln2_g, l1_ln2_b, "l1")
    return h32.reshape(nb, lb, dm)[:, front + n_meta:, :]
```

```python
import functools
import math

import jax
import jax.numpy as jnp
from jax import lax
from jax.experimental import pallas as pl
from jax.experimental.pallas import tpu as pltpu

F32 = jnp.float32
BF16 = jnp.bfloat16

DEPTH = 2
CHUNK = 64
N_META = 16
DN_ALPHA = (2 * DEPTH) ** 0.25
LN_EPS = 1e-5
RMS_EPS = 1e-6
S5_GROUP = 16
S5_STATE = 64
S5_Q = 16
S5_LANE_GROUPS = 128 // S5_GROUP
MLA_HEADS = 8
MLA_NOPE = 128
MLA_ROPE = 64
MLA_V = 128
MLA_Q_RANK = 512
MLA_KV_RANK = 256
ROPE_BASE = 10000.0
SSD_HEAD_DIM = 64
SSD_GROUPS = 8
SSD_HPG = 8
SSD_STATE = 128
SSD_CONV = 4
SSD_BLOCK = 128

LANES = 128
SUBLANES = 8
VMEM_LIMIT = 60 * 1024 * 1024
MASK_VALUE = -1e30
HI = lax.Precision.HIGHEST


def _pick(n, candidates):
    for c in candidates:
        if n % c == 0:
            return c
    raise ValueError(f"no tile for {n} in {candidates}")


def _params(*sem):
    return pltpu.CompilerParams(dimension_semantics=sem, vmem_limit_bytes=VMEM_LIMIT)


def _mm_kernel(x_ref, w_ref, o_ref):
    o_ref[...] = jnp.dot(x_ref[...].astype(BF16), w_ref[...], preferred_element_type=F32).astype(o_ref.dtype)


def _matmul(x, w, out_dtype, *, tm, tn, name):
    m, k = x.shape
    n = w.shape[1]
    return pl.pallas_call(
        _mm_kernel,
        grid=(m // tm, n // tn),
        in_specs=[pl.BlockSpec((tm, k), lambda i, j: (i, 0)),
                  pl.BlockSpec((k, tn), lambda i, j: (0, j))],
        out_specs=pl.BlockSpec((tm, tn), lambda i, j: (i, j)),
        out_shape=jax.ShapeDtypeStruct((m, n), out_dtype),
        compiler_params=_params("parallel", "arbitrary"),
        name=name,
    )(x, w)


NT_DIMS = (((1,), (1,)), ((), ()))


def _mm_ws_kernel(x_ref, wt_ref, o_ref, wb_ref, *, valid):
    @pl.when(pl.program_id(1) == 0)
    def _():
        w = wt_ref[...]
        if valid < w.shape[0]:
            w = jnp.where(lax.broadcasted_iota(jnp.int32, w.shape, 0) < valid, w, 0.0)
        wb_ref[...] = w.astype(BF16)

    o_ref[...] = lax.dot_general(x_ref[...], wb_ref[...], NT_DIMS, preferred_element_type=F32).astype(o_ref.dtype)


def _matmul_ws(x, wt, out_dtype, *, col0, ncols, tm, tn, name):
    m, k = x.shape
    assert col0 % tn == 0 and ncols % tn == 0
    c0 = col0 // tn
    valid = min(tn, wt.shape[0] - (col0 + ncols - tn))
    assert valid == tn or ncols == tn
    return pl.pallas_call(
        functools.partial(_mm_ws_kernel, valid=valid),
        grid=(ncols // tn, m // tm),
        in_specs=[pl.BlockSpec((tm, k), lambda j, i: (i, 0)),
                  pl.BlockSpec((tn, k), lambda j, i: (c0 + j, 0))],
        out_specs=pl.BlockSpec((tm, tn), lambda j, i: (i, j)),
        out_shape=jax.ShapeDtypeStruct((m, ncols), out_dtype),
        scratch_shapes=[pltpu.VMEM((tn, k), BF16)],
        compiler_params=_params("parallel", "arbitrary"),
        name=name,
    )(x, wt)


def _mm_res_ln_kernel(*refs, nk, parts):
    x_refs = refs[:parts]
    w_ref, res_ref, g_ref, b_ref, o32_ref, o16_ref, acc_ref = refs[parts:]
    k = pl.program_id(1)

    def partial_sum(x_ref):
        return jnp.dot(x_ref[...], w_ref[...], preferred_element_type=F32)

    @pl.when(k == 0)
    def _():
        acc_ref[...] = partial_sum(x_refs[0])

    if parts == 1:
        @pl.when(k > 0)
        def _():
            acc_ref[...] += partial_sum(x_refs[0])
    else:
        for p in range(1, parts):
            @pl.when(k == p)
            def _(p=p):
                acc_ref[...] += partial_sum(x_refs[p])

    @pl.when(k == nk - 1)
    def _():
        y = DN_ALPHA * res_ref[...] + acc_ref[...]
        mu = jnp.mean(y, axis=-1, keepdims=True)
        yc = y - mu
        var = jnp.mean(yc * yc, axis=-1, keepdims=True)
        out = yc * lax.rsqrt(var + LN_EPS) * g_ref[...] + b_ref[...]
        o32_ref[...] = out
        o16_ref[...] = out.astype(BF16)


def _mm_res_ln(x, w, res, g, b, *, tm, tk, name):
    xs = x if isinstance(x, tuple) else (x,)
    m = xs[0].shape[0]
    kk, d = w.shape
    nk = kk // tk
    if len(xs) > 1:
        assert len(xs) == nk and all(p.shape == (m, tk) for p in xs)
        x_specs = [pl.BlockSpec((tm, tk), lambda i, k: (i, 0)) for _ in xs]
    else:
        x_specs = [pl.BlockSpec((tm, tk), lambda i, k: (i, k))]
    return pl.pallas_call(
        functools.partial(_mm_res_ln_kernel, nk=nk, parts=len(xs)),
        grid=(m // tm, nk),
        in_specs=x_specs + [
                  pl.BlockSpec((tk, d), lambda i, k: (k, 0)),
                  pl.BlockSpec((tm, d), lambda i, k: (i, 0)),
                  pl.BlockSpec((1, d), lambda i, k: (0, 0)),
                  pl.BlockSpec((1, d), lambda i, k: (0, 0))],
        out_specs=[pl.BlockSpec((tm, d), lambda i, k: (i, 0)),
                   pl.BlockSpec((tm, d), lambda i, k: (i, 0))],
        out_shape=[jax.ShapeDtypeStruct((m, d), F32), jax.ShapeDtypeStruct((m, d), BF16)],
        scratch_shapes=[pltpu.VMEM((tm, d), F32)],
        compiler_params=_params("parallel", "arbitrary"),
        name=name,
    )(*xs, w, res, g.reshape(1, d), b.reshape(1, d))


def _ffn_up_kernel(x_ref, wg_ref, wu_ref, o_ref, wgb_ref, wub_ref):
    @pl.when(pl.program_id(1) == 0)
    def _():
        wgb_ref[...] = wg_ref[...].astype(BF16)
        wub_ref[...] = wu_ref[...].astype(BF16)

    x = x_ref[...]
    g = jnp.dot(x, wgb_ref[...], preferred_element_type=F32)
    u = jnp.dot(x, wub_ref[...], preferred_element_type=F32)
    o_ref[...] = (g * jax.nn.sigmoid(g) * u).astype(o_ref.dtype)


def _ffn_up(x, wg, wu, *, tm, tf, name):
    m, d = x.shape
    f = wg.shape[1]
    return pl.pallas_call(
        _ffn_up_kernel,
        grid=(f // tf, m // tm),
        in_specs=[pl.BlockSpec((tm, d), lambda j, i: (i, 0)),
                  pl.BlockSpec((d, tf), lambda j, i: (0, j)),
                  pl.BlockSpec((d, tf), lambda j, i: (0, j))],
        out_specs=pl.BlockSpec((tm, tf), lambda j, i: (i, j)),
        out_shape=jax.ShapeDtypeStruct((m, f), BF16),
        scratch_shapes=[pltpu.VMEM((d, tf), BF16), pltpu.VMEM((d, tf), BF16)],
        compiler_params=_params("parallel", "arbitrary"),
        name=name,
    )(x, wg, wu)


def _ffn_block(h32, h16, w_gate, w_up, w_down, g, b, tag):
    m = h16.shape[0]
    f = w_gate.shape[1]
    hid = _ffn_up(h16, w_gate, w_up,
                  tm=_pick(m, (1664, 1280, 640, 128)), tf=_pick(f, (512, 256, 128)),
                  name=f"{tag}_ffn_up")
    return _mm_res_ln(hid, w_down.astype(BF16), h32, g, b,
                      tm=_pick(m, (640, 128)), tk=_pick(f, (1408, 512, 256, 128)), name=f"{tag}_ffn_down_ln")


def _s5_tables(log_dt, a_re, a_im, b_re, b_im, c_re, c_im):
    g_, p_ = a_re.shape
    j_ = b_re.shape[-1]
    q = S5_Q
    dt = jnp.exp(log_dt)[:, None]
    lr = dt * a_re
    li = dt * a_im
    ks = jnp.arange(q + 1, dtype=F32)[:, None, None]
    mag = jnp.exp(ks * lr)
    pw_re = mag * jnp.cos(ks * li)
    pw_im = mag * jnp.sin(ks * li)
    ab_re, ab_im = pw_re[1], pw_im[1]
    den = a_re * a_re + a_im * a_im
    nr = ab_re - 1.0
    f_re = (nr * a_re + ab_im * a_im) / den
    f_im = (ab_im * a_re - nr * a_im) / den
    bb_re = f_re[..., None] * b_re - f_im[..., None] * b_im
    bb_im = f_re[..., None] * b_im + f_im[..., None] * b_re
    cp_re = c_re[None] * pw_re[:, :, None, :] - c_im[None] * pw_im[:, :, None, :]
    cp_im = c_re[None] * pw_im[:, :, None, :] + c_im[None] * pw_re[:, :, None, :]
    kern = jnp.sum(cp_re[:q, :, :, :, None] * bb_re[None, :, None, :, :]
                   - cp_im[:q, :, :, :, None] * bb_im[None, :, None, :, :], axis=3)
    kr = (q - 1 - jnp.arange(q)).astype(F32)[:, None, None]
    rev_mag = jnp.exp(kr * lr)
    rev_re = (rev_mag * jnp.cos(kr * li))[:, :, :, None]
    rev_im = (rev_mag * jnp.sin(kr * li))[:, :, :, None]
    st_re = (rev_re * bb_re[None] - rev_im * bb_im[None]).transpose(1, 0, 3, 2)
    st_im = (rev_re * bb_im[None] + rev_im * bb_re[None]).transpose(1, 0, 3, 2)
    wo_re = cp_re[1:].transpose(1, 3, 0, 2)
    wo_im = (-cp_im[1:]).transpose(1, 3, 0, 2)

    gl = S5_LANE_GROUPS
    nblk = g_ // gl

    def block_diag(w, rows_per_group, cols_per_group):
        w = jnp.tile(w, (1,) * (w.ndim - 1) + (gl,))
        row_g = jnp.arange(gl * rows_per_group)[:, None] // rows_per_group
        col_g = jnp.arange(gl * cols_per_group)[None, :] // cols_per_group
        return jnp.where(row_g == col_g, w, 0.0)

    kc = kern.reshape(q, nblk, gl, j_, j_).transpose(1, 0, 2, 4, 3).reshape(nblk, q, gl * j_, j_)
    lag_b = block_diag(kc, j_, j_).astype(BF16)

    def st_pairs(st):
        w = st.reshape(nblk, gl, q, j_, p_).transpose(0, 2, 1, 3, 4).reshape(nblk, q, gl * j_, p_)
        return jnp.concatenate([w, w], axis=3)

    def wo_blocks(wo):
        w = jnp.tile(wo.reshape(nblk, gl * p_, q, j_), (1, 1, 1, gl))
        row_g = jnp.arange(gl * p_)[:, None, None] // p_
        col_g = jnp.arange(gl * j_)[None, None, :] // j_
        return jnp.where(row_g == col_g, w, 0.0).reshape(nblk, gl * p_, q * gl * j_).astype(BF16)

    aq_re = pw_re[q].reshape(1, g_ * p_)
    aq_im = pw_im[q].reshape(1, g_ * p_)
    return lag_b, st_pairs(st_re), st_pairs(st_im), wo_blocks(wo_re), wo_blocks(wo_im), aq_re, aq_im


def _s5_rows(u_ref, front_rows):
    rows = u_ref.shape[0] // S5_Q
    live = lax.broadcasted_iota(jnp.int32, (rows, u_ref.shape[1]), 0) >= front_rows
    pieces = [jnp.where(live, u_ref[pl.ds(i, rows, stride=S5_Q), :], 0.0) for i in range(S5_Q)]
    return pieces, jnp.concatenate([p.astype(BF16) for p in pieces], axis=1)


def _s5_local_kernel(u_ref, sre_ref, sim_ref, zr_ref, zi_ref, w_ref, *, front_rows):
    @pl.when(pl.program_id(1) == 0)
    def _():
        half = w_ref.shape[1] // 2
        per_col = LANES // S5_STATE
        row_g = lax.broadcasted_iota(jnp.int32, (LANES, LANES), 0) // S5_GROUP
        lane_g = lax.broadcasted_iota(jnp.int32, (LANES, LANES), 1) // S5_STATE
        for i in range(S5_Q):
            rows = slice(i * LANES, (i + 1) * LANES)
            for kb in range(half // LANES):
                keep = row_g == lane_g + kb * per_col
                w_ref[rows, kb * LANES:(kb + 1) * LANES] = jnp.where(keep, sre_ref[0, i], 0.0).astype(BF16)
                w_ref[rows, half + kb * LANES:half + (kb + 1) * LANES] = (
                    jnp.where(keep, sim_ref[0, i], 0.0).astype(BF16))

    _, x = _s5_rows(u_ref, front_rows)
    z = jnp.dot(x, w_ref[...], preferred_element_type=F32)
    half = z.shape[1] // 2
    zr_ref[...] = z[:, :half]
    zi_ref[...] = z[:, half:]


def _s5_scan_kernel(zr_ref, zi_ref, ar_ref, ai_ref, sr_ref, si_ref, *, nb, nc):
    ar = ar_ref[...]
    ai = ai_ref[...]

    def body(c, carry):
        new = []
        for b in range(nb):
            s_re, s_im = carry[2 * b], carry[2 * b + 1]
            row = b * nc + c
            sr_ref[pl.ds(row, 1), :] = s_re
            si_ref[pl.ds(row, 1), :] = s_im
            z_re = zr_ref[pl.ds(row, 1), :]
            z_im = zi_ref[pl.ds(row, 1), :]
            new.append(ar * s_re - ai * s_im + z_re)
            new.append(ar * s_im + ai * s_re + z_im)
        return tuple(new)

    zero = jnp.zeros(ar.shape, F32)
    lax.fori_loop(0, nc, body, (zero,) * (2 * nb))


def _s5_out_kernel(u_ref, lag_ref, sr_ref, si_ref, wr_ref, wi_ref, d_ref, y_ref, toe_ref, *, front_rows):
    @pl.when(pl.program_id(1) == 0)
    def _():
        zero = jnp.zeros((LANES, LANES), toe_ref.dtype)
        for i in range(S5_Q):
            for j in range(S5_Q):
                toe_ref[i * LANES:(i + 1) * LANES, j * LANES:(j + 1) * LANES] = lag_ref[0, j - i] if j >= i else zero

    pieces, x = _s5_rows(u_ref, front_rows)
    rows = x.shape[0]
    half = toe_ref.shape[0] // 2
    y = jnp.concatenate([jnp.dot(x[:, :half], toe_ref[:half, :half], preferred_element_type=F32),
                         jnp.dot(x, toe_ref[:, half:], preferred_element_type=F32)], axis=1)
    y += jnp.dot(sr_ref[...].astype(BF16), wr_ref[0], preferred_element_type=F32)
    y += jnp.dot(si_ref[...].astype(BF16), wi_ref[0], preferred_element_type=F32)
    d = d_ref[...]
    for j in range(S5_Q):
        y_ref[pl.ds(j, rows, stride=S5_Q), :] = y[:, j * LANES:(j + 1) * LANES] + d * pieces[j]


def _glu_kernel(y_ref, w_ref, o_ref):
    g = jax.nn.gelu(y_ref[...])
    gate = jax.nn.sigmoid(jnp.dot(g.astype(BF16), w_ref[...], preferred_element_type=F32))
    o_ref[...] = (g * gate).astype(o_ref.dtype)


def _s5_mixer(proj, nb, lb, front, log_dt, a_re, a_im, b_re, b_im, c_re, c_im, d, w_glu):
    r = proj.shape[0]
    g_, p_ = a_re.shape
    w_ = g_ * S5_GROUP
    q = S5_Q
    m = r // q
    nc = lb // q
    nblk = w_ // LANES
    sw = S5_LANE_GROUPS * p_
    kw = q * LANES
    front_rows = front // q
    lag_b, st_re_c, st_im_c, wo_re_b, wo_im_b, aq_re, aq_im = _s5_tables(
        log_dt, a_re, a_im, b_re, b_im, c_re, c_im)

    z_re, z_im = pl.pallas_call(
        functools.partial(_s5_local_kernel, front_rows=front_rows),
        grid=(nblk, nb),
        in_specs=[pl.BlockSpec((lb, LANES), lambda c, b: (b, c)),
                  pl.BlockSpec((1, q, LANES, LANES), lambda c, b: (c, 0, 0, 0)),
                  pl.BlockSpec((1, q, LANES, LANES), lambda c, b: (c, 0, 0, 0))],
        out_specs=[pl.BlockSpec((nc, sw), lambda c, b: (b, c)),
                   pl.BlockSpec((nc, sw), lambda c, b: (b, c))],
        out_shape=[jax.ShapeDtypeStruct((m, g_ * p_), F32)] * 2,
        scratch_shapes=[pltpu.VMEM((kw, 2 * sw), BF16)],
        compiler_params=_params("parallel", "arbitrary"),
        name="s5_local",
    )(proj, st_re_c, st_im_c)

    cw = _pick(g_ * p_, (512, 256, 128))
    s_re, s_im = pl.pallas_call(
        functools.partial(_s5_scan_kernel, nb=nb, nc=nc),
        grid=(g_ * p_ // cw,),
        in_specs=[pl.BlockSpec((m, cw), lambda i: (0, i)),
                  pl.BlockSpec((m, cw), lambda i: (0, i)),
                  pl.BlockSpec((1, cw), lambda i: (0, i)),
                  pl.BlockSpec((1, cw), lambda i: (0, i))],
        out_specs=[pl.BlockSpec((m, cw), lambda i: (0, i)),
                   pl.BlockSpec((m, cw), lambda i: (0, i))],
        out_shape=[jax.ShapeDtypeStruct((m, g_ * p_), F32)] * 2,
        compiler_params=_params("parallel"),
        name="s5_scan",
    )(z_re, z_im, aq_re, aq_im)

    y = pl.pallas_call(
        functools.partial(_s5_out_kernel, front_rows=front_rows),
        grid=(nblk, nb),
        in_specs=[pl.BlockSpec((lb, LANES), lambda c, b: (b, c)),
                  pl.BlockSpec((1, q, LANES, LANES), lambda c, b: (c, 0, 0, 0)),
                  pl.BlockSpec((nc, sw), lambda c, b: (b, c)),
                  pl.BlockSpec((nc, sw), lambda c, b: (b, c)),
                  pl.BlockSpec((1, sw, kw), lambda c, b: (c, 0, 0)),
                  pl.BlockSpec((1, sw, kw), lambda c, b: (c, 0, 0)),
                  pl.BlockSpec((1, LANES), lambda c, b: (0, c))],
        out_specs=pl.BlockSpec((lb, LANES), lambda c, b: (b, c)),
        out_shape=jax.ShapeDtypeStruct((r, w_), F32),
        scratch_shapes=[pltpu.VMEM((kw, kw), BF16)],
        compiler_params=_params("parallel", "arbitrary"),
        name="s5_out",
    )(proj, lag_b, s_re, s_im, wo_re_b, wo_im_b, d.reshape(1, w_))

    tm = _pick(r, (832, 640, 128))
    return pl.pallas_call(
        _glu_kernel,
        grid=(r // tm,),
        in_specs=[pl.BlockSpec((tm, w_), lambda i: (i, 0)),
                  pl.BlockSpec((w_, w_), lambda i: (0, 0))],
        out_specs=pl.BlockSpec((tm, w_), lambda i: (i, 0)),
        out_shape=jax.ShapeDtypeStruct((r, w_), BF16),
        compiler_params=_params("parallel"),
        name="s5_glu",
    )(y, w_glu.astype(BF16))


def _rms(x, g):
    return x * lax.rsqrt(jnp.mean(x * x, axis=-1, keepdims=True) + RMS_EPS) * g


def _mla_prep_kernel(lat_ref, cc_ref, ss_ref, qg_ref, kg_ref, wqn_ref, wqr_ref, wqs_ref, wkn_ref, wvt_ref,
                     q_ref, k_ref, vt_ref, *, scale, tiles_per_batch, front):
    tm = lat_ref.shape[0]
    lat = lat_ref[...]
    o1 = MLA_Q_RANK
    o2 = o1 + MLA_KV_RANK
    o3 = o2 + LANES
    qn = _rms(lat[:, :o1], qg_ref[...]).astype(BF16)
    kvn = _rms(lat[:, o1:o2], kg_ref[...]).astype(BF16)
    cc = cc_ref[...]
    ss = ss_ref[...]
    lane = lax.broadcasted_iota(jnp.int32, (tm, LANES), 1)
    row = (pl.program_id(0) % tiles_per_batch) * tm + lax.broadcasted_iota(jnp.int32, (tm, LANES), 0)
    bias_lane = lane == MLA_ROPE
    k_rope = jnp.where(bias_lane & (row < front), MASK_VALUE, lat[:, o2:o3] * cc + lat[:, o3:] * ss).astype(BF16)
    q_nope = jnp.dot(qn, wqn_ref[...], preferred_element_type=F32) * scale
    q_r = jnp.dot(qn, wqr_ref[...], preferred_element_type=F32)
    q_s = jnp.dot(qn, wqs_ref[...], preferred_element_type=F32)
    k_nope = jnp.dot(kvn, wkn_ref[...], preferred_element_type=F32)
    v_t = lax.dot_general(wvt_ref[...], kvn, (((1,), (1,)), ((), ())), preferred_element_type=F32)
    for h in range(MLA_HEADS):
        sl = slice(h * LANES, (h + 1) * LANES)
        q_ref[h, :, 0:LANES] = q_nope[:, sl].astype(BF16)
        q_rope = jnp.where(bias_lane, 1.0, (q_r[:, sl] * cc + q_s[:, sl] * ss) * scale)
        q_ref[h, :, LANES:2 * LANES] = q_rope.astype(BF16)
        k_ref[h, :, 0:LANES] = k_nope[:, sl].astype(BF16)
        k_ref[h, :, LANES:2 * LANES] = k_rope
        vt_ref[h, 0] = v_t[h * MLA_V:(h + 1) * MLA_V, :].astype(BF16)


def _attn_kernel(q_ref, k_ref, vt_ref, o_ref, sa_ref, sb_ref, m_ref, l_ref, acc_ref, *, tq):
    i = pl.program_id(2)
    heads = q_ref.shape[0]
    m_ref[...] = jnp.full(m_ref.shape, MASK_VALUE, F32)
    l_ref[...] = jnp.zeros(l_ref.shape, F32)
    acc_ref[...] = jnp.zeros(acc_ref.shape, F32)
    shift = CHUNK.bit_length() - 1

    def scores(s_ref, j):
        for h in range(heads):
            k = k_ref[h, pl.ds(pl.multiple_of(j * tq, tq), tq), :]
            s_ref[h] = lax.dot_general(k, q_ref[h], (((1,), (1,)), ((), ())), preferred_element_type=F32)

    def absorb(s_ref, j, diagonal):
        for h in range(heads):
            s = s_ref[h]
            if diagonal:
                k_chunk = lax.broadcasted_iota(jnp.int32, (tq, tq), 0) >> shift
                q_chunk = lax.broadcasted_iota(jnp.int32, (tq, tq), 1) >> shift
                s = jnp.where(k_chunk <= q_chunk, s, MASK_VALUE)
            m_prev = m_ref[h]
            m_new = jnp.maximum(m_prev, jnp.max(s, axis=0, keepdims=True))
            alpha = jnp.exp2(m_prev - m_new)
            p = jnp.exp2(s - m_new)
            l_ref[h] = alpha * l_ref[h] + jnp.sum(p, axis=0, keepdims=True)
            acc_ref[h] = alpha * acc_ref[h] + jnp.dot(vt_ref[h, j], p.astype(BF16), preferred_element_type=F32)
            m_ref[h] = m_new

    scores(sa_ref, 0)

    def body(t, carry):
        j = 2 * t
        scores(sb_ref, j + 1)
        absorb(sa_ref, j, False)
        scores(sa_ref, j + 2)
        absorb(sb_ref, j + 1, False)
        return carry

    lax.fori_loop(0, i // 2, body, 0)

    @pl.when(i % 2 == 0)
    def _():
        absorb(sa_ref, i, True)

    @pl.when(i % 2 == 1)
    def _():
        scores(sb_ref, i)
        absorb(sa_ref, i - 1, False)
        absorb(sb_ref, i, True)

    for h in range(heads):
        o_ref[:, h * MLA_V:(h + 1) * MLA_V] = (acc_ref[h] / l_ref[h]).T.astype(o_ref.dtype)


def _rope_tables(lb, front):
    pos = (jnp.arange(lb) - front).astype(F32)
    inv = ROPE_BASE ** (-jnp.arange(0, MLA_ROPE, 2, dtype=F32) / MLA_ROPE)
    ang = pos[:, None] * inv[None, :]
    cos, sin = jnp.cos(ang), jnp.sin(ang)
    zeros = jnp.zeros((lb, LANES - MLA_ROPE), F32)
    cc = jnp.concatenate([cos, cos, zeros], axis=1)
    ss = jnp.concatenate([-sin, sin, zeros], axis=1)
    return cc, ss


def _pad_heads(w, width, take):
    k = w.shape[0]
    w = w.reshape(k, MLA_HEADS, width)[:, :, take]
    pad = LANES - w.shape[-1]
    if pad:
        w = jnp.concatenate([w, jnp.zeros((k, MLA_HEADS, pad), w.dtype)], axis=-1)
    return w.reshape(k, MLA_HEADS * LANES).astype(BF16)


def _mla_mixer(proj, nb, lb, front, q_norm, w_uq, kv_norm, w_ukv):
    r = proj.shape[0]
    half = MLA_ROPE // 2
    qw = MLA_NOPE + MLA_ROPE
    rope_cols = jnp.arange(MLA_NOPE, qw)
    swap_cols = jnp.concatenate([rope_cols[half:], rope_cols[:half]])
    wqn = _pad_heads(w_uq, qw, jnp.arange(MLA_NOPE))
    wqr = _pad_heads(w_uq, qw, rope_cols)
    wqs = _pad_heads(w_uq, qw, swap_cols)
    wkn = _pad_heads(w_ukv, MLA_NOPE + MLA_V, jnp.arange(MLA_NOPE))
    wvt = _pad_heads(w_ukv, MLA_NOPE + MLA_V, jnp.arange(MLA_NOPE, MLA_NOPE + MLA_V)).T
    cc, ss = _rope_tables(lb, front)
    scale = qw ** -0.5 * math.log2(math.e)

    tq = _pick(lb, (640, 128))
    nq = lb // tq
    hl = MLA_HEADS * LANES
    lat_w = MLA_Q_RANK + MLA_KV_RANK + 2 * LANES
    const = lambda i: (0, 0)
    q, k, vt = pl.pallas_call(
        functools.partial(_mla_prep_kernel, scale=scale, tiles_per_batch=nq, front=front),
        grid=(r // tq,),
        in_specs=[pl.BlockSpec((tq, lat_w), lambda i: (i, 1)),
                  pl.BlockSpec((tq, LANES), lambda i: (i % nq, 0)),
                  pl.BlockSpec((tq, LANES), lambda i: (i % nq, 0)),
                  pl.BlockSpec((1, MLA_Q_RANK), const),
                  pl.BlockSpec((1, MLA_KV_RANK), const),
                  pl.BlockSpec((MLA_Q_RANK, hl), const),
                  pl.BlockSpec((MLA_Q_RANK, hl), const),
                  pl.BlockSpec((MLA_Q_RANK, hl), const),
                  pl.BlockSpec((MLA_KV_RANK, hl), const),
                  pl.BlockSpec((hl, MLA_KV_RANK), const)],
        out_specs=[pl.BlockSpec((MLA_HEADS, tq, 2 * LANES), lambda i: (0, i, 0)),
                   pl.BlockSpec((MLA_HEADS, tq, 2 * LANES), lambda i: (0, i, 0)),
                   pl.BlockSpec((MLA_HEADS, 1, MLA_V, tq), lambda i: (0, i, 0, 0))],
        out_shape=[jax.ShapeDtypeStruct((MLA_HEADS, r, 2 * LANES), BF16),
                   jax.ShapeDtypeStruct((MLA_HEADS, r, 2 * LANES), BF16),
                   jax.ShapeDtypeStruct((MLA_HEADS, r // tq, MLA_V, tq), BF16)],
        compiler_params=_params("parallel"),
        name="mla_prep",
    )(proj, cc, ss, q_norm.reshape(1, -1), kv_norm.reshape(1, -1), wqn, wqr, wqs, wkn, wvt)

    hs = 2
    return pl.pallas_call(
        functools.partial(_attn_kernel, tq=tq),
        grid=(nb, MLA_HEADS // hs, nq),
        in_specs=[pl.BlockSpec((hs, tq, 2 * LANES), lambda b, h, i: (h, b * nq + i, 0)),
                  pl.BlockSpec((hs, lb, 2 * LANES), lambda b, h, i: (h, b, 0)),
                  pl.BlockSpec((hs, nq, MLA_V, tq), lambda b, h, i: (h, b, 0, 0))],
        out_specs=pl.BlockSpec((tq, hs * MLA_V), lambda b, h, i: (b * nq + i, h)),
        out_shape=jax.ShapeDtypeStruct((r, hl), BF16),
        scratch_shapes=[pltpu.VMEM((hs, tq, tq), F32), pltpu.VMEM((hs, tq, tq), F32),
                        pltpu.VMEM((hs, 1, tq), F32), pltpu.VMEM((hs, 1, tq), F32),
                        pltpu.VMEM((hs, MLA_V, tq), F32)],
        compiler_params=_params("parallel", "parallel", "arbitrary"),
        name="mla_attention",
    )(q, k, vt)


def _proj_conv_kernel(x_ref, xh_ref, w_ref, cw_ref, cb_ref, o_ref, wb_ref, pre_ref, *, tm, lb, front):
    tile = pl.program_id(1)

    @pl.when(tile == 0)
    def _():
        wb_ref[...] = w_ref[...].astype(BF16)

    pre_ref[...] = lax.dot_general(x_ref[...], wb_ref[...], NT_DIMS, preferred_element_type=F32)
    pre = pre_ref[...]
    pre_halo = lax.dot_general(xh_ref[...], wb_ref[...], NT_DIMS, preferred_element_type=F32)
    w = cw_ref[...]
    bias = cb_ref[...]
    k = w.shape[0]

    def taps(x, halo):
        acc = bias + w[k - 1:k, :] * x
        head = bias + w[k - 1:k, :] * x[:SUBLANES]
        sub = lax.broadcasted_iota(jnp.int32, halo.shape, 0)
        for d in range(1, k):
            wd = w[k - 1 - d:k - d, :]
            acc += wd * pltpu.roll(x, d, 0)
            head += wd * jnp.where(sub < d, pltpu.roll(halo, d, 0), pltpu.roll(x[:SUBLANES], d, 0))
        return acc, head

    def silu(v):
        return (v * jax.nn.sigmoid(v)).astype(o_ref.dtype)

    acc, head = taps(pre, pre_halo)
    o_ref[...] = silu(acc)
    o_ref[0:SUBLANES, :] = silu(head)

    hb = -(-(front + k) // (2 * SUBLANES)) * (2 * SUBLANES)
    @pl.when((tile * tm) % lb == 0)
    def _():
        xh = pre_ref[0:hb, :]
        xh = jnp.where(lax.broadcasted_iota(jnp.int32, xh.shape, 0) < front, 0.0, xh)
        acc_h, head_h = taps(xh, jnp.zeros((SUBLANES, xh.shape[1]), F32))
        o_ref[0:hb, :] = silu(acc_h)
        o_ref[0:SUBLANES, :] = silu(head_h)


def _split3(x):
    hi = x.astype(BF16).astype(F32)
    mid = (x - hi).astype(BF16).astype(F32)
    lo = (x - hi - mid).astype(BF16).astype(F32)
    return hi, mid, lo


def _ssd_kernel(xs_ref, bm_ref, cm_ref, dtc_ref, dtr_ref, bc_ref, br_ref, ac_ref, ar_ref,
                d_ref, z_ref, ng_ref, ex_ref, o_ref, state_ref, xdt_ref, *, front):
    c = pl.program_id(1)
    q = xs_ref.shape[0]
    groups, nh = dtr_ref.shape[0], dtr_ref.shape[1]
    hp = xs_ref.shape[1] // groups
    ns = bm_ref.shape[1] // groups
    hd = hp // nh
    wide = dtc_ref.shape[1]

    @pl.when(c == 0)
    def _():
        state_ref[...] = jnp.zeros(state_ref.shape, F32)

    first_live = jnp.where(c == 0, front, 0)
    live_c = lax.broadcasted_iota(jnp.int32, (q, wide), 0) >= first_live
    live_r = lax.broadcasted_iota(jnp.int32, (nh, q), 1) >= first_live
    li = lax.broadcasted_iota(jnp.int32, (q, q), 0)
    si = lax.broadcasted_iota(jnp.int32, (q, q), 1)
    lower = li >= si
    upper16 = (li <= si).astype(BF16)
    lane = lax.broadcasted_iota(jnp.int32, (q, 2 * hd), 1)

    def terms(x, axis):
        return jnp.concatenate(_split3(x), axis=axis).astype(BF16)

    dt_c = jnp.where(live_c, jax.nn.softplus(dtc_ref[...] + bc_ref[0]), 0.0)
    dt_terms = terms(dt_c, 1)
    cs3 = jnp.dot(lower.astype(BF16), terms(dt_c * ac_ref[0], 1), preferred_element_type=F32)
    cs_terms = terms(cs3[:, :wide] + cs3[:, wide:2 * wide] + cs3[:, 2 * wide:], 1)

    for g in range(groups):
        cols = slice(g * hp, (g + 1) * hp)
        expand3 = ex_ref[g]
        dt_full = jnp.dot(dt_terms, expand3, preferred_element_type=F32)
        cs_full = jnp.dot(cs_terms, expand3, preferred_element_type=F32)
        dt_r = jnp.where(live_r, jax.nn.softplus(dtr_ref[g] + br_ref[g]), 0.0)
        csr3 = jnp.dot(terms(dt_r * ar_ref[g], 0), upper16, preferred_element_type=F32)
        cs_r = csr3[:nh] + csr3[nh:2 * nh] + csr3[2 * nh:]
        last = cs_full[q - 1:q, :]

        xdt_ref[g] = xs_ref[:, cols].astype(F32) * dt_full
        bm = bm_ref[:, g * ns:(g + 1) * ns]
        cm = cm_ref[:, g * ns:(g + 1) * ns]
        cb = lax.dot_general(cm, bm, (((1,), (1,)), ((), ())), preferred_element_type=F32)

        prev = state_ref[g]
        y = jnp.dot(cm, prev.astype(BF16), preferred_element_type=F32) * jnp.exp2(cs_full)
        local = lax.dot_general(bm, (xdt_ref[g] * jnp.exp2(last - cs_full)).astype(BF16),
                                (((0,), (0,)), ((), ())), preferred_element_type=F32)
        state_ref[g] = prev * jnp.exp2(last) + local

        diag = []
        for pr in range(nh // 2):
            ms = []
            for h in (2 * pr, 2 * pr + 1):
                seg = cs_full[:, h * hd:h * hd + 1] - cs_r[h:h + 1, :]
                ms.append((cb * jnp.exp2(jnp.where(lower, seg, -jnp.inf))).astype(BF16))
            xp = xdt_ref[g, :, pr * 2 * hd:(pr + 1) * 2 * hd]
            rhs = jnp.concatenate([jnp.where(lane < hd, xp, 0.0), jnp.where(lane >= hd, xp, 0.0)], axis=0)
            diag.append(jnp.dot(jnp.concatenate(ms, axis=1), rhs.astype(BF16), preferred_element_type=F32))
        y = y + jnp.concatenate(diag, axis=1) + d_ref[g] * xs_ref[:, cols].astype(F32)

        z = z_ref[:, cols].astype(F32)
        y = y * (z * jax.nn.sigmoid(z))
        y = y * lax.rsqrt(jnp.mean(y * y, axis=-1, keepdims=True) + RMS_EPS) * ng_ref[:, cols]
        o_ref[:, cols] = y.astype(o_ref.dtype)


def _mamba2_mixer(h16, nb, lb, front, w_in, conv_w, conv_b, dt_bias, a_log, d, norm_g):
    r = h16.shape[0]
    inner = norm_g.shape[0]
    gn = SSD_GROUPS * SSD_STATE
    cdim = inner + 2 * gn
    tm = _pick(r, (1664, 1280, 640, 128))
    nheads = SSD_GROUPS * SSD_HPG
    tn = _pick(math.gcd(inner, cdim), (1024, 512, 128))
    wt = w_in.T
    z = _matmul_ws(h16, wt, BF16, col0=0, ncols=inner, tm=tm, tn=tn, name="l1_in_z")
    tcv = _pick(math.gcd(inner, cdim), (512, 256, 128))
    blocks = tm // SUBLANES
    dm = h16.shape[1]
    c0 = inner // tcv
    assert lb % tm == 0
    act = pl.pallas_call(
        functools.partial(_proj_conv_kernel, tm=tm, lb=lb, front=front),
        grid=(cdim // tcv, r // tm),
        in_specs=[pl.BlockSpec((tm, dm), lambda j, i: (i, 0)),
                  pl.BlockSpec((SUBLANES, dm), lambda j, i: (jnp.maximum(i * blocks - 1, 0), 0)),
                  pl.BlockSpec((tcv, dm), lambda j, i: (c0 + j, 0)),
                  pl.BlockSpec((SSD_CONV, tcv), lambda j, i: (0, j)),
                  pl.BlockSpec((1, tcv), lambda j, i: (0, j))],
        out_specs=pl.BlockSpec((tm, tcv), lambda j, i: (i, j)),
        out_shape=jax.ShapeDtypeStruct((r, cdim), BF16),
        scratch_shapes=[pltpu.VMEM((tcv, dm), BF16), pltpu.VMEM((tm, tcv), F32)],
        compiler_params=_params("parallel", "arbitrary"),
        name="l1_in_xbc_conv",
    )(h16, h16, wt, conv_w, conv_b.reshape(1, cdim))
    assert w_in.shape[1] == inner + cdim + nheads and nheads <= LANES
    dt_raw = _matmul_ws(h16, wt, F32, col0=inner + cdim, ncols=LANES, tm=tm, tn=LANES, name="l1_in_dt")

    gsz = inner // SSD_GROUPS
    nc = lb // SSD_BLOCK
    dt_r = dt_raw[:, :nheads].T.reshape(SSD_GROUPS, SSD_HPG, r)
    lane_pad = jnp.zeros((LANES - nheads,), F32)
    bias = dt_bias.reshape(SSD_GROUPS, SSD_HPG)
    bias_c = jnp.concatenate([dt_bias, lane_pad]).reshape(1, 1, LANES)
    a = -jnp.exp(a_log) * math.log2(math.e)
    a_c = jnp.concatenate([a, lane_pad]).reshape(1, 1, LANES)
    a = a.reshape(SSD_GROUPS, SSD_HPG)
    d_full = jnp.repeat(d, SSD_HEAD_DIM).reshape(SSD_GROUPS, 1, gsz)
    head_of_lane = (jnp.arange(gsz)[None, None, :] // SSD_HEAD_DIM
                    + SSD_HPG * jnp.arange(SSD_GROUPS)[:, None, None])
    expand = (jnp.arange(LANES)[None, :, None] == head_of_lane).astype(BF16)
    expand3 = jnp.concatenate([expand] * 3, axis=1)
    row = lambda b, c: b * nc + c
    whole3 = lambda b, c: (0, 0, 0)
    ng_ = SSD_GROUPS
    return pl.pallas_call(
        functools.partial(_ssd_kernel, front=front),
        grid=(nb, nc),
        in_specs=[pl.BlockSpec((SSD_BLOCK, inner), lambda b, c: (row(b, c), 0)),
                  pl.BlockSpec((SSD_BLOCK, gn), lambda b, c: (row(b, c), inner // gn)),
                  pl.BlockSpec((SSD_BLOCK, gn), lambda b, c: (row(b, c), inner // gn + 1)),
                  pl.BlockSpec((SSD_BLOCK, LANES), lambda b, c: (row(b, c), 0)),
                  pl.BlockSpec((ng_, SSD_HPG, SSD_BLOCK), lambda b, c: (0, 0, row(b, c))),
                  pl.BlockSpec((1, 1, LANES), whole3),
                  pl.BlockSpec((ng_, SSD_HPG, 1), whole3),
                  pl.BlockSpec((1, 1, LANES), whole3),
                  pl.BlockSpec((ng_, SSD_HPG, 1), whole3),
                  pl.BlockSpec((ng_, 1, gsz), whole3),
                  pl.BlockSpec((SSD_BLOCK, inner), lambda b, c: (row(b, c), 0)),
                  pl.BlockSpec((1, inner), lambda b, c: (0, 0)),
                  pl.BlockSpec((ng_, 3 * LANES, gsz), whole3)],
        out_specs=pl.BlockSpec((SSD_BLOCK, inner), lambda b, c: (row(b, c), 0)),
        out_shape=jax.ShapeDtypeStruct((r, inner), BF16),
        scratch_shapes=[pltpu.VMEM((ng_, SSD_STATE, gsz), F32), pltpu.VMEM((ng_, SSD_BLOCK, gsz), F32)],
        compiler_params=_params("parallel", "arbitrary"),
        name="ssd_scan",
    )(act, act, act, dt_raw, dt_r,
      bias_c, bias[:, :, None], a_c, a[:, :, None],
      d_full, z, norm_g.reshape(1, inner), expand3)


def kernel(x, meta_tokens, l0_w_in, l0_s5_log_dt, l0_s5_a_re, l0_s5_a_im, l0_s5_b_re, l0_s5_b_im, l0_s5_c_re, l0_s5_c_im, l0_s5_d, l0_s5_w_glu, l0_mla_q_norm, l0_mla_w_uq, l0_mla_kv_norm, l0_mla_w_ukv, l0_w_out, l0_ln1_g, l0_ln1_b, l0_ffn_w_gate, l0_ffn_w_up, l0_ffn_w_down, l0_ln2_g, l0_ln2_b, l1_w_in, l1_conv_w, l1_conv_b, l1_dt_bias, l1_a_log, l1_d, l1_norm_g, l1_w_out, l1_ln1_g, l1_ln1_b, l1_ffn_w_gate, l1_ffn_w_up, l1_ffn_w_down, l1_ln2_g, l1_ln2_b):
    nb, seq, dm = x.shape
    n_meta = meta_tokens.shape[0]
    front = (-n_meta) % SSD_BLOCK
    lb = front + n_meta + seq
    r = nb * lb
    meta = jnp.broadcast_to(meta_tokens[None].astype(x.dtype), (nb, n_meta, dm))
    h32 = jnp.concatenate([jnp.zeros((nb, front, dm), x.dtype), meta, x], axis=1).reshape(r, dm)

    s5w = l0_s5_d.shape[0]
    o3 = s5w + MLA_Q_RANK + MLA_KV_RANK
    kr = l0_w_in[:, o3:]
    half = MLA_ROPE // 2
    zpad = jnp.zeros((dm, LANES - MLA_ROPE), l0_w_in.dtype)
    w0 = jnp.concatenate([l0_w_in[:, :o3], kr, zpad, kr[:, half:], kr[:, :half], zpad], axis=1).astype(BF16)
    proj = _matmul(h32, w0, F32, tm=_pick(r, (1664, 1280, 640, 128)), tn=_pick(w0.shape[1], (512, 256, 128)),
                   name="l0_in")
    a_out = _s5_mixer(proj, nb, lb, front, l0_s5_log_dt, l0_s5_a_re, l0_s5_a_im, l0_s5_b_re, l0_s5_b_im,
                      l0_s5_c_re, l0_s5_c_im, l0_s5_d, l0_s5_w_glu)
    b_out = _mla_mixer(proj, nb, lb, front, l0_mla_q_norm, l0_mla_w_uq, l0_mla_kv_norm, l0_mla_w_ukv)
    tml = _pick(r, (640, 128))
    h32, h16 = _mm_res_ln((a_out, b_out), l0_w_out.astype(BF16), h32, l0_ln1_g, l0_ln1_b, tm=tml,
                          tk=a_out.shape[1], name="l0_out_ln")
    h32, h16 = _ffn_block(h32, h16, l0_ffn_w_gate, l0_ffn_w_up, l0_ffn_w_down, l0_ln2_g, l0_ln2_b, "l0")

    y = _mamba2_mixer(h16, nb, lb, front, l1_w_in, l1_conv_w, l1_conv_b, l1_dt_bias, l1_a_log, l1_d, l1_norm_g)
    h32, h16 = _mm_res_ln(y, l1_w_out.astype(BF16), h32, l1_ln1_g, l1_ln1_b, tm=tml, tk=2048, name="l1_out_ln")
    h32, h16 = _ffn_block(h32, h16, l1_ffn_w_gate, l1_ffn_w_up, l1_ffn_w_down, l1_ln2_g, l1_ln2_b, "l1")
    return h32.reshape(nb, lb, dm)[:, front + n_meta:, :]
```
